```python
import jax
import jax.numpy as jnp
from jax import lax
import numpy as np

D_MODEL = 1024
BATCH = 4
SEQ = 4096
DEPTH = 1
DEC_BATCH = 128
DEC_SEQ = 1
PAST_LEN = 8192
PAGE_SIZE = 128

HEAD_DIM = 64
N_Q_HEADS = 8
N_KV_HEADS = 2
GROUP = N_Q_HEADS // N_KV_HEADS
NSA_WIDTH = N_Q_HEADS * HEAD_DIM
SCALE = HEAD_DIM ** -0.5
ROPE_DIM = HEAD_DIM // 4
ROPE_THETA = 500000.0
CMP_BLOCK = 32
CMP_STRIDE = 16
CMP_HIDDEN = 64
SEL_BLOCK = 64
N_SEL = 16
WINDOW = 512
Q_BLOCK = 128
FORCE_BONUS = 1.0e4
D_RNN = D_MODEL // 2
RNN_HEADS = 8
RNN_HD = D_RNN // RNN_HEADS
CONV_W = 4
RG_C = 8.0
EPS = 1e-6
SPLIT_SIZES = (NSA_WIDTH, 6 * N_KV_HEADS * HEAD_DIM, 3 * N_Q_HEADS, NSA_WIDTH, D_RNN, D_RNN, 2 * D_MODEL)
D_IN = sum(SPLIT_SIZES)

kernel_name = 'nsa_rglru_gated_hybrid_step'


def _rmsnorm(x, g):
    xf = x.astype(jnp.float32)
    y = xf * lax.rsqrt(jnp.mean(xf * xf, axis=-1, keepdims=True) + EPS)
    return (y * g.astype(jnp.float32)).astype(x.dtype)


def _rope(x, pos):
    half = ROPE_DIM // 2
    inv = ROPE_THETA ** (-jnp.arange(half, dtype=jnp.float32) / half)
    ang = pos.astype(jnp.float32)[:, None] * inv
    ang = ang.reshape(ang.shape[0], *([1] * (x.ndim - 3)), half)
    cos, sin = jnp.cos(ang), jnp.sin(ang)
    x1, x2 = x[..., :half].astype(jnp.float32), x[..., half:ROPE_DIM].astype(jnp.float32)
    rot = jnp.concatenate([x1 * cos - x2 * sin, x2 * cos + x1 * sin], axis=-1).astype(x.dtype)
    return jnp.concatenate([rot, x[..., ROPE_DIM:]], axis=-1)


def _masked_softmax(s, mask):
    s = jnp.where(mask, s.astype(jnp.float32), -jnp.inf)
    m = jnp.max(s, axis=-1, keepdims=True)
    m = jnp.where(jnp.isfinite(m), m, 0.0)
    e = jnp.where(mask, jnp.exp(s - m), 0.0)
    d = jnp.sum(e, axis=-1, keepdims=True)
    return e / jnp.where(d > 0, d, 1.0)


def _compress(rows, pe, w1, w2):
    b, t = rows.shape[:2]
    sub = rows.reshape(b, t // CMP_STRIDE, CMP_STRIDE, N_KV_HEADS, HEAD_DIM)
    lo = jnp.einsum('bsjhd,jde->bshe', sub, w1[:CMP_STRIDE])
    hi = jnp.einsum('bsjhd,jde->bshe', sub, w1[CMP_STRIDE:])
    pos_bias = jnp.einsum('jd,jde->e', pe, w1)
    hid = jax.nn.silu(lo[:, :-1] + hi[:, 1:] + pos_bias)
    return jnp.einsum('bche,ed->bchd', hid, w2)


def _sel_importance(p, ns):
    r = SEL_BLOCK // CMP_STRIDE
    lead = CMP_BLOCK // CMP_STRIDE - 1
    pp = jnp.pad(p, [(0, 0)] * (p.ndim - 1) + [(lead, r)])
    imp = pp[..., :r * ns].reshape(*p.shape[:-1], ns, r).sum(axis=-1)
    for k in range(r, r + lead):
        imp = imp + pp[..., k:k + r * ns:r]
    return imp


def _nsa_cmp_sel(q, q_pos, kc_raw, vc_raw, ks, vs, g_kc, pe_k, w1_k, w2_k, pe_v, w1_v, w2_v):
    b, tq = q.shape[:2]
    tk = ks.shape[1]
    kc = _compress(kc_raw, pe_k, w1_k, w2_k)
    vc = _compress(vc_raw, pe_v, w1_v, w2_v)
    c_end = jnp.arange(kc.shape[1]) * CMP_STRIDE + (CMP_BLOCK - 1)
    kc = _rope(_rmsnorm(kc, g_kc), c_end)
    ns = tk // SEL_BLOCK
    n_top = min(N_SEL, ns)
    ks_b = ks.reshape(b, ns, SEL_BLOCK, N_KV_HEADS, HEAD_DIM).transpose(0, 3, 1, 2, 4)
    vs_b = vs.reshape(b, ns, SEL_BLOCK, N_KV_HEADS, HEAD_DIM).transpose(0, 3, 1, 2, 4)
    bi = jnp.arange(b)[:, None, None, None]
    hi = jnp.arange(N_KV_HEADS)[None, :, None, None]
    sel_off = jnp.arange(SEL_BLOCK)
    blk_ids = jnp.arange(ns)[None, :]
    qb = min(Q_BLOCK, tq)
    nqb = tq // qb
    m_sel = n_top * SEL_BLOCK

    def one_block(args):
        q_blk, pos = args
        s_c = jnp.einsum('bqhgd,bchd->bhgqc', q_blk, kc) * SCALE
        p_c = _masked_softmax(s_c, c_end[None, :] <= pos[:, None])
        o_c = jnp.einsum('bhgqc,bchd->bqhgd', p_c, vc)
        imp = _sel_importance(p_c.sum(axis=2), ns)
        cur = pos[:, None] // SEL_BLOCK
        forced = (blk_ids == 0) | (blk_ids == cur) | (blk_ids == cur - 1)
        _, idx = lax.top_k(imp + jnp.where(forced, FORCE_BONUS, 0.0), n_top)
        k_g = ks_b[bi, hi, idx]
        v_g = vs_b[bi, hi, idx].reshape(b, N_KV_HEADS, qb, m_sel, HEAD_DIM)
        tok = idx[..., None] * SEL_BLOCK + sel_off
        mask = (tok <= pos[:, None, None]).reshape(b, N_KV_HEADS, 1, qb, m_sel)
        s_s = jnp.einsum('bqhgd,bhqkld->bhgqkl', q_blk, k_g).reshape(b, N_KV_HEADS, GROUP, qb, m_sel) * SCALE
        p_s = _masked_softmax(s_s, mask)
        o_s = jnp.einsum('bhgqm,bhqmd->bqhgd', p_s, v_g)
        return o_c, o_s

    qs = q.reshape(b, nqb, qb, N_KV_HEADS, GROUP, HEAD_DIM).swapaxes(0, 1)
    o_c, o_s = lax.map(one_block, (qs, q_pos.reshape(nqb, qb)))
    shape = (b, tq, N_KV_HEADS, GROUP, HEAD_DIM)
    return o_c.swapaxes(0, 1).reshape(shape), o_s.swapaxes(0, 1).reshape(shape)


def _window_attn(q, k, v, q_pos, k_pos):
    s = jnp.einsum('...qhgd,...khd->...hgqk', q, k) * SCALE
    d = q_pos[..., :, None] - k_pos[..., None, :]
    mask = (d >= 0) & (d <= WINDOW) & (k_pos[..., None, :] >= 0)
    p = _masked_softmax(s, mask[..., None, None, :, :])
    return jnp.einsum('...hgqk,...khd->...qhgd', p, v)


def _prompt_window(q, k, v):
    b, t = q.shape[:2]
    nqb = t // Q_BLOCK
    nback = WINDOW // Q_BLOCK

    def band(a):
        ab = a.reshape(b, nqb, Q_BLOCK, N_KV_HEADS, HEAD_DIM)
        ab = jnp.pad(ab, ((0, 0), (nback, 0), (0, 0), (0, 0), (0, 0)))
        return jnp.concatenate([ab[:, i:i + nqb] for i in range(nback + 1)], axis=2)

    k_pos = (jnp.arange(nqb)[:, None] - nback) * Q_BLOCK + jnp.arange((nback + 1) * Q_BLOCK)[None, :]
    q_pos = jnp.arange(t).reshape(nqb, Q_BLOCK)
    qb = q.reshape(b, nqb, Q_BLOCK, N_KV_HEADS, GROUP, HEAD_DIM)
    o = _window_attn(qb, band(k), band(v), q_pos, k_pos)
    return o.reshape(b, t, N_KV_HEADS, GROUP, HEAD_DIM)


def _causal_conv(x, buf, w, bias):
    t = x.shape[1]
    xp = jnp.concatenate([buf.astype(x.dtype), x], axis=1)
    y = bias
    for k in range(CONV_W):
        y = y + xp[:, k:k + t] * w[k]
    return y.astype(x.dtype), xp[:, -(CONV_W - 1):]


def _rglru(xc, h0, w_ra, b_ra, w_rx, b_rx, lam):
    b, t = xc.shape[:2]
    xf = xc.astype(jnp.float32)
    xh = xf.reshape(b, t, RNN_HEADS, RNN_HD)
    r = jax.nn.sigmoid(jnp.einsum('bthi,hij->bthj', xh, w_ra) + b_ra).reshape(b, t, D_RNN)
    i = jax.nn.sigmoid(jnp.einsum('bthi,hij->bthj', xh, w_rx) + b_rx).reshape(b, t, D_RNN)
    log_a = -RG_C * r * jax.nn.softplus(-lam.astype(jnp.float32))
    a = jnp.exp(log_a)
    u = jnp.sqrt(-jnp.expm1(2.0 * log_a)) * (i * xf)

    def step(h, au):
        h = au[0] * h + au[1]
        return h, h

    h_last, hs = lax.scan(step, h0.astype(jnp.float32), (a.swapaxes(0, 1), u.swapaxes(0, 1)))
    return hs.swapaxes(0, 1).astype(xc.dtype), h_last


def _split_cols(p):
    out, start = [], 0
    for n in SPLIT_SIZES:
        out.append(p[..., start:start + n])
        start += n
    return out


def _mixer_inputs(x, pos, g_norm, w_in, g_q, g_ks, g_kw):
    b, t = x.shape[:2]
    u = _rmsnorm(x, g_norm)
    q, kv, g_nsa, z_nsa, x_rnn, z_rnn, g_mrg = _split_cols(u @ w_in)
    q = _rope(_rmsnorm(q.reshape(b, t, N_KV_HEADS, GROUP, HEAD_DIM), g_q), pos)
    kv = kv.reshape(b, t, 3, 2, N_KV_HEADS, HEAD_DIM)
    kc, vc = kv[:, :, 0, 0], kv[:, :, 0, 1]
    ks = _rope(_rmsnorm(kv[:, :, 1, 0], g_ks), pos)
    vs = kv[:, :, 1, 1]
    kw = _rope(_rmsnorm(kv[:, :, 2, 0], g_kw), pos)
    vw = kv[:, :, 2, 1]
    return q, kc, vc, ks, vs, kw, vw, g_nsa, z_nsa, x_rnn, z_rnn, g_mrg


def _mixer_output(x, o_cmp, o_sel, o_win, g_nsa, z_nsa, h_rnn, z_rnn, g_mrg, w_pa, w_pb, w_out):
    b, t = x.shape[:2]
    gb = jax.nn.sigmoid(g_nsa.astype(jnp.float32)).reshape(b, t, N_KV_HEADS, GROUP, 3)
    o = gb[..., 0:1] * o_cmp + gb[..., 1:2] * o_sel + gb[..., 2:3] * o_win
    a_br = (o.reshape(b, t, NSA_WIDTH) * jax.nn.silu(z_nsa.astype(jnp.float32))).astype(x.dtype)
    r_br = (h_rnn.astype(jnp.float32) * jax.nn.silu(z_rnn.astype(jnp.float32))).astype(x.dtype)
    gm = jax.nn.sigmoid(g_mrg.astype(jnp.float32))
    merged = gm[..., :D_MODEL] * (a_br @ w_pa) + gm[..., D_MODEL:] * (r_br @ w_pb)
    return (x + merged.astype(x.dtype) @ w_out).astype(x.dtype)


def _paged_rows(cache, page_table, new_rows):
    nb = page_table.shape[0]
    past = cache[page_table].reshape(nb, -1, 2, N_KV_HEADS, HEAD_DIM)
    t = past.shape[1] + new_rows.shape[1]
    t_pad = -(-t // SEL_BLOCK) * SEL_BLOCK
    pad = jnp.zeros((nb, t_pad - t, 2, N_KV_HEADS, HEAD_DIM), past.dtype)
    rows = jnp.concatenate([past, new_rows.astype(past.dtype), pad], axis=1)
    return rows[:, :, 0], rows[:, :, 1]


def setup_inputs(seed: int = 0) -> dict:
    key = jax.random.key(seed)
    k = jax.random.split(key, 32)
    f32 = jnp.float32

    def nrm(kk, shape, scale=1.0):
        return jax.random.normal(kk, shape, f32) * scale

    n_pages = PAST_LEN // PAGE_SIZE
    n_used = DEC_BATCH * n_pages
    n_pool = n_used + max(1, n_used // 4)
    wbuf = min(WINDOW, PAST_LEN)
    page_table = jax.random.permutation(k[5], n_pool)[:n_used].reshape(DEC_BATCH, n_pages).astype(jnp.int32)
    ua = jax.random.uniform(k[20], (D_RNN,), f32, 0.9, 0.999)
    sa = ua ** (1.0 / RG_C)
    lam = jnp.log(sa) - jnp.log1p(-sa)
    return {
        'x_prompt': nrm(k[0], (BATCH, SEQ, D_MODEL)),
        'x_sample': nrm(k[1], (DEC_BATCH, DEC_SEQ, D_MODEL)),
        'cache_kv_cmp': nrm(k[2], (n_pool, PAGE_SIZE, 2, N_KV_HEADS, HEAD_DIM)),
        'cache_kv_sel': nrm(k[3], (n_pool, PAGE_SIZE, 2, N_KV_HEADS, HEAD_DIM)),
        'cache_kv_win': nrm(k[4], (DEC_BATCH, wbuf, 2, N_KV_HEADS, HEAD_DIM)),
        'state_conv': nrm(k[6], (DEC_BATCH, CONV_W - 1, D_RNN)),
        'state_h': nrm(k[7], (DEC_BATCH, D_RNN), 0.5),
        'page_table': page_table,
        'g_norm': 1.0 + nrm(k[8], (D_MODEL,), 0.02),
        'w_in': nrm(k[9], (D_MODEL, D_IN), D_MODEL ** -0.5),
        'g_q': 1.0 + nrm(k[10], (HEAD_DIM,), 0.02),
        'g_kc': 1.0 + nrm(k[11], (HEAD_DIM,), 0.02),
        'g_ks': 1.0 + nrm(k[12], (HEAD_DIM,), 0.02),
        'g_kw': 1.0 + nrm(k[13], (HEAD_DIM,), 0.02),
        'pe_k': nrm(k[14], (CMP_BLOCK, HEAD_DIM), 0.1),
        'w1_k': nrm(k[15], (CMP_BLOCK, HEAD_DIM, CMP_HIDDEN), (CMP_BLOCK * HEAD_DIM) ** -0.5),
        'w2_k': nrm(k[16], (CMP_HIDDEN, HEAD_DIM), CMP_HIDDEN ** -0.5),
        'pe_v': nrm(k[17], (CMP_BLOCK, HEAD_DIM), 0.1),
        'w1_v': nrm(k[18], (CMP_BLOCK, HEAD_DIM, CMP_HIDDEN), (CMP_BLOCK * HEAD_DIM) ** -0.5),
        'w2_v': nrm(k[19], (CMP_HIDDEN, HEAD_DIM), CMP_HIDDEN ** -0.5),
        'conv_w': nrm(k[21], (CONV_W, D_RNN), CONV_W ** -0.5),
        'conv_b': nrm(k[22], (D_RNN,), 0.01),
        'w_ra': nrm(k[23], (RNN_HEADS, RNN_HD, RNN_HD), RNN_HD ** -0.5),
        'b_ra': nrm(k[24], (RNN_HEADS, RNN_HD), 0.01),
        'w_rx': nrm(k[25], (RNN_HEADS, RNN_HD, RNN_HD), RNN_HD ** -0.5),
        'b_rx': nrm(k[26], (RNN_HEADS, RNN_HD), 0.01),
        'lam': lam,
        'w_pa': nrm(k[27], (NSA_WIDTH, D_MODEL), NSA_WIDTH ** -0.5),
        'w_pb': nrm(k[28], (D_RNN, D_MODEL), D_RNN ** -0.5),
        'w_out': nrm(k[29], (D_MODEL, D_MODEL), D_MODEL ** -0.5),
    }


def reference(x_prompt, x_sample, cache_kv_cmp, cache_kv_sel, cache_kv_win, state_conv, state_h, page_table,
              g_norm, w_in, g_q, g_kc, g_ks, g_kw, pe_k, w1_k, w2_k, pe_v, w1_v, w2_v,
              conv_w, conv_b, w_ra, b_ra, w_rx, b_rx, lam, w_pa, w_pb, w_out):
    b_p, t_p = x_prompt.shape[:2]
    pos_p = jnp.arange(t_p)
    (q_p, kc_p, vc_p, ks_p, vs_p, kw_p, vw_p,
     gn_p, zn_p, xr_p, zr_p, gm_p) = _mixer_inputs(x_prompt, pos_p, g_norm, w_in, g_q, g_ks, g_kw)
    o_cmp_p, o_sel_p = _nsa_cmp_sel(q_p, pos_p, kc_p, vc_p, ks_p, vs_p, g_kc,
                                    pe_k, w1_k, w2_k, pe_v, w1_v, w2_v)
    o_win_p = _prompt_window(q_p, kw_p, vw_p)
    xc_p, conv_prompt = _causal_conv(xr_p, jnp.zeros((b_p, CONV_W - 1, D_RNN), xr_p.dtype), conv_w, conv_b)
    hs_p, h_prompt = _rglru(xc_p, jnp.zeros((b_p, D_RNN), jnp.float32), w_ra, b_ra, w_rx, b_rx, lam)
    y_prompt = _mixer_output(x_prompt, o_cmp_p, o_sel_p, o_win_p, gn_p, zn_p, hs_p, zr_p, gm_p, w_pa, w_pb, w_out)
    kv_cmp_prompt = jnp.stack([kc_p, vc_p], axis=2)
    kv_sel_prompt = jnp.stack([ks_p, vs_p], axis=2)
    kv_win_prompt = jnp.stack([kw_p, vw_p], axis=2)[:, -min(WINDOW, t_p):]

    pos_s = PAST_LEN + jnp.arange(x_sample.shape[1])
    (q_s, kc_s, vc_s, ks_s, vs_s, kw_s, vw_s,
     gn_s, zn_s, xr_s, zr_s, gm_s) = _mixer_inputs(x_sample, pos_s, g_norm, w_in, g_q, g_ks, g_kw)
    kv_cmp_sample = jnp.stack([kc_s, vc_s], axis=2)
    kv_sel_sample = jnp.stack([ks_s, vs_s], axis=2)
    kc_all, vc_all = _paged_rows(cache_kv_cmp, page_table, kv_cmp_sample)
    ks_all, vs_all = _paged_rows(cache_kv_sel, page_table, kv_sel_sample)
    o_cmp_s, o_sel_s = _nsa_cmp_sel(q_s, pos_s, kc_all, vc_all, ks_all, vs_all, g_kc,
                                    pe_k, w1_k, w2_k, pe_v, w1_v, w2_v)
    wbuf = cache_kv_win.shape[1]
    win_rows = jnp.concatenate([cache_kv_win, jnp.stack([kw_s, vw_s], axis=2).astype(cache_kv_win.dtype)], axis=1)
    k_pos_w = PAST_LEN - wbuf + jnp.arange(win_rows.shape[1])
    o_win_s = _window_attn(q_s, win_rows[:, :, 0], win_rows[:, :, 1], pos_s, k_pos_w)
    kv_win_sample = win_rows[:, -min(WINDOW, win_rows.shape[1]):]
    xc_s, conv_sample = _causal_conv(xr_s, state_conv, conv_w, conv_b)
    hs_s, h_sample = _rglru(xc_s, state_h, w_ra, b_ra, w_rx, b_rx, lam)
    y_sample = _mixer_output(x_sample, o_cmp_s, o_sel_s, o_win_s, gn_s, zn_s, hs_s, zr_s, gm_s, w_pa, w_pb, w_out)

    return (y_prompt, y_sample, kv_cmp_prompt, kv_cmp_sample, kv_sel_prompt, kv_sel_sample,
            kv_win_prompt, kv_win_sample, conv_prompt, conv_sample, h_prompt, h_sample)
```

```python
import functools

import numpy as np
import jax
import jax.numpy as jnp
from jax import lax
from jax.experimental import pallas as pl
from jax.experimental.pallas import tpu as pltpu

D_MODEL = 1024
HEAD_DIM = 64
N_Q_HEADS = 8
N_KV_HEADS = 2
GROUP = N_Q_HEADS // N_KV_HEADS
NSA_WIDTH = N_Q_HEADS * HEAD_DIM
SCALE = HEAD_DIM ** -0.5
ROPE_DIM = HEAD_DIM // 4
ROPE_HALF = ROPE_DIM // 2
ROPE_THETA = 500000.0
CMP_BLOCK = 32
CMP_STRIDE = 16
CMP_HIDDEN = 64
SEL_BLOCK = 64
N_SEL = 16
WINDOW = 512
Q_BLOCK = 128
FORCE_BONUS = 1.0e4
D_RNN = D_MODEL // 2
RNN_HEADS = 8
RNN_HD = D_RNN // RNN_HEADS
CONV_W = 4
RG_C = 8.0
EPS = 1e-6
PAGE_SIZE = 128

LANES = 128
KV_COLS = 2 * N_KV_HEADS * HEAD_DIM
NEG = -(2.0 ** 100)
VMEM_LIMIT = 56 * 1024 * 1024
F32 = jnp.float32
BF16 = jnp.bfloat16

C_Q = 0
C_KV = C_Q + NSA_WIDTH
C_ZN = C_KV + 3 * KV_COLS
C_XR = C_ZN + NSA_WIDTH
C_ZR = C_XR + D_RNN
C_GM = C_ZR + D_RNN
C_GN = C_GM + 2 * D_MODEL
N_WCOLS = C_GN + LANES


def _dot(a, b):
    return jnp.dot(a, b, preferred_element_type=F32)


def _dot_nt(a, b):
    return lax.dot_general(a, b, (((1,), (1,)), ((), ())), preferred_element_type=F32)


def _split3(a):
    a1 = a.astype(BF16)
    r1 = a - a1.astype(F32)
    a2 = r1.astype(BF16)
    a3 = (r1 - a2.astype(F32)).astype(BF16)
    return a1, a2, a3


def _dot_exact01(a, b01):
    a1, a2, a3 = _split3(a)
    return _dot(a1, b01) + _dot(a2, b01) + _dot(a3, b01)


def _dot_nt_exact01(b01, a):
    a1, a2, a3 = _split3(a)
    return _dot_nt(b01, a1) + _dot_nt(b01, a2) + _dot_nt(b01, a3)


def _sigmoid(x):
    return 1.0 / (1.0 + jnp.exp(-x))


def _silu(x):
    return x * _sigmoid(x)


def _headnorm_rope(xs, gain, cos, s1, s2, seg01):
    ss = _dot_exact01(xs * xs, seg01)
    y = xs * lax.rsqrt(ss * (1.0 / HEAD_DIM) + EPS) * gain
    return y * cos + pltpu.roll(y, ROPE_HALF, 1) * s1 + pltpu.roll(y, LANES - ROPE_HALF, 1) * s2


def _masked_softmax(s, mask):
    s = jnp.where(mask, s, -jnp.inf)
    m = jnp.max(s, axis=-1, keepdims=True)
    m = jnp.where(m == -jnp.inf, 0.0, m)
    e = jnp.where(mask, jnp.exp(s - m), 0.0)
    d = jnp.sum(e, axis=-1, keepdims=True)
    return e / jnp.where(d > 0, d, 1.0)


def _rope_tables(pos):
    pos = np.asarray(pos, np.float64)
    inv = ROPE_THETA ** (-np.arange(ROPE_HALF, dtype=np.float64) / ROPE_HALF)
    ang = (pos.astype(np.float32)[:, None] * inv.astype(np.float32)[None, :]).astype(np.float32).astype(np.float64)
    cos, sin = np.cos(ang), np.sin(ang)
    n = pos.shape[0]
    c = np.ones((n, HEAD_DIM)); s1 = np.zeros((n, HEAD_DIM)); s2 = np.zeros((n, HEAD_DIM))
    c[:, :ROPE_HALF] = cos; c[:, ROPE_HALF:ROPE_DIM] = cos
    s1[:, ROPE_HALF:ROPE_DIM] = sin
    s2[:, :ROPE_HALF] = -sin
    t = lambda a: jnp.asarray(np.tile(a, (1, 2)), F32)
    return t(c), t(s1), t(s2)


def _seg01():
    lane = np.arange(LANES)
    return jnp.asarray((lane[:, None] // HEAD_DIM) == (lane[None, :] // HEAD_DIM), BF16)


def _imp_matrix(nc_pad, ns_pad, nc, ns):
    r = SEL_BLOCK // CMP_STRIDE
    lead = CMP_BLOCK // CMP_STRIDE - 1
    c = np.arange(nc_pad)[:, None]
    j = np.arange(ns_pad)[None, :]
    m = (c >= r * j - lead) & (c <= r * j + r - 1) & (c < nc) & (j < ns)
    return m


def _inproj_body(x_ref, gn_ref, w_ref, cos_ref, s1_ref, s2_ref, oh_ref, gq_ref, gks_ref, gkw_ref, seg_ref,
                 kvc_ref, kvs_ref, kvw_ref, qx_ref, qf_ref, ka_ref, vs_ref, kw_ref, vw_ref,
                 zn_ref, xr_ref, zr_ref, gm_ref, gate_ref):
    x = x_ref[...]
    r = lax.rsqrt(jnp.mean(x * x, axis=-1, keepdims=True) + EPS)
    u = (x * r * gn_ref[...]).astype(BF16)
    cos, s1, s2, seg = cos_ref[...], s1_ref[...], s2_ref[...], seg_ref[...]
    lane = lax.broadcasted_iota(jnp.int32, (1, LANES), 1)

    def proj(c0, n):
        return _dot(u, w_ref[:, c0:c0 + n])

    q = proj(C_Q, NSA_WIDTH)
    for s in range(NSA_WIDTH // LANES):
        qs = _headnorm_rope(q[:, s * LANES:(s + 1) * LANES], gq_ref[...], cos, s1, s2, seg) * SCALE
        qf_ref[:, s * LANES:(s + 1) * LANES] = qs
        qs_sw = pltpu.roll(qs, HEAD_DIM, 1)
        for half in range(2):
            i = 2 * s + half
            h = i // GROUP
            src = qs if half == h else qs_sw
            keep = (lane >= h * HEAD_DIM) & (lane < (h + 1) * HEAD_DIM)
            qx_ref[:, i * LANES:(i + 1) * LANES] = jnp.where(keep, src, 0.0).astype(BF16)

    kv = proj(C_KV, 3 * KV_COLS)
    kvc_ref[...] = kv[:, :KV_COLS]
    ks = _headnorm_rope(kv[:, KV_COLS:KV_COLS + LANES], gks_ref[...], cos, s1, s2, seg)
    vs = kv[:, KV_COLS + LANES:2 * KV_COLS]
    kvs_ref[:, :LANES] = ks
    kvs_ref[:, LANES:] = vs
    ka_ref[:, :LANES] = ks.astype(BF16)
    ka_ref[:, LANES:] = oh_ref[...]
    vs_ref[...] = vs.astype(BF16)
    kw = _headnorm_rope(kv[:, 2 * KV_COLS:2 * KV_COLS + LANES], gkw_ref[...], cos, s1, s2, seg)
    vw = kv[:, 2 * KV_COLS + LANES:]
    kvw_ref[:, :LANES] = kw
    kvw_ref[:, LANES:] = vw
    kw_ref[...] = kw.astype(BF16)
    vw_ref[...] = vw.astype(BF16)

    zn_ref[...] = _silu(proj(C_ZN, NSA_WIDTH))
    xr_ref[...] = proj(C_XR, D_RNN)
    zr_ref[...] = _silu(proj(C_ZR, D_RNN))
    for c in range(2):
        gm_ref[:, c * D_MODEL:(c + 1) * D_MODEL] = _sigmoid(proj(C_GM + c * D_MODEL, D_MODEL))
    gate_ref[...] = proj(C_GN, LANES)


def _inproj(x2d, gnorm, wp, tabs, oh, gq, gks, gkw, seg, tm, n_tab_blocks):
    rows = x2d.shape[0]
    grid = (rows // tm,)
    row = lambda n: pl.BlockSpec((tm, n), lambda i: (i, 0))
    tab = pl.BlockSpec((tm, LANES), lambda i: (i % n_tab_blocks, 0))
    const = lambda a: pl.BlockSpec(a.shape, lambda i: (0,) * a.ndim)
    out_defs = [(KV_COLS, F32), (KV_COLS, F32), (KV_COLS, F32), (N_Q_HEADS * LANES, BF16), (NSA_WIDTH, F32),
                (2 * LANES, BF16), (LANES, BF16), (LANES, BF16), (LANES, BF16),
                (NSA_WIDTH, F32), (D_RNN, F32), (D_RNN, F32), (2 * D_MODEL, F32), (LANES, F32)]
    return pl.pallas_call(
        _inproj_body,
        grid=grid,
        in_specs=[row(D_MODEL), const(gnorm), const(wp), tab, tab, tab, tab, const(gq), const(gks), const(gkw), const(seg)],
        out_specs=[row(n) for n, _ in out_defs],
        out_shape=[jax.ShapeDtypeStruct((rows, n), dt) for n, dt in out_defs],
        compiler_params=pltpu.CompilerParams(dimension_semantics=("arbitrary",), vmem_limit_bytes=VMEM_LIMIT),
        name="inproj",
    )(x2d, gnorm, wp, *tabs, oh, gq, gks, gkw, seg)


def _compress_rows(load_rows, n_sub, wbd_ref, w2_ref, pe_ref, w1t_ref):
    acc = jnp.zeros((n_sub, 2 * LANES), F32)
    for j in range(CMP_STRIDE):
        acc = acc + _dot(load_rows(j).astype(BF16), wbd_ref[j])
    lo, hi = acc[:, :LANES], acc[:, LANES:]
    pos_bias = jnp.sum(pe_ref[...] * w1t_ref[...], axis=0, keepdims=True)
    hid = _silu(lo + pltpu.roll(hi, n_sub - 1, 0) + pos_bias)
    return _dot(hid.astype(BF16), w2_ref[...])


def _gate_col(gate_ref, col):
    return _sigmoid(gate_ref[:, col:col + 1])


def _prompt_cmp_body(kraw_ref, vraw_ref, qx_ref, gate_ref, wbdk_ref, w2k_ref, pek_ref, w1tk_ref, wbdv_ref, w2v_ref, pev_ref,
                     w1tv_ref, gkc_ref, cos_ref, s1_ref, s2_ref, seg_ref, mimp_ref,
                     oc_ref, ni_ref, kc_scr, vc_scr, *, n_sub, ns, n_top):
    qi = pl.program_id(1)

    @pl.when(qi == 0)
    def _():
        def rows(ref):
            return lambda j: ref[0, pl.ds(j, n_sub, stride=CMP_STRIDE), :]
        kc = _compress_rows(rows(kraw_ref), n_sub, wbdk_ref, w2k_ref, pek_ref, w1tk_ref)
        kc = _headnorm_rope(kc, gkc_ref[...], cos_ref[...], s1_ref[...], s2_ref[...], seg_ref[...])
        kc_scr[...] = kc.astype(BF16)
        vc_scr[...] = _compress_rows(rows(vraw_ref), n_sub, wbdv_ref, w2v_ref, pev_ref, w1tv_ref).astype(BF16)

    rq = GROUP * Q_BLOCK
    qpos_r = qi * Q_BLOCK + lax.broadcasted_iota(jnp.int32, (rq, 1), 0) % Q_BLOCK
    c_end = lax.broadcasted_iota(jnp.int32, (1, n_sub), 1) * CMP_STRIDE + (CMP_BLOCK - 1)
    cmask = c_end <= qpos_r
    qpos_l = qi * Q_BLOCK + lax.broadcasted_iota(jnp.int32, (1, Q_BLOCK), 1)
    jrow = lax.broadcasted_iota(jnp.int32, (ns, 1), 0)
    cur = qpos_l // SEL_BLOCK
    forced = (jrow == 0) | (jrow == cur) | (jrow == cur - 1)
    lane = lax.broadcasted_iota(jnp.int32, (1, LANES), 1)
    outs, negs = [], []
    for h in range(N_KV_HEADS):
        qh = jnp.concatenate([qx_ref[:, (h * GROUP + g) * LANES:(h * GROUP + g + 1) * LANES] for g in range(GROUP)], axis=0)
        p = _masked_softmax(_dot_nt(qh, kc_scr[...]), cmask)
        o = _dot(p.astype(BF16), vc_scr[...])
        gcol = jnp.concatenate([_gate_col(gate_ref, (h * GROUP + g) * 3) for g in range(GROUP)], axis=0)
        outs.append(o * gcol)
        psum = (p[0:Q_BLOCK] + p[Q_BLOCK:2 * Q_BLOCK]) + p[2 * Q_BLOCK:3 * Q_BLOCK] + p[3 * Q_BLOCK:]
        v = _dot_nt_exact01(mimp_ref[...], psum) + jnp.where(forced, FORCE_BONUS, 0.0)
        rank = jnp.zeros((ns, Q_BLOCK), F32)
        for i in range(ns):
            vi = v[i:i + 1, :]
            before = (vi > v) | ((vi == v) & (jrow > i))
            rank = rank + jnp.where(before, 1.0, 0.0)
        neg = jnp.where(rank < n_top, 0.0, NEG)
        if ns < HEAD_DIM:
            neg = jnp.concatenate([neg, jnp.zeros((HEAD_DIM - ns, Q_BLOCK), F32)], axis=0)
        negs.append(neg)
    ni_ref[...] = jnp.concatenate(negs, axis=0).T.astype(BF16)
    oc_ref[0] = jnp.where(lane < HEAD_DIM, outs[0], outs[1]).reshape(GROUP, Q_BLOCK, LANES)


def _prompt_cmp(kvc, qx, gate, cw, gkc, ctabs, seg, mimp, b, t):
    n_sub = t // CMP_STRIDE
    ns = t // SEL_BLOCK
    n_top = min(N_SEL, ns)
    nq = t // Q_BLOCK
    const = lambda a: pl.BlockSpec(a.shape, lambda bi, qi: (0,) * a.ndim)
    qrow = lambda n: pl.BlockSpec((Q_BLOCK, n), lambda bi, qi: (bi * nq + qi, 0))
    consts = [*cw, gkc, *ctabs, seg, mimp]
    return pl.pallas_call(
        functools.partial(_prompt_cmp_body, n_sub=n_sub, ns=ns, n_top=n_top),
        grid=(b, nq),
        in_specs=[pl.BlockSpec((1, t, LANES), lambda bi, qi: (bi, 0, 0)), pl.BlockSpec((1, t, LANES), lambda bi, qi: (bi, 0, 1)),
                  qrow(N_Q_HEADS * LANES), qrow(LANES)] + [const(a) for a in consts],
        out_specs=[pl.BlockSpec((1, GROUP, Q_BLOCK, LANES), lambda bi, qi: (bi, 0, qi, 0)), qrow(LANES)],
        out_shape=[jax.ShapeDtypeStruct((b, GROUP, t, LANES), F32), jax.ShapeDtypeStruct((b * t, LANES), BF16)],
        scratch_shapes=[pltpu.VMEM((n_sub, LANES), BF16), pltpu.VMEM((n_sub, LANES), BF16)],
        compiler_params=pltpu.CompilerParams(dimension_semantics=("arbitrary", "arbitrary"), vmem_limit_bytes=VMEM_LIMIT),
        name="prompt_cmp",
    )(kvc.reshape(b, t, KV_COLS), kvc.reshape(b, t, KV_COLS), qx, gate, *consts)


SEL_TK = 256


def _prompt_sel_body(qx_ref, ni_ref, gate_ref, ka_ref, v_ref, o_ref, qa_scr, m_scr, l_scr, acc_scr):
    qi, kt = pl.program_id(1), pl.program_id(2)
    last = (qi * Q_BLOCK + Q_BLOCK - 1) // SEL_TK
    rq = GROUP * Q_BLOCK
    lane = lax.broadcasted_iota(jnp.int32, (1, LANES), 1)

    @pl.when(kt == 0)
    def _():
        ni = ni_ref[...]
        for h in range(N_KV_HEADS):
            keep = (lane >= h * HEAD_DIM) & (lane < (h + 1) * HEAD_DIM)
            nih = jnp.where(keep, ni, jnp.zeros_like(ni))
            for g in range(GROUP):
                i = h * GROUP + g
                qa_scr[h, g * Q_BLOCK:(g + 1) * Q_BLOCK, :LANES] = qx_ref[:, i * LANES:(i + 1) * LANES]
                qa_scr[h, g * Q_BLOCK:(g + 1) * Q_BLOCK, LANES:] = nih
        m_scr[...] = jnp.full(m_scr.shape, -jnp.inf, F32)
        l_scr[...] = jnp.zeros(l_scr.shape, F32)
        acc_scr[...] = jnp.zeros(acc_scr.shape, F32)

    @pl.when(kt <= last)
    def _():
        qpos = qi * Q_BLOCK + lax.broadcasted_iota(jnp.int32, (rq, 1), 0) % Q_BLOCK
        kpos = kt * SEL_TK + lax.broadcasted_iota(jnp.int32, (1, SEL_TK), 1)
        causal = kpos <= qpos
        for h in range(N_KV_HEADS):
            s = jnp.where(causal, _dot_nt(qa_scr[h], ka_ref[...]), NEG)
            m_old = m_scr[h]
            m_new = jnp.maximum(m_old, jnp.max(s, axis=-1, keepdims=True))
            alpha = jnp.exp(m_old - m_new)
            p = jnp.exp(s - m_new)
            l_scr[h] = alpha * l_scr[h] + jnp.sum(p, axis=-1, keepdims=True)
            acc_scr[h] = alpha * acc_scr[h] + _dot(p.astype(BF16), v_ref[...])
            m_scr[h] = m_new

    @pl.when(kt == last)
    def _():
        outs = []
        for h in range(N_KV_HEADS):
            gcol = jnp.concatenate([_gate_col(gate_ref, (h * GROUP + g) * 3 + 1) for g in range(GROUP)], axis=0)
            outs.append(acc_scr[h] / l_scr[h] * gcol)
        o_ref[0] = jnp.where(lane < HEAD_DIM, outs[0], outs[1]).reshape(GROUP, Q_BLOCK, LANES)


def _prompt_sel(qx, ni, gate, ka, vs, b, t):
    nq, nkt = t // Q_BLOCK, t // SEL_TK
    rq = GROUP * Q_BLOCK
    qrow = lambda n: pl.BlockSpec((Q_BLOCK, n), lambda bi, qi, kt: (bi * nq + qi, 0))
    krow = lambda n: pl.BlockSpec(
        (SEL_TK, n), lambda bi, qi, kt: (bi * nkt + jnp.minimum(kt, (qi * Q_BLOCK + Q_BLOCK - 1) // SEL_TK), 0))
    return pl.pallas_call(
        _prompt_sel_body,
        grid=(b, nq, nkt),
        in_specs=[qrow(N_Q_HEADS * LANES), qrow(LANES), qrow(LANES), krow(2 * LANES), krow(LANES)],
        out_specs=pl.BlockSpec((1, GROUP, Q_BLOCK, LANES), lambda bi, qi, kt: (bi, 0, qi, 0)),
        out_shape=jax.ShapeDtypeStruct((b, GROUP, t, LANES), F32),
        scratch_shapes=[pltpu.VMEM((N_KV_HEADS, rq, 2 * LANES), BF16), pltpu.VMEM((N_KV_HEADS, rq, 1), F32),
                        pltpu.VMEM((N_KV_HEADS, rq, 1), F32), pltpu.VMEM((N_KV_HEADS, rq, LANES), F32)],
        compiler_params=pltpu.CompilerParams(dimension_semantics=("arbitrary", "arbitrary", "arbitrary"),
                                             vmem_limit_bytes=VMEM_LIMIT),
        name="prompt_sel",
    )(qx, ni, gate, ka, vs)


N_BACK = WINDOW // Q_BLOCK


def _prompt_win_body(qx_ref, gate_ref, *refs):
    k_refs, v_refs, o_ref = refs[:N_BACK + 1], refs[N_BACK + 1:2 * N_BACK + 2], refs[-1]
    qi = pl.program_id(1)
    rq = GROUP * Q_BLOCK
    lane = lax.broadcasted_iota(jnp.int32, (1, LANES), 1)
    qpos = qi * Q_BLOCK + lax.broadcasted_iota(jnp.int32, (rq, 1), 0) % Q_BLOCK
    kpos = jnp.concatenate([(qi - N_BACK + i) * Q_BLOCK + lane for i in range(N_BACK + 1)], axis=1)
    d = qpos - kpos
    mask = (d >= 0) & (d <= WINDOW) & (kpos >= 0)
    outs = []
    for h in range(N_KV_HEADS):
        qh = jnp.concatenate([qx_ref[:, (h * GROUP + g) * LANES:(h * GROUP + g + 1) * LANES] for g in range(GROUP)], axis=0)
        s = jnp.concatenate([_dot_nt(qh, k_refs[i][...]) for i in range(N_BACK + 1)], axis=1)
        p = _masked_softmax(s, mask).astype(BF16)
        o = _dot(p[:, :Q_BLOCK], v_refs[0][...])
        for i in range(1, N_BACK + 1):
            o = o + _dot(p[:, i * Q_BLOCK:(i + 1) * Q_BLOCK], v_refs[i][...])
        gcol = jnp.concatenate([_gate_col(gate_ref, (h * GROUP + g) * 3 + 2) for g in range(GROUP)], axis=0)
        outs.append(o * gcol)
    o_ref[0] = jnp.where(lane < HEAD_DIM, outs[0], outs[1]).reshape(GROUP, Q_BLOCK, LANES)


def _prompt_win(qx, gate, kw, vw, b, t):
    nq = t // Q_BLOCK
    qrow = lambda n: pl.BlockSpec((Q_BLOCK, n), lambda bi, qi: (bi * nq + qi, 0))
    kblk = lambda i: pl.BlockSpec((Q_BLOCK, LANES), lambda bi, qi: (bi * nq + jnp.maximum(qi - N_BACK + i, 0), 0))
    return pl.pallas_call(
        _prompt_win_body,
        grid=(b, nq),
        in_specs=[qrow(N_Q_HEADS * LANES), qrow(LANES)] + [kblk(i) for i in range(N_BACK + 1)] * 2,
        out_specs=pl.BlockSpec((1, GROUP, Q_BLOCK, LANES), lambda bi, qi: (bi, 0, qi, 0)),
        out_shape=jax.ShapeDtypeStruct((b, GROUP, t, LANES), F32),
        compiler_params=pltpu.CompilerParams(dimension_semantics=("arbitrary", "arbitrary"), vmem_limit_bytes=VMEM_LIMIT),
        name="prompt_win",
    )(qx, gate, *([kw] * (N_BACK + 1)), *([vw] * (N_BACK + 1)))


def _rglru_coeffs(xc, wra_ref, bra_ref, wrx_ref, brx_ref, lam_ref):
    rs, is_ = [], []
    for s in range(D_RNN // LANES):
        xs = xc[:, s * LANES:(s + 1) * LANES].astype(BF16)
        rs.append(_dot(xs, wra_ref[s]))
        is_.append(_dot(xs, wrx_ref[s]))
    r = _sigmoid(jnp.concatenate(rs, axis=1) + bra_ref[...])
    i = _sigmoid(jnp.concatenate(is_, axis=1) + brx_ref[...])
    z = -lam_ref[...]
    softplus = jnp.maximum(z, 0.0) + jnp.log1p(jnp.exp(-jnp.abs(z)))
    log_a = -RG_C * r * softplus
    a = jnp.exp(log_a)
    u = jnp.sqrt(-jnp.tanh(log_a) * (a * a + 1.0)) * (i * xc)
    return a, u


RNN_TC = 256


def _prompt_rnn_body(xr_ref, zr_ref, cw_ref, cb_ref, wra_ref, bra_ref, wrx_ref, brx_ref, lam_ref,
                     rb_ref, conv_ref, h_ref, xp_scr, hc_scr):
    tc = pl.program_id(1)

    @pl.when(tc == 0)
    def _():
        xp_scr[0:8, :] = jnp.zeros((8, D_RNN), F32)
        hc_scr[...] = jnp.zeros(hc_scr.shape, F32)

    x = xr_ref[...]
    xp_scr[8:8 + RNN_TC, :] = x
    xc = cb_ref[...]
    for k in range(CONV_W):
        off = 8 - (CONV_W - 1) + k
        xc = xc + xp_scr[off:off + RNN_TC, :] * cw_ref[k:k + 1, :]
    xp_scr[0:8, :] = x[RNN_TC - 8:, :]
    conv_ref[0] = x[RNN_TC - 8:, :]

    a, u = _rglru_coeffs(xc, wra_ref, bra_ref, wrx_ref, brx_ref, lam_ref)
    row = lax.broadcasted_iota(jnp.int32, (RNN_TC, 1), 0)
    d = 1
    while d < RNN_TC:
        keep = row >= d
        a_sh = jnp.where(keep, pltpu.roll(a, d, 0), 1.0)
        u_sh = jnp.where(keep, pltpu.roll(u, d, 0), 0.0)
        u = a * u_sh + u
        a = a * a_sh
        d *= 2
    h = u + a * hc_scr[...]
    hc_scr[...] = h[RNN_TC - 1:, :]
    h_ref[0] = h[RNN_TC - 1:, :]
    rb_ref[...] = (h * zr_ref[...]).astype(BF16)


def _prompt_rnn(xr, zr, rw, b, t):
    ntc = t // RNN_TC
    const = lambda a: pl.BlockSpec(a.shape, lambda bi, ti: (0,) * a.ndim)
    row = pl.BlockSpec((RNN_TC, D_RNN), lambda bi, ti: (bi * ntc + ti, 0))
    return pl.pallas_call(
        _prompt_rnn_body,
        grid=(b, ntc),
        in_specs=[row, row] + [const(a) for a in rw],
        out_specs=[row, pl.BlockSpec((1, 8, D_RNN), lambda bi, ti: (bi, 0, 0)),
                   pl.BlockSpec((1, 1, D_RNN), lambda bi, ti: (bi, 0, 0))],
        out_shape=[jax.ShapeDtypeStruct((b * t, D_RNN), BF16), jax.ShapeDtypeStruct((b, 8, D_RNN), F32),
                   jax.ShapeDtypeStruct((b, 1, D_RNN), F32)],
        scratch_shapes=[pltpu.VMEM((8 + RNN_TC, D_RNN), F32), pltpu.VMEM((1, D_RNN), F32)],
        compiler_params=pltpu.CompilerParams(dimension_semantics=("arbitrary", "arbitrary"), vmem_limit_bytes=VMEM_LIMIT),
        name="prompt_rnn",
    )(xr, zr, *rw)


def _sample_rnn_body(xr_ref, zr_ref, sc_ref, h0_ref, cw_ref, cb_ref, wra_ref, bra_ref, wrx_ref, brx_ref, lam_ref,
                     rb_ref, conv_ref, h_ref):
    x = xr_ref[...]
    xc = cb_ref[...]
    for k in range(CONV_W - 1):
        xc = xc + sc_ref[k] * cw_ref[k:k + 1, :]
    xc = xc + x * cw_ref[CONV_W - 1:CONV_W, :]
    for k in range(CONV_W - 2):
        conv_ref[k] = sc_ref[k + 1]
    conv_ref[CONV_W - 2] = x
    a, u = _rglru_coeffs(xc, wra_ref, bra_ref, wrx_ref, brx_ref, lam_ref)
    h = a * h0_ref[...] + u
    h_ref[...] = h
    rb_ref[...] = (h * zr_ref[...]).astype(BF16)


def _sample_rnn(xr, zr, sc, h0, rw):
    n = xr.shape[0]
    return pl.pallas_call(
        _sample_rnn_body,
        out_shape=[jax.ShapeDtypeStruct((n, D_RNN), BF16), jax.ShapeDtypeStruct((CONV_W - 1, n, D_RNN), F32),
                   jax.ShapeDtypeStruct((n, D_RNN), F32)],
        compiler_params=pltpu.CompilerParams(vmem_limit_bytes=VMEM_LIMIT),
        name="sample_rnn",
    )(xr, zr, sc, h0, *rw)


def _outproj_body(x_ref, oc_ref, os_ref, ow_ref, zn_ref, rb_ref, gm_ref, wpa_ref, wpb_ref, wo_ref, y_ref):
    pa = None
    for g in range(GROUP):
        o = (oc_ref[0, g] + os_ref[0, g]) + ow_ref[0, g]
        a = (o * zn_ref[:, g * LANES:(g + 1) * LANES]).astype(BF16)
        term = _dot(a, wpa_ref[g])
        pa = term if pa is None else pa + term
    pb = _dot(rb_ref[...], wpb_ref[...])
    merged = gm_ref[:, :D_MODEL] * pa + gm_ref[:, D_MODEL:] * pb
    y_ref[...] = x_ref[...] + _dot(merged.astype(BF16), wo_ref[...])


def _outproj(x2d, oc, osel, ow, zn, rb, gm, wpa, wpb, wo, tm):
    b, _, t, _ = oc.shape
    nt = t // tm
    row = lambda n: pl.BlockSpec((tm, n), lambda i: (i, 0))
    oblk = pl.BlockSpec((1, GROUP, tm, LANES), lambda i: (i // nt, 0, i % nt, 0))
    const = lambda a: pl.BlockSpec(a.shape, lambda i: (0,) * a.ndim)
    return pl.pallas_call(
        _outproj_body,
        grid=(b * nt,),
        in_specs=[row(D_MODEL), oblk, oblk, oblk, row(NSA_WIDTH), row(D_RNN), row(2 * D_MODEL), const(wpa), const(wpb), const(wo)],
        out_specs=row(D_MODEL),
        out_shape=jax.ShapeDtypeStruct(x2d.shape, F32),
        compiler_params=pltpu.CompilerParams(dimension_semantics=("arbitrary",), vmem_limit_bytes=VMEM_LIMIT),
        name="outproj",
    )(x2d, oc, osel, ow, zn, rb, gm, wpa, wpb, wo)


def _sample_cmp_body(pt_ref, cache_ref, qbd_ref, gate_ref, wbdk_ref, w2k_ref, pek_ref, w1tk_ref, wbdv_ref, w2v_ref, pev_ref,
                     w1tv_ref, gkc_ref, cos_ref, s1_ref, s2_ref, seg_ref, mimp_ref,
                     oc_ref, idx_ref, buf, sem, xtk, xtv, *, n_pages, past, ns, n_top):
    s = pl.program_id(0)
    n_seq = pl.num_programs(0)
    slot = s % 2
    n_sub = n_pages * (PAGE_SIZE // CMP_STRIDE)
    nsp = mimp_ref.shape[1]

    def page_copy(sl, seq, p):
        return pltpu.make_async_copy(cache_ref.at[pt_ref[seq * n_pages + p]], buf.at[sl, p], sem.at[sl])

    def start_all(sl, seq):
        lax.fori_loop(0, n_pages, lambda p, c: (page_copy(sl, seq, p).start(), c)[1], 0)

    @pl.when(s == 0)
    def _():
        start_all(0, 0)

    @pl.when(s + 1 < n_seq)
    def _():
        start_all(1 - slot, s + 1)

    lax.fori_loop(0, n_pages, lambda p, c: (page_copy(slot, s, p).wait(), c)[1], 0)

    def xpose(p, c):
        dst = pl.ds(pl.multiple_of(p * PAGE_SIZE, PAGE_SIZE), PAGE_SIZE)
        xtk[dst, :] = buf[slot, p, :LANES, :].T
        xtv[dst, :] = buf[slot, p, LANES:, :].T
        return c
    lax.fori_loop(0, n_pages, xpose, 0)

    def rows(ref):
        return lambda j: ref[pl.ds(j, n_sub, stride=CMP_STRIDE), :]
    kc = _compress_rows(rows(xtk), n_sub, wbdk_ref, w2k_ref, pek_ref, w1tk_ref)
    kc = _headnorm_rope(kc, gkc_ref[...], cos_ref[...], s1_ref[...], s2_ref[...], seg_ref[...]).astype(BF16)
    vc = _compress_rows(rows(xtv), n_sub, wbdv_ref, w2v_ref, pev_ref, w1tv_ref).astype(BF16)

    nr = N_Q_HEADS
    c_end = lax.broadcasted_iota(jnp.int32, (1, n_sub), 1) * CMP_STRIDE + (CMP_BLOCK - 1)
    p = _masked_softmax(_dot_nt(qbd_ref[0].astype(BF16), kc), jnp.broadcast_to(c_end <= past, (nr, n_sub)))
    o = _dot(p.astype(BF16), vc)
    lane = lax.broadcasted_iota(jnp.int32, (nr, LANES), 1)
    rowh = lax.broadcasted_iota(jnp.int32, (nr, LANES), 0) // GROUP
    oc_ref[0] = jnp.where(lane // HEAD_DIM == rowh, o * _sigmoid(gate_ref[0]), 0.0)

    psum = jnp.concatenate([jnp.sum(p[h * GROUP:(h + 1) * GROUP], axis=0, keepdims=True) for h in range(N_KV_HEADS)]
                           + [jnp.zeros((nr - N_KV_HEADS, n_sub), F32)], axis=0)
    jl = lax.broadcasted_iota(jnp.int32, (1, nsp), 1)
    cur = past // SEL_BLOCK
    forced = (jl == 0) | (jl == cur) | (jl == cur - 1)
    v = _dot_exact01(psum, mimp_ref[...]) + jnp.where(forced, FORCE_BONUS, 0.0)
    v = jnp.where(jl < ns, v, -1.0)
    vt = v.T
    js = lax.broadcasted_iota(jnp.int32, (nsp, 1), 0)
    rl = lax.broadcasted_iota(jnp.int32, (1, LANES), 1)
    idx_rows = []
    for h in range(N_KV_HEADS):
        vrow, vcol = v[h:h + 1, :], vt[:, h:h + 1]
        before = (vrow > vcol) | ((vrow == vcol) & (jl < js))
        rank = jnp.sum(jnp.where(before, 1.0, 0.0), axis=1, keepdims=True)
        hit = rank == rl.astype(F32)
        idx_rows.append(jnp.sum(jnp.where(hit, js, 0), axis=0, keepdims=True))
    idx_ref[0] = jnp.concatenate(idx_rows + [jnp.zeros((nr - N_KV_HEADS, LANES), jnp.int32)], axis=0)


def _sample_cmp(pt_flat, cache_t, qbd, gate_c, cw, gkc, ctabs, seg, mimp, n_seq, n_pages, past, ns, n_top):
    n_sub = n_pages * (PAGE_SIZE // CMP_STRIDE)
    const = lambda a: pl.BlockSpec(a.shape, lambda s, pt: (0,) * a.ndim)
    seqblk = pl.BlockSpec((1, N_Q_HEADS, LANES), lambda s, pt: (s, 0, 0))
    consts = [*cw, gkc, *ctabs, seg, mimp]
    return pl.pallas_call(
        functools.partial(_sample_cmp_body, n_pages=n_pages, past=past, ns=ns, n_top=n_top),
        grid_spec=pltpu.PrefetchScalarGridSpec(
            num_scalar_prefetch=1,
            grid=(n_seq,),
            in_specs=[pl.BlockSpec(memory_space=pl.ANY), seqblk, seqblk] + [const(a) for a in consts],
            out_specs=[seqblk, seqblk],
            scratch_shapes=[pltpu.VMEM((2, n_pages, KV_COLS, PAGE_SIZE), F32), pltpu.SemaphoreType.DMA((2,)),
                            pltpu.VMEM((n_pages * PAGE_SIZE, LANES), F32), pltpu.VMEM((n_pages * PAGE_SIZE, LANES), F32)],
        ),
        out_shape=[jax.ShapeDtypeStruct((n_seq, N_Q_HEADS, LANES), F32), jax.ShapeDtypeStruct((n_seq, N_Q_HEADS, LANES), jnp.int32)],
        compiler_params=pltpu.CompilerParams(dimension_semantics=("arbitrary",), vmem_limit_bytes=VMEM_LIMIT),
        name="sample_cmp",
    )(pt_flat, cache_t, qbd, gate_c, *consts)


def _decode_attend(q_col, kts, vts, mask, k_new, v_new):
    s = jnp.sum(kts * q_col[None], axis=1)
    s_new = jnp.sum(q_col * k_new, axis=0, keepdims=True)
    if mask is not None:
        s = jnp.where(mask, s, -jnp.inf)
    m = jnp.maximum(jnp.max(jnp.max(s, axis=1, keepdims=True), axis=0, keepdims=True), s_new)
    e = jnp.exp(s - m)
    if mask is not None:
        e = jnp.where(mask, e, 0.0)
    e_new = jnp.exp(s_new - m)
    den = jnp.sum(jnp.sum(e, axis=1, keepdims=True), axis=0, keepdims=True) + e_new
    acc = jnp.sum(e[:, None, :] * vts, axis=0)
    num = jnp.sum(acc, axis=1, keepdims=True) + e_new * v_new
    return num / den


def _sample_sel_body(pg_ref, meta_ref, cache_ref, q_ref, kv_ref, gate_ref, o_ref, buf, sem, *, n_top):
    s = pl.program_id(0)
    n_seq = pl.num_programs(0)
    slot = s % 2
    n_slots = N_KV_HEADS * n_top

    def page_copy(sl, seq, i):
        return pltpu.make_async_copy(cache_ref.at[pg_ref[seq * n_slots + i]], buf.at[sl, i], sem.at[sl])

    def start_all(sl, seq):
        lax.fori_loop(0, n_slots, lambda i, c: (page_copy(sl, seq, i).start(), c)[1], 0)

    @pl.when(s == 0)
    def _():
        start_all(0, 0)

    @pl.when(s + 1 < n_seq)
    def _():
        start_all(1 - slot, s + 1)

    lax.fori_loop(0, n_slots, lambda i, c: (page_copy(slot, s, i).wait(), c)[1], 0)

    lane = lax.broadcasted_iota(jnp.int32, (1, PAGE_SIZE), 1)
    slot_row = lax.broadcasted_iota(jnp.int32, (n_top, 1), 0)
    for h in range(N_KV_HEADS):
        lo = jnp.zeros((n_top, 1), jnp.int32)
        for r in range(n_top):
            lo = jnp.where(slot_row == r, meta_ref[s * n_slots + h * n_top + r] * SEL_BLOCK, lo)
        mask = (lane >= lo) & (lane < lo + SEL_BLOCK) & (lane < PAGE_SIZE)
        kts = buf[slot, h * n_top:(h + 1) * n_top, h * HEAD_DIM:(h + 1) * HEAD_DIM, :]
        vts = buf[slot, h * n_top:(h + 1) * n_top, (N_KV_HEADS + h) * HEAD_DIM:(N_KV_HEADS + h + 1) * HEAD_DIM, :]
        k_new = kv_ref[0, h * HEAD_DIM:(h + 1) * HEAD_DIM, :]
        v_new = kv_ref[0, (N_KV_HEADS + h) * HEAD_DIM:(N_KV_HEADS + h + 1) * HEAD_DIM, :]
        for g in range(GROUP):
            c0 = (h * GROUP + g) * HEAD_DIM
            o = _decode_attend(q_ref[0, c0:c0 + HEAD_DIM, :], kts, vts, mask, k_new, v_new)
            o_ref[0, c0:c0 + HEAD_DIM, :] = o * _sigmoid(gate_ref[0, c0:c0 + HEAD_DIM, :])


def _sample_sel(pg, meta, cache_t, qcol, kvcol, gexp, n_seq, n_top):
    col = lambda n: pl.BlockSpec((1, n, 1), lambda s, pg, meta: (s, 0, 0))
    return pl.pallas_call(
        functools.partial(_sample_sel_body, n_top=n_top),
        grid_spec=pltpu.PrefetchScalarGridSpec(
            num_scalar_prefetch=2,
            grid=(n_seq,),
            in_specs=[pl.BlockSpec(memory_space=pl.ANY), col(NSA_WIDTH), col(KV_COLS), col(NSA_WIDTH)],
            out_specs=col(NSA_WIDTH),
            scratch_shapes=[pltpu.VMEM((2, N_KV_HEADS * n_top, KV_COLS, PAGE_SIZE), F32), pltpu.SemaphoreType.DMA((2,))],
        ),
        out_shape=jax.ShapeDtypeStruct((n_seq, NSA_WIDTH, 1), F32),
        compiler_params=pltpu.CompilerParams(dimension_semantics=("arbitrary",), vmem_limit_bytes=VMEM_LIMIT),
        name="sample_sel",
    )(pg, meta, cache_t, qcol, kvcol, gexp)


def _sample_win_body(win_ref, q_ref, kv_ref, gate_ref, o_ref, wout_ref):
    w = win_ref[0]
    new = kv_ref[0]
    lane = lax.broadcasted_iota(jnp.int32, (1, WINDOW), 1)
    wout_ref[0] = jnp.where(lane == WINDOW - 1, new, pltpu.roll(w, WINDOW - 1, 1))
    for h in range(N_KV_HEADS):
        kts = w[h * HEAD_DIM:(h + 1) * HEAD_DIM, :][None]
        vts = w[(N_KV_HEADS + h) * HEAD_DIM:(N_KV_HEADS + h + 1) * HEAD_DIM, :][None]
        k_new = new[h * HEAD_DIM:(h + 1) * HEAD_DIM, :]
        v_new = new[(N_KV_HEADS + h) * HEAD_DIM:(N_KV_HEADS + h + 1) * HEAD_DIM, :]
        for g in range(GROUP):
            c0 = (h * GROUP + g) * HEAD_DIM
            o = _decode_attend(q_ref[0, c0:c0 + HEAD_DIM, :], kts, vts, None, k_new, v_new)
            o_ref[0, c0:c0 + HEAD_DIM, :] = o * _sigmoid(gate_ref[0, c0:c0 + HEAD_DIM, :])


def _sample_win(win_t, qcol, kvcol, gexp):
    n_seq = win_t.shape[0]
    col = lambda n: pl.BlockSpec((1, n, 1), lambda s: (s, 0, 0))
    wblk = pl.BlockSpec((1, KV_COLS, WINDOW), lambda s: (s, 0, 0))
    return pl.pallas_call(
        _sample_win_body,
        grid=(n_seq,),
        in_specs=[wblk, col(NSA_WIDTH), col(KV_COLS), col(NSA_WIDTH)],
        out_specs=[col(NSA_WIDTH), wblk],
        out_shape=[jax.ShapeDtypeStruct((n_seq, NSA_WIDTH, 1), F32), jax.ShapeDtypeStruct(win_t.shape, F32)],
        compiler_params=pltpu.CompilerParams(dimension_semantics=("arbitrary",), vmem_limit_bytes=VMEM_LIMIT),
        name="sample_win",
    )(win_t, qcol, kvcol, gexp)


def _prep_inproj_weight(w_in):
    sizes = (NSA_WIDTH, 3 * KV_COLS, 3 * N_Q_HEADS, NSA_WIDTH, D_RNN, D_RNN, 2 * D_MODEL)
    offs = np.concatenate([[0], np.cumsum(sizes)])
    q, kv, gn, zn, xr, zr, gm = [w_in[:, offs[i]:offs[i + 1]] for i in range(len(sizes))]
    zn = zn.reshape(D_MODEL, N_KV_HEADS, GROUP, HEAD_DIM).transpose(0, 2, 1, 3).reshape(D_MODEL, NSA_WIDTH)
    pad = jnp.zeros((D_MODEL, LANES - 3 * N_Q_HEADS), w_in.dtype)
    return jnp.concatenate([q, kv, zn, xr, zr, gm, gn, pad], axis=1).astype(BF16)


def _prep_compress(pe, w1, w2):
    eye = jnp.eye(N_KV_HEADS, dtype=F32)
    bd = lambda w: jnp.einsum('jde,hk->jhdke', w, eye).reshape(CMP_STRIDE, LANES, LANES)
    wbd = jnp.concatenate([bd(w1[:CMP_STRIDE]), bd(w1[CMP_STRIDE:])], axis=2).astype(BF16)
    w2bd = jnp.einsum('ed,hk->hekd', w2, eye).reshape(LANES, LANES).astype(BF16)
    pe_col = pe.reshape(CMP_BLOCK * HEAD_DIM, 1)
    w1t = jnp.tile(w1.reshape(CMP_BLOCK * HEAD_DIM, CMP_HIDDEN), (1, N_KV_HEADS))
    return wbd, w2bd, pe_col, w1t


def _prep_rnn(conv_w, conv_b, w_ra, b_ra, w_rx, b_rx, lam):
    def pairs(w):
        w = w.reshape(RNN_HEADS // 2, 2, RNN_HD, RNN_HD)
        eye = jnp.eye(2, dtype=F32)
        return jnp.einsum('shij,hk->shikj', w, eye).reshape(RNN_HEADS // 2, LANES, LANES).astype(BF16)
    return (conv_w, conv_b.reshape(1, D_RNN), pairs(w_ra), b_ra.reshape(1, D_RNN), pairs(w_rx), b_rx.reshape(1, D_RNN),
            lam.reshape(1, D_RNN))


def _feature_major(cache):
    n, r = cache.shape[:2]
    return cache.transpose(0, 2, 3, 4, 1).reshape(n, KV_COLS, r)


def kernel(x_prompt, x_sample, cache_kv_cmp, cache_kv_sel, cache_kv_win, state_conv, state_h, page_table, g_norm, w_in, g_q, g_kc, g_ks, g_kw, pe_k, w1_k, w2_k, pe_v, w1_v, w2_v, conv_w, conv_b, w_ra, b_ra, w_rx, b_rx, lam, w_pa, w_pb, w_out):
    b, t, _ = x_prompt.shape
    n_seq = x_sample.shape[0]
    n_pages = page_table.shape[1]
    past = n_pages * PAGE_SIZE
    assert x_sample.shape[1] == 1 and cache_kv_win.shape[1] == WINDOW and past >= WINDOW
    assert t % SEL_TK == 0 and t // SEL_BLOCK <= HEAD_DIM and t >= WINDOW

    wp = _prep_inproj_weight(w_in)
    seg = _seg01()
    tile2 = lambda g: jnp.tile(g.reshape(1, HEAD_DIM), (1, 2))
    gq, gks, gkw, gkc = tile2(g_q), tile2(g_ks), tile2(g_kw), tile2(g_kc)
    gnorm = g_norm.reshape(1, D_MODEL)
    cw = (*_prep_compress(pe_k, w1_k, w2_k), *_prep_compress(pe_v, w1_v, w2_v))
    rw = _prep_rnn(conv_w, conv_b, w_ra, b_ra, w_rx, b_rx, lam)
    wpa = w_pa.reshape(N_KV_HEADS, GROUP, HEAD_DIM, D_MODEL).transpose(1, 0, 2, 3).reshape(GROUP, LANES, D_MODEL).astype(BF16)
    wpb, wo = w_pb.astype(BF16), w_out.astype(BF16)

    tm = 256
    tok = np.arange(t)
    oh = jnp.asarray((tok[:, None] // SEL_BLOCK) == (np.arange(LANES)[None, :] % HEAD_DIM), BF16)
    xp2 = x_prompt.reshape(b * t, D_MODEL)
    (kvc, kvs, kvw, qx, _, ka, vs, kw, vw, zn, xr, zr, gm, gate) = _inproj(
        xp2, gnorm, wp, _rope_tables(tok), oh, gq, gks, gkw, seg, tm, t // tm)
    n_sub = t // CMP_STRIDE
    ns = t // SEL_BLOCK
    mimp = jnp.asarray(_imp_matrix(n_sub, ns, n_sub - 1, ns).T, BF16)
    ctabs = _rope_tables(np.arange(n_sub) * CMP_STRIDE + (CMP_BLOCK - 1))
    oc, ni = _prompt_cmp(kvc, qx, gate, cw, gkc, ctabs, seg, mimp, b, t)
    osel = _prompt_sel(qx, ni, gate, ka, vs, b, t)
    owin = _prompt_win(qx, gate, kw, vw, b, t)
    rb, conv_tail, h_last = _prompt_rnn(xr, zr, rw, b, t)
    y_prompt = _outproj(xp2, oc, osel, owin, zn, rb, gm, wpa, wpb, wo, tm).reshape(b, t, D_MODEL)
    kv5 = lambda a, rows: a.reshape(-1, rows, 2, N_KV_HEADS, HEAD_DIM)
    kv_cmp_prompt, kv_sel_prompt = kv5(kvc, t), kv5(kvs, t)
    kv_win_prompt = kv5(kvw, t)[:, t - min(WINDOW, t):]
    conv_prompt = conv_tail[:, 8 - (CONV_W - 1):]
    h_prompt = h_last.reshape(b, D_RNN)

    xs2 = x_sample.reshape(n_seq, D_MODEL)
    oh_s = jnp.zeros((n_seq, LANES), BF16)
    (kvc_s, kvs_s, kvw_s, _, qf_s, _, _, _, _, zn_s, xr_s, zr_s, gm_s, gate_s) = _inproj(
        xs2, gnorm, wp, _rope_tables(np.full((n_seq,), past)), oh_s, gq, gks, gkw, seg, n_seq, 1)
    eye = jnp.eye(N_KV_HEADS, dtype=F32)
    q4 = qf_s.reshape(n_seq, N_KV_HEADS, GROUP, HEAD_DIM)
    qbd = (q4[:, :, :, None, :] * eye[None, :, None, :, None]).reshape(n_seq, N_Q_HEADS, LANES)
    qcol = qf_s.reshape(n_seq, NSA_WIDTH, 1)
    gates3 = gate_s[:, :3 * N_Q_HEADS].reshape(n_seq, N_Q_HEADS, 3)
    gate_c = jnp.broadcast_to(gates3[:, :, 0:1], (n_seq, N_Q_HEADS, LANES))
    gexp = lambda k: jnp.broadcast_to(gates3[:, :, k:k + 1], (n_seq, N_Q_HEADS, HEAD_DIM)).reshape(n_seq, NSA_WIDTH, 1)

    n_sub_s = past // CMP_STRIDE
    ns_s = past // SEL_BLOCK + 1
    n_top_s = min(N_SEL, ns_s)
    nsp = -(-ns_s // LANES) * LANES
    mimp_s = jnp.asarray(_imp_matrix(n_sub_s, nsp, n_sub_s - 1, ns_s), BF16)
    ctabs_s = _rope_tables(np.arange(n_sub_s) * CMP_STRIDE + (CMP_BLOCK - 1))
    oc_s, idx = _sample_cmp(page_table.reshape(-1), _feature_major(cache_kv_cmp), qbd, gate_c, cw, gkc, ctabs_s, seg, mimp_s,
                            n_seq, n_pages, past, ns_s, n_top_s)
    blocks = idx[:, :N_KV_HEADS, :n_top_s]
    per_page = PAGE_SIZE // SEL_BLOCK
    in_cache = blocks < past // SEL_BLOCK
    pg = jnp.take_along_axis(page_table, jnp.minimum(blocks // per_page, n_pages - 1).reshape(n_seq, -1), axis=1)
    meta = jnp.where(in_cache, blocks % per_page, per_page).reshape(-1).astype(jnp.int32)
    o_sel_s = _sample_sel(pg.reshape(-1).astype(jnp.int32), meta, _feature_major(cache_kv_sel), qcol,
                          kvs_s.reshape(n_seq, KV_COLS, 1), gexp(1), n_seq, n_top_s)
    o_win_s, win_new = _sample_win(_feature_major(cache_kv_win), qcol, kvw_s.reshape(n_seq, KV_COLS, 1), gexp(2))
    sc_t = state_conv.transpose(1, 0, 2)
    rb_s, conv_new, h_sample = _sample_rnn(xr_s, zr_s, sc_t, state_h, rw)
    to_o = lambda o: o.reshape(n_seq, N_KV_HEADS, GROUP, HEAD_DIM).transpose(2, 0, 1, 3).reshape(1, GROUP, n_seq, LANES)
    oc_s4 = (oc_s[:, :GROUP] + oc_s[:, GROUP:]).transpose(1, 0, 2)[None]
    y_sample = _outproj(xs2, oc_s4, to_o(o_sel_s), to_o(o_win_s), zn_s, rb_s, gm_s, wpa, wpb, wo, n_seq).reshape(n_seq, 1, D_MODEL)
    kv_cmp_sample, kv_sel_sample = kv5(kvc_s, 1), kv5(kvs_s, 1)
    kv_win_sample = win_new.reshape(n_seq, 2, N_KV_HEADS, HEAD_DIM, WINDOW).transpose(0, 4, 1, 2, 3)
    conv_sample = conv_new.transpose(1, 0, 2)

    return (y_prompt, y_sample, kv_cmp_prompt, kv_cmp_sample, kv_sel_prompt, kv_sel_sample,
            kv_win_prompt, kv_win_sample, conv_prompt, conv_sample, h_prompt, h_sample)
```

```python
import functools

import numpy as np
import jax
import jax.numpy as jnp
from jax import lax
from jax.experimental import pallas as pl
from jax.experimental.pallas import tpu as pltpu

D_MODEL = 1024
HEAD_DIM = 64
N_Q_HEADS = 8
N_KV_HEADS = 2
GROUP = N_Q_HEADS // N_KV_HEADS
NSA_WIDTH = N_Q_HEADS * HEAD_DIM
SCALE = HEAD_DIM ** -0.5
ROPE_DIM = HEAD_DIM // 4
ROPE_HALF = ROPE_DIM // 2
ROPE_THETA = 500000.0
CMP_BLOCK = 32
CMP_STRIDE = 16
CMP_HIDDEN = 64
SEL_BLOCK = 64
N_SEL = 16
WINDOW = 512
Q_BLOCK = 128
FORCE_BONUS = 1.0e4
D_RNN = D_MODEL // 2
RNN_HEADS = 8
RNN_HD = D_RNN // RNN_HEADS
CONV_W = 4
RG_C = 8.0
EPS = 1e-6
PAGE_SIZE = 128

LANES = 128
KV_COLS = 2 * N_KV_HEADS * HEAD_DIM
NEG = -(2.0 ** 100)
LOG2E = 1.4426950408889634
V_ROWS = LANES + 16
VMEM_LIMIT = 56 * 1024 * 1024
F32 = jnp.float32
BF16 = jnp.bfloat16

C_Q = 0
C_KV = C_Q + NSA_WIDTH
C_ZN = C_KV + 3 * KV_COLS
C_XR = C_ZN + NSA_WIDTH
C_ZR = C_XR + D_RNN
C_GM = C_ZR + D_RNN
C_GN = C_GM + 2 * D_MODEL
N_WCOLS = C_GN + LANES


def _dot(a, b):
    return jnp.dot(a, b, preferred_element_type=F32)


def _dot_nt(a, b):
    return lax.dot_general(a, b, (((1,), (1,)), ((), ())), preferred_element_type=F32)


def _split3(a):
    a1 = a.astype(BF16)
    r1 = a - a1.astype(F32)
    a2 = r1.astype(BF16)
    a3 = (r1 - a2.astype(F32)).astype(BF16)
    return a1, a2, a3


def _dot_exact01(a, b01):
    a1, a2, a3 = _split3(a)
    return _dot(a1, b01) + _dot(a2, b01) + _dot(a3, b01)


def _dot_nt_exact01(b01, a):
    a1, a2, a3 = _split3(a)
    return _dot_nt(b01, a1) + _dot_nt(b01, a2) + _dot_nt(b01, a3)


def _sigmoid(x):
    return 1.0 / (1.0 + jnp.exp(-x))


def _silu(x):
    return x * _sigmoid(x)


def _headnorm_rope(xs, gain, cos, s1, s2, seg01):
    ss = _dot_exact01(xs * xs, seg01)
    y = xs * lax.rsqrt(ss * (1.0 / HEAD_DIM) + EPS) * gain
    return y * cos + pltpu.roll(y, ROPE_HALF, 1) * s1 + pltpu.roll(y, LANES - ROPE_HALF, 1) * s2


def _masked_softmax(s, mask, exp=jnp.exp):
    s = jnp.where(mask, s, -jnp.inf)
    m = jnp.max(s, axis=-1, keepdims=True)
    m = jnp.where(m == -jnp.inf, 0.0, m)
    e = jnp.where(mask, exp(s - m), 0.0)
    d = jnp.sum(e, axis=-1, keepdims=True)
    return e / jnp.where(d > 0, d, 1.0)


def _rope_tables(pos):
    pos = np.asarray(pos, np.float64)
    inv = ROPE_THETA ** (-np.arange(ROPE_HALF, dtype=np.float64) / ROPE_HALF)
    ang = (pos.astype(np.float32)[:, None] * inv.astype(np.float32)[None, :]).astype(np.float32).astype(np.float64)
    cos, sin = np.cos(ang), np.sin(ang)
    n = pos.shape[0]
    c = np.ones((n, HEAD_DIM)); s1 = np.zeros((n, HEAD_DIM)); s2 = np.zeros((n, HEAD_DIM))
    c[:, :ROPE_HALF] = cos; c[:, ROPE_HALF:ROPE_DIM] = cos
    s1[:, ROPE_HALF:ROPE_DIM] = sin
    s2[:, :ROPE_HALF] = -sin
    t = lambda a: jnp.asarray(np.tile(a, (1, 2)), F32)
    return t(c), t(s1), t(s2)


def _seg01():
    lane = np.arange(LANES)
    return jnp.asarray((lane[:, None] // HEAD_DIM) == (lane[None, :] // HEAD_DIM), BF16)


def _imp_matrix(nc_pad, ns_pad, nc, ns):
    r = SEL_BLOCK // CMP_STRIDE
    lead = CMP_BLOCK // CMP_STRIDE - 1
    c = np.arange(nc_pad)[:, None]
    j = np.arange(ns_pad)[None, :]
    m = (c >= r * j - lead) & (c <= r * j + r - 1) & (c < nc) & (j < ns)
    return m


def _inproj_body(x_ref, gn_ref, w_ref, cos_ref, s1_ref, s2_ref, oh_ref, gq_ref, gks_ref, gkw_ref, seg_ref,
                 kvc_ref, kvs_ref, kvw_ref, qx_ref, qf_ref, ka_ref, kw_ref, vw_ref, vst_ref,
                 zn_ref, xr_ref, zr_ref, gm_ref, gate_ref):
    x = x_ref[...]
    r = lax.rsqrt(jnp.mean(x * x, axis=-1, keepdims=True) + EPS)
    u = (x * r * gn_ref[...]).astype(BF16)
    cos, s1, s2, seg = cos_ref[...], s1_ref[...], s2_ref[...], seg_ref[...]
    lane = lax.broadcasted_iota(jnp.int32, (1, LANES), 1)

    def proj(c0, n):
        return _dot(u, w_ref[:, c0:c0 + n])

    q = proj(C_Q, NSA_WIDTH)
    for s in range(NSA_WIDTH // LANES):
        qs = _headnorm_rope(q[:, s * LANES:(s + 1) * LANES], gq_ref[...], cos, s1, s2, seg) * SCALE
        qf_ref[:, s * LANES:(s + 1) * LANES] = qs
        qs = qs * LOG2E
        qs_sw = pltpu.roll(qs, HEAD_DIM, 1)
        for half in range(2):
            i = 2 * s + half
            h = i // GROUP
            src = qs if half == h else qs_sw
            keep = (lane >= h * HEAD_DIM) & (lane < (h + 1) * HEAD_DIM)
            qx_ref[:, i * LANES:(i + 1) * LANES] = jnp.where(keep, src, 0.0).astype(BF16)

    kv = proj(C_KV, 3 * KV_COLS)
    kvc_ref[...] = kv[:, :KV_COLS]
    ks = _headnorm_rope(kv[:, KV_COLS:KV_COLS + LANES], gks_ref[...], cos, s1, s2, seg)
    vs = kv[:, KV_COLS + LANES:2 * KV_COLS]
    kvs_ref[:, :LANES] = ks
    kvs_ref[:, LANES:] = vs
    ka_ref[:, :LANES] = ks.astype(BF16)
    ka_ref[:, LANES:] = oh_ref[...]
    vst_ref[:LANES, :] = vs.T.astype(BF16)
    ones_row = lax.broadcasted_iota(jnp.int32, (V_ROWS - LANES, x.shape[0]), 0) == 0
    vst_ref[LANES:, :] = jnp.where(ones_row, 1.0, 0.0).astype(BF16)
    kw = _headnorm_rope(kv[:, 2 * KV_COLS:2 * KV_COLS + LANES], gkw_ref[...], cos, s1, s2, seg)
    vw = kv[:, 2 * KV_COLS + LANES:]
    kvw_ref[:, :LANES] = kw
    kvw_ref[:, LANES:] = vw
    kw_ref[...] = kw.astype(BF16)
    vw_ref[...] = vw.astype(BF16)

    zn_ref[...] = _silu(proj(C_ZN, NSA_WIDTH))
    xr_ref[...] = proj(C_XR, D_RNN)
    zr_ref[...] = _silu(proj(C_ZR, D_RNN))
    for c in range(2):
        gm_ref[:, c * D_MODEL:(c + 1) * D_MODEL] = _sigmoid(proj(C_GM + c * D_MODEL, D_MODEL))
    gate_ref[...] = proj(C_GN, LANES)


def _inproj(x2d, gnorm, wp, tabs, oh, gq, gks, gkw, seg, tm, n_tab_blocks):
    rows = x2d.shape[0]
    grid = (rows // tm,)
    row = lambda n: pl.BlockSpec((tm, n), lambda i: (i, 0))
    tab = pl.BlockSpec((tm, LANES), lambda i: (i % n_tab_blocks, 0))
    const = lambda a: pl.BlockSpec(a.shape, lambda i: (0,) * a.ndim)
    out_defs = [(KV_COLS, F32), (KV_COLS, F32), (KV_COLS, F32), (N_Q_HEADS * LANES, BF16), (NSA_WIDTH, F32),
                (2 * LANES, BF16), (LANES, BF16), (LANES, BF16)]
    tail_defs = [(NSA_WIDTH, F32), (D_RNN, F32), (D_RNN, F32), (2 * D_MODEL, F32), (LANES, F32)]
    vst_spec = pl.BlockSpec((V_ROWS, tm), lambda i: (0, i))
    return pl.pallas_call(
        _inproj_body,
        grid=grid,
        in_specs=[row(D_MODEL), const(gnorm), const(wp), tab, tab, tab, tab, const(gq), const(gks), const(gkw), const(seg)],
        out_specs=[row(n) for n, _ in out_defs] + [vst_spec] + [row(n) for n, _ in tail_defs],
        out_shape=[jax.ShapeDtypeStruct((rows, n), dt) for n, dt in out_defs] + [jax.ShapeDtypeStruct((V_ROWS, rows), BF16)]
                  + [jax.ShapeDtypeStruct((rows, n), dt) for n, dt in tail_defs],
        compiler_params=pltpu.CompilerParams(dimension_semantics=("arbitrary",), vmem_limit_bytes=VMEM_LIMIT),
        name="inproj",
    )(x2d, gnorm, wp, *tabs, oh, gq, gks, gkw, seg)


def _compress_rows(load_rows, n_sub, wbd_ref, w2_ref, pe_ref, w1t_ref):
    acc = jnp.zeros((n_sub, 2 * LANES), F32)
    for j in range(CMP_STRIDE):
        acc = acc + _dot(load_rows(j).astype(BF16), wbd_ref[j])
    return _compress_finish(acc, n_sub, w2_ref, _pos_bias(pe_ref, w1t_ref))


def _pos_bias(pe_ref, w1t_ref):
    return jnp.sum(pe_ref[...] * w1t_ref[...], axis=0, keepdims=True)


def _compress_finish(acc, n_sub, w2_ref, pos_bias):
    lo, hi = acc[:, :LANES], acc[:, LANES:]
    hid = _silu(lo + pltpu.roll(hi, n_sub - 1, 0) + pos_bias)
    return _dot(hid.astype(BF16), w2_ref[...])


def _gate_col(gate_ref, col):
    return _sigmoid(gate_ref[:, col:col + 1])


def _prompt_cmp_body(kraw_ref, vraw_ref, qx_ref, gate_ref, wbdk_ref, w2k_ref, pek_ref, w1tk_ref, wbdv_ref, w2v_ref, pev_ref,
                     w1tv_ref, gkc_ref, cos_ref, s1_ref, s2_ref, seg_ref, mimp_ref,
                     oc_ref, ni_ref, kc_scr, vc_scr, *, n_sub, ns, n_top):
    qi = pl.program_id(1)

    @pl.when(qi == 0)
    def _():
        def rows(ref):
            return lambda j: ref[0, pl.ds(j, n_sub, stride=CMP_STRIDE), :]
        kc = _compress_rows(rows(kraw_ref), n_sub, wbdk_ref, w2k_ref, pek_ref, w1tk_ref)
        kc = _headnorm_rope(kc, gkc_ref[...], cos_ref[...], s1_ref[...], s2_ref[...], seg_ref[...])
        kc_scr[...] = kc.astype(BF16)
        vc_scr[...] = _compress_rows(rows(vraw_ref), n_sub, wbdv_ref, w2v_ref, pev_ref, w1tv_ref).astype(BF16)

    rq = GROUP * Q_BLOCK
    qpos_r = qi * Q_BLOCK + lax.broadcasted_iota(jnp.int32, (rq, 1), 0) % Q_BLOCK
    c_end = lax.broadcasted_iota(jnp.int32, (1, n_sub), 1) * CMP_STRIDE + (CMP_BLOCK - 1)
    cmask = c_end <= qpos_r
    qpos_l = qi * Q_BLOCK + lax.broadcasted_iota(jnp.int32, (1, Q_BLOCK), 1)
    jrow = lax.broadcasted_iota(jnp.int32, (ns, 1), 0)
    cur = qpos_l // SEL_BLOCK
    forced = (jrow == 0) | (jrow == cur) | (jrow == cur - 1)
    lane = lax.broadcasted_iota(jnp.int32, (1, LANES), 1)
    outs, negs = [], []
    for h in range(N_KV_HEADS):
        qh = jnp.concatenate([qx_ref[:, (h * GROUP + g) * LANES:(h * GROUP + g + 1) * LANES] for g in range(GROUP)], axis=0)
        p = _masked_softmax(_dot_nt(qh, kc_scr[...]), cmask, jnp.exp2)
        o = _dot(p.astype(BF16), vc_scr[...])
        gcol = jnp.concatenate([_gate_col(gate_ref, (h * GROUP + g) * 3) for g in range(GROUP)], axis=0)
        outs.append(o * gcol)
        psum = (p[0:Q_BLOCK] + p[Q_BLOCK:2 * Q_BLOCK]) + p[2 * Q_BLOCK:3 * Q_BLOCK] + p[3 * Q_BLOCK:]
        v = _dot_nt_exact01(mimp_ref[...], psum) + jnp.where(forced, FORCE_BONUS, 0.0)
        rank = jnp.zeros((ns, Q_BLOCK), F32)
        for i in range(ns):
            vi = v[i:i + 1, :]
            before = (vi > v) | ((vi == v) & (jrow > i))
            rank = rank + jnp.where(before, 1.0, 0.0)
        neg = jnp.where(rank < n_top, 0.0, NEG)
        if ns < HEAD_DIM:
            neg = jnp.concatenate([neg, jnp.zeros((HEAD_DIM - ns, Q_BLOCK), F32)], axis=0)
        negs.append(neg)
    ni_ref[...] = jnp.concatenate(negs, axis=0).T.astype(BF16)
    oc_ref[0] = jnp.where(lane < HEAD_DIM, outs[0], outs[1]).reshape(GROUP, Q_BLOCK, LANES)


def _prompt_cmp(kvc, qx, gate, cw, gkc, ctabs, seg, mimp, b, t):
    n_sub = t // CMP_STRIDE
    ns = t // SEL_BLOCK
    n_top = min(N_SEL, ns)
    nq = t // Q_BLOCK
    const = lambda a: pl.BlockSpec(a.shape, lambda bi, qi: (0,) * a.ndim)
    qrow = lambda n: pl.BlockSpec((Q_BLOCK, n), lambda bi, qi: (bi * nq + qi, 0))
    consts = [*cw, gkc, *ctabs, seg, mimp]
    return pl.pallas_call(
        functools.partial(_prompt_cmp_body, n_sub=n_sub, ns=ns, n_top=n_top),
        grid=(b, nq),
        in_specs=[pl.BlockSpec((1, t, LANES), lambda bi, qi: (bi, 0, 0)), pl.BlockSpec((1, t, LANES), lambda bi, qi: (bi, 0, 1)),
                  qrow(N_Q_HEADS * LANES), qrow(LANES)] + [const(a) for a in consts],
        out_specs=[pl.BlockSpec((1, GROUP, Q_BLOCK, LANES), lambda bi, qi: (bi, 0, qi, 0)), qrow(LANES)],
        out_shape=[jax.ShapeDtypeStruct((b, GROUP, t, LANES), F32), jax.ShapeDtypeStruct((b * t, LANES), BF16)],
        scratch_shapes=[pltpu.VMEM((n_sub, LANES), BF16), pltpu.VMEM((n_sub, LANES), BF16)],
        compiler_params=pltpu.CompilerParams(dimension_semantics=("arbitrary", "arbitrary"), vmem_limit_bytes=VMEM_LIMIT),
        name="prompt_cmp",
    )(kvc.reshape(b, t, KV_COLS), kvc.reshape(b, t, KV_COLS), qx, gate, *consts)


SEL_TK = 512
SEL_TC = 256


def _prompt_sel_body(qx_ref, ni_ref, gate_ref, ka_ref, vt_ref, o_ref, qat_scr, m_scr, acc_scr, sa_scr, sb_scr):
    qi = pl.program_id(1)
    ncol = N_Q_HEADS * Q_BLOCK
    lane = lax.broadcasted_iota(jnp.int32, (1, LANES), 1)

    ni = ni_ref[...].astype(F32)
    for h in range(N_KV_HEADS):
        keep = (lane >= h * HEAD_DIM) & (lane < (h + 1) * HEAD_DIM)
        nit = jnp.where(keep, ni, 0.0).T.astype(BF16)
        for g in range(GROUP):
            i = h * GROUP + g
            qat_scr[:LANES, i * Q_BLOCK:(i + 1) * Q_BLOCK] = qx_ref[:, i * LANES:(i + 1) * LANES].astype(F32).T.astype(BF16)
            qat_scr[LANES:, i * Q_BLOCK:(i + 1) * Q_BLOCK] = nit
    m_scr[...] = jnp.full(m_scr.shape, -jnp.inf, F32)
    acc_scr[...] = jnp.zeros(acc_scr.shape, F32)
    last = (qi * Q_BLOCK) // SEL_TK

    def scores(st_ref, kt):
        k0 = pl.multiple_of(kt * SEL_TK, SEL_TK)
        st_ref[...] = _dot(ka_ref[pl.ds(k0, SEL_TK), :], qat_scr[...])

    def attend(st_ref, kt, on_diagonal):
        k0 = pl.multiple_of(kt * SEL_TK, SEL_TK)
        vt = vt_ref[:, pl.ds(k0, SEL_TK)]
        m_old = m_scr[...]
        m_news, pvs = [], []
        for c in range(ncol // SEL_TC):
            cs = slice(c * SEL_TC, (c + 1) * SEL_TC)
            st = st_ref[:, cs]
            if on_diagonal:
                kpos = k0 + lax.broadcasted_iota(jnp.int32, (SEL_TK, 1), 0)
                qpos = qi * Q_BLOCK + lax.broadcasted_iota(jnp.int32, (1, SEL_TC), 1) % Q_BLOCK
                st = jnp.where(kpos <= qpos, st, NEG)
            m_new = jnp.maximum(m_old[:, cs], jnp.max(st, axis=0, keepdims=True))
            pvs.append(_dot(vt, jnp.exp2(st - m_new).astype(BF16)))
            m_news.append(m_new)
        m_new = jnp.concatenate(m_news, axis=1)
        acc_scr[...] = jnp.exp2(m_old - m_new) * acc_scr[...] + jnp.concatenate(pvs, axis=1)
        m_scr[...] = m_new

    scores(sa_scr, 0)

    def tile_pair(pp, c):
        scores(sb_scr, 2 * pp + 1)
        attend(sa_scr, 2 * pp, False)
        scores(sa_scr, 2 * pp + 2)
        attend(sb_scr, 2 * pp + 1, False)
        return c
    lax.fori_loop(0, last // 2, tile_pair, 0)

    @pl.when(last % 2 == 0)
    def _():
        attend(sa_scr, last, True)

    @pl.when(last % 2 == 1)
    def _():
        scores(sb_scr, last)
        attend(sa_scr, last - 1, False)
        attend(sb_scr, last, True)

    ot = acc_scr[:LANES, :] / acc_scr[LANES:LANES + 1, :]
    for g in range(GROUP):
        cols = lambda h: slice((h * GROUP + g) * Q_BLOCK, (h * GROUP + g + 1) * Q_BLOCK)
        blk = jnp.concatenate([ot[h * HEAD_DIM:(h + 1) * HEAD_DIM, cols(h)] for h in range(N_KV_HEADS)], axis=0)
        gsel = jnp.where(lane < HEAD_DIM, _gate_col(gate_ref, g * 3 + 1), _gate_col(gate_ref, (GROUP + g) * 3 + 1))
        o_ref[0, g] = blk.T * gsel


def _prompt_sel(qx, ni, gate, ka, vst, b, t):
    nq = t // Q_BLOCK
    ncol = N_Q_HEADS * Q_BLOCK
    qrow = lambda n: pl.BlockSpec((Q_BLOCK, n), lambda bi, qi: (bi * nq + qi, 0))
    return pl.pallas_call(
        _prompt_sel_body,
        grid=(b, nq),
        in_specs=[qrow(N_Q_HEADS * LANES), qrow(LANES), qrow(LANES),
                  pl.BlockSpec((t, 2 * LANES), lambda bi, qi: (bi, 0)), pl.BlockSpec((V_ROWS, t), lambda bi, qi: (0, bi))],
        out_specs=pl.BlockSpec((1, GROUP, Q_BLOCK, LANES), lambda bi, qi: (bi, 0, qi, 0)),
        out_shape=jax.ShapeDtypeStruct((b, GROUP, t, LANES), F32),
        scratch_shapes=[pltpu.VMEM((2 * LANES, ncol), BF16), pltpu.VMEM((1, ncol), F32), pltpu.VMEM((V_ROWS, ncol), F32),
                        pltpu.VMEM((SEL_TK, ncol), F32), pltpu.VMEM((SEL_TK, ncol), F32)],
        compiler_params=pltpu.CompilerParams(dimension_semantics=("arbitrary", "arbitrary"), vmem_limit_bytes=VMEM_LIMIT),
        name="prompt_sel",
    )(qx, ni, gate, ka, vst)


N_BACK = WINDOW // Q_BLOCK


def _prompt_win_body(qx_ref, gate_ref, *refs):
    k_refs, v_refs, o_ref = refs[:N_BACK + 1], refs[N_BACK + 1:2 * N_BACK + 2], refs[-1]
    qi = pl.program_id(1)
    rq = GROUP * Q_BLOCK
    lane = lax.broadcasted_iota(jnp.int32, (1, LANES), 1)
    qpos = qi * Q_BLOCK + lax.broadcasted_iota(jnp.int32, (rq, 1), 0) % Q_BLOCK
    kpos = jnp.concatenate([(qi - N_BACK + i) * Q_BLOCK + lane for i in range(N_BACK + 1)], axis=1)
    d = qpos - kpos
    mask = (d >= 0) & (d <= WINDOW) & (kpos >= 0)
    outs = []
    for h in range(N_KV_HEADS):
        qh = jnp.concatenate([qx_ref[:, (h * GROUP + g) * LANES:(h * GROUP + g + 1) * LANES] for g in range(GROUP)], axis=0)
        s = jnp.concatenate([_dot_nt(qh, k_refs[i][...]) for i in range(N_BACK + 1)], axis=1)
        p = _masked_softmax(s, mask, jnp.exp2).astype(BF16)
        o = _dot(p[:, :Q_BLOCK], v_refs[0][...])
        for i in range(1, N_BACK + 1):
            o = o + _dot(p[:, i * Q_BLOCK:(i + 1) * Q_BLOCK], v_refs[i][...])
        gcol = jnp.concatenate([_gate_col(gate_ref, (h * GROUP + g) * 3 + 2) for g in range(GROUP)], axis=0)
        outs.append(o * gcol)
    o_ref[0] = jnp.where(lane < HEAD_DIM, outs[0], outs[1]).reshape(GROUP, Q_BLOCK, LANES)


def _prompt_win(qx, gate, kw, vw, b, t):
    nq = t // Q_BLOCK
    qrow = lambda n: pl.BlockSpec((Q_BLOCK, n), lambda bi, qi: (bi * nq + qi, 0))
    kblk = lambda i: pl.BlockSpec((Q_BLOCK, LANES), lambda bi, qi: (bi * nq + jnp.maximum(qi - N_BACK + i, 0), 0))
    return pl.pallas_call(
        _prompt_win_body,
        grid=(b, nq),
        in_specs=[qrow(N_Q_HEADS * LANES), qrow(LANES)] + [kblk(i) for i in range(N_BACK + 1)] * 2,
        out_specs=pl.BlockSpec((1, GROUP, Q_BLOCK, LANES), lambda bi, qi: (bi, 0, qi, 0)),
        out_shape=jax.ShapeDtypeStruct((b, GROUP, t, LANES), F32),
        compiler_params=pltpu.CompilerParams(dimension_semantics=("arbitrary", "arbitrary"), vmem_limit_bytes=VMEM_LIMIT),
        name="prompt_win",
    )(qx, gate, *([kw] * (N_BACK + 1)), *([vw] * (N_BACK + 1)))


def _rglru_coeffs(xc, wra_ref, bra_ref, wrx_ref, brx_ref, lam_ref):
    rs, is_ = [], []
    for s in range(D_RNN // LANES):
        xs = xc[:, s * LANES:(s + 1) * LANES].astype(BF16)
        rs.append(_dot(xs, wra_ref[s]))
        is_.append(_dot(xs, wrx_ref[s]))
    r = _sigmoid(jnp.concatenate(rs, axis=1) + bra_ref[...])
    i = _sigmoid(jnp.concatenate(is_, axis=1) + brx_ref[...])
    z = -lam_ref[...]
    softplus = jnp.maximum(z, 0.0) + jnp.log1p(jnp.exp(-jnp.abs(z)))
    log_a = -RG_C * r * softplus
    a = jnp.exp(log_a)
    u = jnp.sqrt(-jnp.tanh(log_a) * (a * a + 1.0)) * (i * xc)
    return a, u


RNN_TC = 256


def _prompt_rnn_body(xr_ref, zr_ref, cw_ref, cb_ref, wra_ref, bra_ref, wrx_ref, brx_ref, lam_ref,
                     rb_ref, conv_ref, h_ref, xp_scr, hc_scr):
    tc = pl.program_id(1)

    @pl.when(tc == 0)
    def _():
        xp_scr[0:8, :] = jnp.zeros((8, D_RNN), F32)
        hc_scr[...] = jnp.zeros(hc_scr.shape, F32)

    x = xr_ref[...]
    xp_scr[8:8 + RNN_TC, :] = x
    xc = cb_ref[...]
    for k in range(CONV_W):
        off = 8 - (CONV_W - 1) + k
        xc = xc + xp_scr[off:off + RNN_TC, :] * cw_ref[k:k + 1, :]
    xp_scr[0:8, :] = x[RNN_TC - 8:, :]
    conv_ref[0] = x[RNN_TC - 8:, :]

    a, u = _rglru_coeffs(xc, wra_ref, bra_ref, wrx_ref, brx_ref, lam_ref)
    row = lax.broadcasted_iota(jnp.int32, (RNN_TC, 1), 0)
    d = 1
    while d < RNN_TC:
        keep = row >= d
        a_sh = jnp.where(keep, pltpu.roll(a, d, 0), 1.0)
        u_sh = jnp.where(keep, pltpu.roll(u, d, 0), 0.0)
        u = a * u_sh + u
        a = a * a_sh
        d *= 2
    h = u + a * hc_scr[...]
    hc_scr[...] = h[RNN_TC - 1:, :]
    h_ref[0] = h[RNN_TC - 1:, :]
    rb_ref[...] = (h * zr_ref[...]).astype(BF16)


def _prompt_rnn(xr, zr, rw, b, t):
    ntc = t // RNN_TC
    const = lambda a: pl.BlockSpec(a.shape, lambda bi, ti: (0,) * a.ndim)
    row = pl.BlockSpec((RNN_TC, D_RNN), lambda bi, ti: (bi * ntc + ti, 0))
    return pl.pallas_call(
        _prompt_rnn_body,
        grid=(b, ntc),
        in_specs=[row, row] + [const(a) for a in rw],
        out_specs=[row, pl.BlockSpec((1, 8, D_RNN), lambda bi, ti: (bi, 0, 0)),
                   pl.BlockSpec((1, 1, D_RNN), lambda bi, ti: (bi, 0, 0))],
        out_shape=[jax.ShapeDtypeStruct((b * t, D_RNN), BF16), jax.ShapeDtypeStruct((b, 8, D_RNN), F32),
                   jax.ShapeDtypeStruct((b, 1, D_RNN), F32)],
        scratch_shapes=[pltpu.VMEM((8 + RNN_TC, D_RNN), F32), pltpu.VMEM((1, D_RNN), F32)],
        compiler_params=pltpu.CompilerParams(dimension_semantics=("arbitrary", "arbitrary"), vmem_limit_bytes=VMEM_LIMIT),
        name="prompt_rnn",
    )(xr, zr, *rw)


def _sample_rnn_body(xr_ref, zr_ref, sc_ref, h0_ref, cw_ref, cb_ref, wra_ref, bra_ref, wrx_ref, brx_ref, lam_ref,
                     rb_ref, conv_ref, h_ref):
    x = xr_ref[...]
    xc = cb_ref[...]
    for k in range(CONV_W - 1):
        xc = xc + sc_ref[k] * cw_ref[k:k + 1, :]
    xc = xc + x * cw_ref[CONV_W - 1:CONV_W, :]
    for k in range(CONV_W - 2):
        conv_ref[k] = sc_ref[k + 1]
    conv_ref[CONV_W - 2] = x
    a, u = _rglru_coeffs(xc, wra_ref, bra_ref, wrx_ref, brx_ref, lam_ref)
    h = a * h0_ref[...] + u
    h_ref[...] = h
    rb_ref[...] = (h * zr_ref[...]).astype(BF16)


def _sample_rnn(xr, zr, sc, h0, rw):
    n = xr.shape[0]
    return pl.pallas_call(
        _sample_rnn_body,
        out_shape=[jax.ShapeDtypeStruct((n, D_RNN), BF16), jax.ShapeDtypeStruct((CONV_W - 1, n, D_RNN), F32),
                   jax.ShapeDtypeStruct((n, D_RNN), F32)],
        compiler_params=pltpu.CompilerParams(vmem_limit_bytes=VMEM_LIMIT),
        name="sample_rnn",
    )(xr, zr, sc, h0, *rw)


def _outproj_body(x_ref, oc_ref, os_ref, ow_ref, zn_ref, rb_ref, gm_ref, wpa_ref, wpb_ref, wo_ref, y_ref):
    pa = None
    for g in range(GROUP):
        o = (oc_ref[0, g] + os_ref[0, g]) + ow_ref[0, g]
        a = (o * zn_ref[:, g * LANES:(g + 1) * LANES]).astype(BF16)
        term = _dot(a, wpa_ref[g])
        pa = term if pa is None else pa + term
    pb = _dot(rb_ref[...], wpb_ref[...])
    merged = gm_ref[:, :D_MODEL] * pa + gm_ref[:, D_MODEL:] * pb
    y_ref[...] = x_ref[...] + _dot(merged.astype(BF16), wo_ref[...])


def _outproj(x2d, oc, osel, ow, zn, rb, gm, wpa, wpb, wo, tm):
    b, _, t, _ = oc.shape
    nt = t // tm
    row = lambda n: pl.BlockSpec((tm, n), lambda i: (i, 0))
    oblk = pl.BlockSpec((1, GROUP, tm, LANES), lambda i: (i // nt, 0, i % nt, 0))
    const = lambda a: pl.BlockSpec(a.shape, lambda i: (0,) * a.ndim)
    return pl.pallas_call(
        _outproj_body,
        grid=(b * nt,),
        in_specs=[row(D_MODEL), oblk, oblk, oblk, row(NSA_WIDTH), row(D_RNN), row(2 * D_MODEL), const(wpa), const(wpb), const(wo)],
        out_specs=row(D_MODEL),
        out_shape=jax.ShapeDtypeStruct(x2d.shape, F32),
        compiler_params=pltpu.CompilerParams(dimension_semantics=("arbitrary",), vmem_limit_bytes=VMEM_LIMIT),
        name="outproj",
    )(x2d, oc, osel, ow, zn, rb, gm, wpa, wpb, wo)


def _sample_cmp_body(pt_ref, cache_ref, qbd_ref, gate_ref, wbdk_ref, w2k_ref, pek_ref, w1tk_ref, wbdv_ref, w2v_ref, pev_ref,
                     w1tv_ref, gkc_ref, cos_ref, s1_ref, s2_ref, seg_ref, mimp_ref, perm_ref,
                     oc_ref, idx_ref, buf, sem, rk, rv, pb_scr, *, n_pages, past, ns, n_top):
    s = pl.program_id(0)
    n_seq = pl.num_programs(0)
    slot = s % 2
    sub_per_page = PAGE_SIZE // CMP_STRIDE
    n_sub = n_pages * sub_per_page
    nsp = mimp_ref.shape[1]

    def page_copy(sl, seq, p):
        return pltpu.make_async_copy(cache_ref.at[pt_ref[seq * n_pages + p]], buf.at[sl, p], sem.at[sl])

    def start_all(sl, seq):
        lax.fori_loop(0, n_pages, lambda p, c: (page_copy(sl, seq, p).start(), c)[1], 0)

    @pl.when(s == 0)
    def _():
        start_all(0, 0)
        pb_scr[0:1, :] = _pos_bias(pek_ref, w1tk_ref)
        pb_scr[1:2, :] = _pos_bias(pev_ref, w1tv_ref)

    @pl.when(s + 1 < n_seq)
    def _():
        start_all(1 - slot, s + 1)

    lax.fori_loop(0, n_pages, lambda p, c: (page_copy(slot, s, p).wait(), c)[1], 0)

    def regroup(pp, c):
        ya = _dot_nt(perm_ref[...], buf[slot, 2 * pp].astype(BF16))
        yb = _dot_nt(perm_ref[...], buf[slot, 2 * pp + 1].astype(BF16))
        r0 = pl.multiple_of(pp * (2 * sub_per_page), 2 * sub_per_page)
        for j in range(CMP_STRIDE):
            rows = slice(j * sub_per_page, (j + 1) * sub_per_page)
            y = jnp.concatenate([ya[rows], yb[rows]], axis=0).astype(BF16)
            rk[pl.ds(r0, 2 * sub_per_page), j * LANES:(j + 1) * LANES] = y[:, :LANES]
            rv[pl.ds(r0, 2 * sub_per_page), j * LANES:(j + 1) * LANES] = y[:, LANES:]
        return c
    lax.fori_loop(0, n_pages // 2, regroup, 0, unroll=4)

    kc = _compress_finish(_dot(rk[...], wbdk_ref[...]), n_sub, w2k_ref, pb_scr[0:1, :])
    kc = _headnorm_rope(kc, gkc_ref[...], cos_ref[...], s1_ref[...], s2_ref[...], seg_ref[...]).astype(BF16)
    vc = _compress_finish(_dot(rv[...], wbdv_ref[...]), n_sub, w2v_ref, pb_scr[1:2, :]).astype(BF16)

    nr = N_Q_HEADS
    c_end = lax.broadcasted_iota(jnp.int32, (1, n_sub), 1) * CMP_STRIDE + (CMP_BLOCK - 1)
    p = _masked_softmax(_dot_nt(qbd_ref[0].astype(BF16), kc), jnp.broadcast_to(c_end <= past, (nr, n_sub)))
    o = _dot(p.astype(BF16), vc)
    lane = lax.broadcasted_iota(jnp.int32, (nr, LANES), 1)
    rowh = lax.broadcasted_iota(jnp.int32, (nr, LANES), 0) // GROUP
    oc_ref[0] = jnp.where(lane // HEAD_DIM == rowh, o * _sigmoid(gate_ref[0]), 0.0)

    psum = jnp.concatenate([jnp.sum(p[h * GROUP:(h + 1) * GROUP], axis=0, keepdims=True) for h in range(N_KV_HEADS)]
                           + [jnp.zeros((nr - N_KV_HEADS, n_sub), F32)], axis=0)
    jl = lax.broadcasted_iota(jnp.int32, (1, nsp), 1)
    cur = past // SEL_BLOCK
    forced = (jl == 0) | (jl == cur) | (jl == cur - 1)
    v = _dot_exact01(psum, mimp_ref[...]) + jnp.where(forced, FORCE_BONUS, 0.0)
    v = jnp.where(jl < ns, v, -1.0)
    vt = v.T
    js = lax.broadcasted_iota(jnp.int32, (nsp, 1), 0)
    rl = lax.broadcasted_iota(jnp.int32, (1, LANES), 1)
    idx_rows = []
    for h in range(N_KV_HEADS):
        vrow, vcol = v[h:h + 1, :], vt[:, h:h + 1]
        before = (vrow > vcol) | ((vrow == vcol) & (jl < js))
        rank = jnp.sum(jnp.where(before, 1.0, 0.0), axis=1, keepdims=True)
        hit = rank == rl.astype(F32)
        idx_rows.append(jnp.sum(jnp.where(hit, js, 0), axis=0, keepdims=True))
    idx_ref[0] = jnp.concatenate(idx_rows + [jnp.zeros((nr - N_KV_HEADS, LANES), jnp.int32)], axis=0)


def _sample_cmp(pt_flat, cache_t, qbd, gate_c, cw, gkc, ctabs, seg, mimp, n_seq, n_pages, past, ns, n_top):
    sub_per_page = PAGE_SIZE // CMP_STRIDE
    n_sub = n_pages * sub_per_page
    const = lambda a: pl.BlockSpec(a.shape, lambda s, pt: (0,) * a.ndim)
    seqblk = pl.BlockSpec((1, N_Q_HEADS, LANES), lambda s, pt: (s, 0, 0))
    tokens = np.arange(PAGE_SIZE)
    dst_row = (tokens % CMP_STRIDE) * sub_per_page + tokens // CMP_STRIDE
    perm = jnp.asarray(np.arange(PAGE_SIZE)[:, None] == dst_row[None, :], BF16)
    flat = lambda w: w.reshape(CMP_STRIDE * LANES, 2 * LANES)
    cw = (flat(cw[0]), *cw[1:4], flat(cw[4]), *cw[5:])
    consts = [*cw, gkc, *ctabs, seg, mimp, perm]
    return pl.pallas_call(
        functools.partial(_sample_cmp_body, n_pages=n_pages, past=past, ns=ns, n_top=n_top),
        grid_spec=pltpu.PrefetchScalarGridSpec(
            num_scalar_prefetch=1,
            grid=(n_seq,),
            in_specs=[pl.BlockSpec(memory_space=pl.ANY), seqblk, seqblk] + [const(a) for a in consts],
            out_specs=[seqblk, seqblk],
            scratch_shapes=[pltpu.VMEM((2, n_pages, KV_COLS, PAGE_SIZE), F32), pltpu.SemaphoreType.DMA((2,)),
                            pltpu.VMEM((n_sub, CMP_STRIDE * LANES), BF16), pltpu.VMEM((n_sub, CMP_STRIDE * LANES), BF16),
                            pltpu.VMEM((8, LANES), F32)],
        ),
        out_shape=[jax.ShapeDtypeStruct((n_seq, N_Q_HEADS, LANES), F32), jax.ShapeDtypeStruct((n_seq, N_Q_HEADS, LANES), jnp.int32)],
        compiler_params=pltpu.CompilerParams(dimension_semantics=("arbitrary",), vmem_limit_bytes=VMEM_LIMIT),
        name="sample_cmp",
    )(pt_flat, cache_t, qbd, gate_c, *consts)


def _decode_attend(q_col, kts, vts, mask, k_new, v_new):
    s = jnp.sum(kts * q_col[None], axis=1)
    s_new = jnp.sum(q_col * k_new, axis=0, keepdims=True)
    if mask is not None:
        s = jnp.where(mask, s, -jnp.inf)
    m = jnp.maximum(jnp.max(jnp.max(s, axis=1, keepdims=True), axis=0, keepdims=True), s_new)
    e = jnp.exp(s - m)
    if mask is not None:
        e = jnp.where(mask, e, 0.0)
    e_new = jnp.exp(s_new - m)
    den = jnp.sum(jnp.sum(e, axis=1, keepdims=True), axis=0, keepdims=True) + e_new
    acc = jnp.sum(e[:, None, :] * vts, axis=0)
    num = jnp.sum(acc, axis=1, keepdims=True) + e_new * v_new
    return num / den


def _sample_sel_body(pg_ref, meta_ref, cache_ref, q_ref, kv_ref, gate_ref, o_ref, buf, sem, *, n_top):
    s = pl.program_id(0)
    n_seq = pl.num_programs(0)
    slot = s % 2
    n_slots = N_KV_HEADS * n_top

    def page_copy(sl, seq, i):
        return pltpu.make_async_copy(cache_ref.at[pg_ref[seq * n_slots + i]], buf.at[sl, i], sem.at[sl])

    def start_all(sl, seq):
        lax.fori_loop(0, n_slots, lambda i, c: (page_copy(sl, seq, i).start(), c)[1], 0)

    @pl.when(s == 0)
    def _():
        start_all(0, 0)

    @pl.when(s + 1 < n_seq)
    def _():
        start_all(1 - slot, s + 1)

    lax.fori_loop(0, n_slots, lambda i, c: (page_copy(slot, s, i).wait(), c)[1], 0)

    lane = lax.broadcasted_iota(jnp.int32, (1, PAGE_SIZE), 1)
    slot_row = lax.broadcasted_iota(jnp.int32, (n_top, 1), 0)
    for h in range(N_KV_HEADS):
        lo = jnp.zeros((n_top, 1), jnp.int32)
        for r in range(n_top):
            lo = jnp.where(slot_row == r, meta_ref[s * n_slots + h * n_top + r] * SEL_BLOCK, lo)
        mask = (lane >= lo) & (lane < lo + SEL_BLOCK) & (lane < PAGE_SIZE)
        kts = buf[slot, h * n_top:(h + 1) * n_top, h * HEAD_DIM:(h + 1) * HEAD_DIM, :]
        vts = buf[slot, h * n_top:(h + 1) * n_top, (N_KV_HEADS + h) * HEAD_DIM:(N_KV_HEADS + h + 1) * HEAD_DIM, :]
        k_new = kv_ref[0, h * HEAD_DIM:(h + 1) * HEAD_DIM, :]
        v_new = kv_ref[0, (N_KV_HEADS + h) * HEAD_DIM:(N_KV_HEADS + h + 1) * HEAD_DIM, :]
        for g in range(GROUP):
            c0 = (h * GROUP + g) * HEAD_DIM
            o = _decode_attend(q_ref[0, c0:c0 + HEAD_DIM, :], kts, vts, mask, k_new, v_new)
            o_ref[0, c0:c0 + HEAD_DIM, :] = o * _sigmoid(gate_ref[0, c0:c0 + HEAD_DIM, :])


def _sample_sel(pg, meta, cache_t, qcol, kvcol, gexp, n_seq, n_top):
    col = lambda n: pl.BlockSpec((1, n, 1), lambda s, pg, meta: (s, 0, 0))
    return pl.pallas_call(
        functools.partial(_sample_sel_body, n_top=n_top),
        grid_spec=pltpu.PrefetchScalarGridSpec(
            num_scalar_prefetch=2,
            grid=(n_seq,),
            in_specs=[pl.BlockSpec(memory_space=pl.ANY), col(NSA_WIDTH), col(KV_COLS), col(NSA_WIDTH)],
            out_specs=col(NSA_WIDTH),
            scratch_shapes=[pltpu.VMEM((2, N_KV_HEADS * n_top, KV_COLS, PAGE_SIZE), F32), pltpu.SemaphoreType.DMA((2,))],
        ),
        out_shape=jax.ShapeDtypeStruct((n_seq, NSA_WIDTH, 1), F32),
        compiler_params=pltpu.CompilerParams(dimension_semantics=("arbitrary",), vmem_limit_bytes=VMEM_LIMIT),
        name="sample_sel",
    )(pg, meta, cache_t, qcol, kvcol, gexp)


def _sample_win_body(win_ref, q_ref, kv_ref, gate_ref, o_ref, wout_ref):
    w = win_ref[0]
    new = kv_ref[0]
    lane = lax.broadcasted_iota(jnp.int32, (1, WINDOW), 1)
    wout_ref[0] = jnp.where(lane == WINDOW - 1, new, pltpu.roll(w, WINDOW - 1, 1))
    for h in range(N_KV_HEADS):
        kts = w[h * HEAD_DIM:(h + 1) * HEAD_DIM, :][None]
        vts = w[(N_KV_HEADS + h) * HEAD_DIM:(N_KV_HEADS + h + 1) * HEAD_DIM, :][None]
        k_new = new[h * HEAD_DIM:(h + 1) * HEAD_DIM, :]
        v_new = new[(N_KV_HEADS + h) * HEAD_DIM:(N_KV_HEADS + h + 1) * HEAD_DIM, :]
        for g in range(GROUP):
            c0 = (h * GROUP + g) * HEAD_DIM
            o = _decode_attend(q_ref[0, c0:c0 + HEAD_DIM, :], kts, vts, None, k_new, v_new)
            o_ref[0, c0:c0 + HEAD_DIM, :] = o * _sigmoid(gate_ref[0, c0:c0 + HEAD_DIM, :])


def _sample_win(win_t, qcol, kvcol, gexp):
    n_seq = win_t.shape[0]
    col = lambda n: pl.BlockSpec((1, n, 1), lambda s: (s, 0, 0))
    wblk = pl.BlockSpec((1, KV_COLS, WINDOW), lambda s: (s, 0, 0))
    return pl.pallas_call(
        _sample_win_body,
        grid=(n_seq,),
        in_specs=[wblk, col(NSA_WIDTH), col(KV_COLS), col(NSA_WIDTH)],
        out_specs=[col(NSA_WIDTH), wblk],
        out_shape=[jax.ShapeDtypeStruct((n_seq, NSA_WIDTH, 1), F32), jax.ShapeDtypeStruct(win_t.shape, F32)],
        compiler_params=pltpu.CompilerParams(dimension_semantics=("arbitrary",), vmem_limit_bytes=VMEM_LIMIT),
        name="sample_win",
    )(win_t, qcol, kvcol, gexp)


def _prep_inproj_weight(w_in):
    sizes = (NSA_WIDTH, 3 * KV_COLS, 3 * N_Q_HEADS, NSA_WIDTH, D_RNN, D_RNN, 2 * D_MODEL)
    offs = np.concatenate([[0], np.cumsum(sizes)])
    q, kv, gn, zn, xr, zr, gm = [w_in[:, offs[i]:offs[i + 1]] for i in range(len(sizes))]
    zn = zn.reshape(D_MODEL, N_KV_HEADS, GROUP, HEAD_DIM).transpose(0, 2, 1, 3).reshape(D_MODEL, NSA_WIDTH)
    pad = jnp.zeros((D_MODEL, LANES - 3 * N_Q_HEADS), w_in.dtype)
    return jnp.concatenate([q, kv, zn, xr, zr, gm, gn, pad], axis=1).astype(BF16)


def _prep_compress(pe, w1, w2):
    eye = jnp.eye(N_KV_HEADS, dtype=F32)
    bd = lambda w: jnp.einsum('jde,hk->jhdke', w, eye).reshape(CMP_STRIDE, LANES, LANES)
    wbd = jnp.concatenate([bd(w1[:CMP_STRIDE]), bd(w1[CMP_STRIDE:])], axis=2).astype(BF16)
    w2bd = jnp.einsum('ed,hk->hekd', w2, eye).reshape(LANES, LANES).astype(BF16)
    pe_col = pe.reshape(CMP_BLOCK * HEAD_DIM, 1)
    w1t = jnp.tile(w1.reshape(CMP_BLOCK * HEAD_DIM, CMP_HIDDEN), (1, N_KV_HEADS))
    return wbd, w2bd, pe_col, w1t


def _prep_rnn(conv_w, conv_b, w_ra, b_ra, w_rx, b_rx, lam):
    def pairs(w):
        w = w.reshape(RNN_HEADS // 2, 2, RNN_HD, RNN_HD)
        eye = jnp.eye(2, dtype=F32)
        return jnp.einsum('shij,hk->shikj', w, eye).reshape(RNN_HEADS // 2, LANES, LANES).astype(BF16)
    return (conv_w, conv_b.reshape(1, D_RNN), pairs(w_ra), b_ra.reshape(1, D_RNN), pairs(w_rx), b_rx.reshape(1, D_RNN),
            lam.reshape(1, D_RNN))


def _feature_major(cache):
    n, r = cache.shape[:2]
    return cache.transpose(0, 2, 3, 4, 1).reshape(n, KV_COLS, r)


def kernel(x_prompt, x_sample, cache_kv_cmp, cache_kv_sel, cache_kv_win, state_conv, state_h, page_table, g_norm, w_in, g_q, g_kc, g_ks, g_kw, pe_k, w1_k, w2_k, pe_v, w1_v, w2_v, conv_w, conv_b, w_ra, b_ra, w_rx, b_rx, lam, w_pa, w_pb, w_out):
    b, t, _ = x_prompt.shape
    n_seq = x_sample.shape[0]
    n_pages = page_table.shape[1]
    past = n_pages * PAGE_SIZE
    assert x_sample.shape[1] == 1 and cache_kv_win.shape[1] == WINDOW and past >= WINDOW
    assert t % SEL_TK == 0 and t // SEL_BLOCK <= HEAD_DIM and t >= WINDOW and n_pages % 2 == 0

    wp = _prep_inproj_weight(w_in)
    seg = _seg01()
    tile2 = lambda g: jnp.tile(g.reshape(1, HEAD_DIM), (1, 2))
    gq, gks, gkw, gkc = tile2(g_q), tile2(g_ks), tile2(g_kw), tile2(g_kc)
    gnorm = g_norm.reshape(1, D_MODEL)
    cw = (*_prep_compress(pe_k, w1_k, w2_k), *_prep_compress(pe_v, w1_v, w2_v))
    rw = _prep_rnn(conv_w, conv_b, w_ra, b_ra, w_rx, b_rx, lam)
    wpa = w_pa.reshape(N_KV_HEADS, GROUP, HEAD_DIM, D_MODEL).transpose(1, 0, 2, 3).reshape(GROUP, LANES, D_MODEL).astype(BF16)
    wpb, wo = w_pb.astype(BF16), w_out.astype(BF16)

    tm = 256
    tok = np.arange(t)
    oh = jnp.asarray((tok[:, None] // SEL_BLOCK) == (np.arange(LANES)[None, :] % HEAD_DIM), BF16)
    xp2 = x_prompt.reshape(b * t, D_MODEL)
    (kvc, kvs, kvw, qx, _, ka, kw, vw, vst, zn, xr, zr, gm, gate) = _inproj(
        xp2, gnorm, wp, _rope_tables(tok), oh, gq, gks, gkw, seg, tm, t // tm)
    n_sub = t // CMP_STRIDE
    ns = t // SEL_BLOCK
    mimp = jnp.asarray(_imp_matrix(n_sub, ns, n_sub - 1, ns).T, BF16)
    ctabs = _rope_tables(np.arange(n_sub) * CMP_STRIDE + (CMP_BLOCK - 1))
    oc, ni = _prompt_cmp(kvc, qx, gate, cw, gkc, ctabs, seg, mimp, b, t)
    osel = _prompt_sel(qx, ni, gate, ka, vst, b, t)
    owin = _prompt_win(qx, gate, kw, vw, b, t)
    rb, conv_tail, h_last = _prompt_rnn(xr, zr, rw, b, t)
    y_prompt = _outproj(xp2, oc, osel, owin, zn, rb, gm, wpa, wpb, wo, tm).reshape(b, t, D_MODEL)
    kv5 = lambda a, rows: a.reshape(-1, rows, 2, N_KV_HEADS, HEAD_DIM)
    kv_cmp_prompt, kv_sel_prompt = kv5(kvc, t), kv5(kvs, t)
    kv_win_prompt = kv5(kvw, t)[:, t - min(WINDOW, t):]
    conv_prompt = conv_tail[:, 8 - (CONV_W - 1):]
    h_prompt = h_last.reshape(b, D_RNN)

    xs2 = x_sample.reshape(n_seq, D_MODEL)
    oh_s = jnp.zeros((n_seq, LANES), BF16)
    (kvc_s, kvs_s, kvw_s, _, qf_s, _, _, _, _, zn_s, xr_s, zr_s, gm_s, gate_s) = _inproj(
        xs2, gnorm, wp, _rope_tables(np.full((n_seq,), past)), oh_s, gq, gks, gkw, seg, n_seq, 1)
    eye = jnp.eye(N_KV_HEADS, dtype=F32)
    q4 = qf_s.reshape(n_seq, N_KV_HEADS, GROUP, HEAD_DIM)
    qbd = (q4[:, :, :, None, :] * eye[None, :, None, :, None]).reshape(n_seq, N_Q_HEADS, LANES)
    qcol = qf_s.reshape(n_seq, NSA_WIDTH, 1)
    gates3 = gate_s[:, :3 * N_Q_HEADS].reshape(n_seq, N_Q_HEADS, 3)
    gate_c = jnp.broadcast_to(gates3[:, :, 0:1], (n_seq, N_Q_HEADS, LANES))
    gexp = lambda k: jnp.broadcast_to(gates3[:, :, k:k + 1], (n_seq, N_Q_HEADS, HEAD_DIM)).reshape(n_seq, NSA_WIDTH, 1)

    n_sub_s = past // CMP_STRIDE
    ns_s = past // SEL_BLOCK + 1
    n_top_s = min(N_SEL, ns_s)
    nsp = -(-ns_s // LANES) * LANES
    mimp_s = jnp.asarray(_imp_matrix(n_sub_s, nsp, n_sub_s - 1, ns_s), BF16)
    ctabs_s = _rope_tables(np.arange(n_sub_s) * CMP_STRIDE + (CMP_BLOCK - 1))
    oc_s, idx = _sample_cmp(page_table.reshape(-1), _feature_major(cache_kv_cmp), qbd, gate_c, cw, gkc, ctabs_s, seg, mimp_s,
                            n_seq, n_pages, past, ns_s, n_top_s)
    blocks = idx[:, :N_KV_HEADS, :n_top_s]
    per_page = PAGE_SIZE // SEL_BLOCK
    in_cache = blocks < past // SEL_BLOCK
    pg = jnp.take_along_axis(page_table, jnp.minimum(blocks // per_page, n_pages - 1).reshape(n_seq, -1), axis=1)
    meta = jnp.where(in_cache, blocks % per_page, per_page).reshape(-1).astype(jnp.int32)
    o_sel_s = _sample_sel(pg.reshape(-1).astype(jnp.int32), meta, _feature_major(cache_kv_sel), qcol,
                          kvs_s.reshape(n_seq, KV_COLS, 1), gexp(1), n_seq, n_top_s)
    o_win_s, win_new = _sample_win(_feature_major(cache_kv_win), qcol, kvw_s.reshape(n_seq, KV_COLS, 1), gexp(2))
    sc_t = state_conv.transpose(1, 0, 2)
    rb_s, conv_new, h_sample = _sample_rnn(xr_s, zr_s, sc_t, state_h, rw)
    to_o = lambda o: o.reshape(n_seq, N_KV_HEADS, GROUP, HEAD_DIM).transpose(2, 0, 1, 3).reshape(1, GROUP, n_seq, LANES)
    oc_s4 = (oc_s[:, :GROUP] + oc_s[:, GROUP:]).transpose(1, 0, 2)[None]
    y_sample = _outproj(xs2, oc_s4, to_o(o_sel_s), to_o(o_win_s), zn_s, rb_s, gm_s, wpa, wpb, wo, n_seq).reshape(n_seq, 1, D_MODEL)
    kv_cmp_sample, kv_sel_sample = kv5(kvc_s, 1), kv5(kvs_s, 1)
    kv_win_sample = win_new.reshape(n_seq, 2, N_KV_HEADS, HEAD_DIM, WINDOW).transpose(0, 4, 1, 2, 3)
    conv_sample = conv_new.transpose(1, 0, 2)

    return (y_prompt, y_sample, kv_cmp_prompt, kv_cmp_sample, kv_sel_prompt, kv_sel_sample,
            kv_win_prompt, kv_win_sample, conv_prompt, conv_sample, h_prompt, h_sample)
```

```python
import functools

import numpy as np
import jax
import jax.numpy as jnp
from jax import lax
from jax.experimental import pallas as pl
from jax.experimental.pallas import tpu as pltpu

D_MODEL = 1024
HEAD_DIM = 64
N_Q_HEADS = 8
N_KV_HEADS = 2
GROUP = N_Q_HEADS // N_KV_HEADS
NSA_WIDTH = N_Q_HEADS * HEAD_DIM
SCALE = HEAD_DIM ** -0.5
ROPE_DIM = HEAD_DIM // 4
ROPE_HALF = ROPE_DIM // 2
ROPE_THETA = 500000.0
CMP_BLOCK = 32
CMP_STRIDE = 16
CMP_HIDDEN = 64
SEL_BLOCK = 64
N_SEL = 16
WINDOW = 512
Q_BLOCK = 128
FORCE_BONUS = 1.0e4
D_RNN = D_MODEL // 2
RNN_HEADS = 8
RNN_HD = D_RNN // RNN_HEADS
CONV_W = 4
RG_C = 8.0
EPS = 1e-6
PAGE_SIZE = 128

LANES = 128
KV_COLS = 2 * N_KV_HEADS * HEAD_DIM
NEG = -(2.0 ** 100)
LOG2E = 1.4426950408889634
V_ROWS = LANES + 16
VMEM_LIMIT = 56 * 1024 * 1024
F32 = jnp.float32
BF16 = jnp.bfloat16

C_Q = 0
C_KV = C_Q + NSA_WIDTH
C_ZN = C_KV + 3 * KV_COLS
C_XR = C_ZN + NSA_WIDTH
C_ZR = C_XR + D_RNN
C_GM = C_ZR + D_RNN
C_GN = C_GM + 2 * D_MODEL
N_WCOLS = C_GN + LANES


def _dot(a, b):
    return jnp.dot(a, b, preferred_element_type=F32)


def _dot_nt(a, b):
    return lax.dot_general(a, b, (((1,), (1,)), ((), ())), preferred_element_type=F32)


def _split3(a):
    a1 = a.astype(BF16)
    r1 = a - a1.astype(F32)
    a2 = r1.astype(BF16)
    a3 = (r1 - a2.astype(F32)).astype(BF16)
    return a1, a2, a3


def _dot_exact01(a, b01):
    a1, a2, a3 = _split3(a)
    return _dot(a1, b01) + _dot(a2, b01) + _dot(a3, b01)


def _dot_01_exact(b01, a):
    a1, a2, a3 = _split3(a)
    return _dot(b01, a1) + _dot(b01, a2) + _dot(b01, a3)


def _sigmoid(x):
    return 1.0 / (1.0 + jnp.exp(-x))


def _silu(x):
    return x * _sigmoid(x)


def _headnorm_rope(xs, gain, cos, s1, s2, seg01):
    ss = _dot_exact01(xs * xs, seg01)
    y = xs * lax.rsqrt(ss * (1.0 / HEAD_DIM) + EPS) * gain
    return y * cos + pltpu.roll(y, ROPE_HALF, 1) * s1 + pltpu.roll(y, LANES - ROPE_HALF, 1) * s2


def _masked_softmax(s, mask):
    s = jnp.where(mask, s, -jnp.inf)
    m = jnp.max(s, axis=-1, keepdims=True)
    m = jnp.where(m == -jnp.inf, 0.0, m)
    e = jnp.where(mask, jnp.exp(s - m), 0.0)
    d = jnp.sum(e, axis=-1, keepdims=True)
    return e / jnp.where(d > 0, d, 1.0)


def _masked_softmax_t(s, mask):
    s = jnp.where(mask, s, -jnp.inf)
    m = jnp.max(s, axis=0, keepdims=True)
    m = jnp.where(m == -jnp.inf, 0.0, m)
    e = jnp.where(mask, jnp.exp2(s - m), 0.0)
    d = jnp.sum(e, axis=0, keepdims=True)
    return e / jnp.where(d > 0, d, 1.0)


def _rope_tables(pos):
    pos = np.asarray(pos, np.float64)
    inv = ROPE_THETA ** (-np.arange(ROPE_HALF, dtype=np.float64) / ROPE_HALF)
    ang = (pos.astype(np.float32)[:, None] * inv.astype(np.float32)[None, :]).astype(np.float32).astype(np.float64)
    cos, sin = np.cos(ang), np.sin(ang)
    n = pos.shape[0]
    c = np.ones((n, HEAD_DIM)); s1 = np.zeros((n, HEAD_DIM)); s2 = np.zeros((n, HEAD_DIM))
    c[:, :ROPE_HALF] = cos; c[:, ROPE_HALF:ROPE_DIM] = cos
    s1[:, ROPE_HALF:ROPE_DIM] = sin
    s2[:, :ROPE_HALF] = -sin
    t = lambda a: jnp.asarray(np.tile(a, (1, 2)), F32)
    return t(c), t(s1), t(s2)


def _seg01():
    lane = np.arange(LANES)
    return jnp.asarray((lane[:, None] // HEAD_DIM) == (lane[None, :] // HEAD_DIM), BF16)


def _imp_matrix(nc_pad, ns_pad, nc, ns):
    r = SEL_BLOCK // CMP_STRIDE
    lead = CMP_BLOCK // CMP_STRIDE - 1
    c = np.arange(nc_pad)[:, None]
    j = np.arange(ns_pad)[None, :]
    m = (c >= r * j - lead) & (c <= r * j + r - 1) & (c < nc) & (j < ns)
    return m


def _values_t(v, ref):
    ref[:LANES, :] = v.T.astype(BF16)
    ones_row = lax.broadcasted_iota(jnp.int32, (V_ROWS - LANES, v.shape[0]), 0) == 0
    ref[LANES:, :] = jnp.where(ones_row, 1.0, 0.0).astype(BF16)


def _inproj_body(x_ref, gn_ref, w_ref, cos_ref, s1_ref, s2_ref, oh_ref, gq_ref, gks_ref, gkw_ref, seg_ref, *out_refs, prompt):
    if prompt:
        (kvc_ref, kvct_ref, kvst_ref, kvwt_ref, qxt_ref, ka_ref, kw_ref, vst_ref, vwt_ref, gatet_ref,
         zn_ref, xr_ref, zr_ref, gm_ref) = out_refs
    else:
        kvc_ref, kvs_ref, kvw_ref, qf_ref, gate_ref, zn_ref, xr_ref, zr_ref, gm_ref = out_refs
    x = x_ref[...]
    r = lax.rsqrt(jnp.mean(x * x, axis=-1, keepdims=True) + EPS)
    u = (x * r * gn_ref[...]).astype(BF16)
    cos, s1, s2, seg = cos_ref[...], s1_ref[...], s2_ref[...], seg_ref[...]
    lane = lax.broadcasted_iota(jnp.int32, (1, LANES), 1)

    def proj(c0, n):
        return _dot(u, w_ref[:, c0:c0 + n])

    q = proj(C_Q, NSA_WIDTH)
    for s in range(NSA_WIDTH // LANES):
        qs = _headnorm_rope(q[:, s * LANES:(s + 1) * LANES], gq_ref[...], cos, s1, s2, seg) * SCALE
        if not prompt:
            qf_ref[:, s * LANES:(s + 1) * LANES] = qs
            continue
        qs = qs * LOG2E
        qs_sw = pltpu.roll(qs, HEAD_DIM, 1)
        for half in range(2):
            i = 2 * s + half
            h = i // GROUP
            src = qs if half == h else qs_sw
            keep = (lane >= h * HEAD_DIM) & (lane < (h + 1) * HEAD_DIM)
            qxt_ref[i * LANES:(i + 1) * LANES, :] = jnp.where(keep, src, 0.0).T.astype(BF16)

    kv = proj(C_KV, 3 * KV_COLS)
    kc, vc = kv[:, :LANES], kv[:, LANES:KV_COLS]
    ks = _headnorm_rope(kv[:, KV_COLS:KV_COLS + LANES], gks_ref[...], cos, s1, s2, seg)
    vs = kv[:, KV_COLS + LANES:2 * KV_COLS]
    kw = _headnorm_rope(kv[:, 2 * KV_COLS:2 * KV_COLS + LANES], gkw_ref[...], cos, s1, s2, seg)
    vw = kv[:, 2 * KV_COLS + LANES:]
    kvc_ref[...] = kv[:, :KV_COLS]
    if prompt:
        for ref, k, v in ((kvct_ref, kc, vc), (kvst_ref, ks, vs), (kvwt_ref, kw, vw)):
            ref[0, :LANES, :] = k.T
            ref[0, LANES:, :] = v.T
        ka_ref[:, :LANES] = ks.astype(BF16)
        ka_ref[:, LANES:] = oh_ref[...]
        kw_ref[...] = kw.astype(BF16)
        _values_t(vs, vst_ref)
        _values_t(vw, vwt_ref)
        gatet_ref[...] = proj(C_GN, LANES).T
    else:
        kvs_ref[:, :LANES] = ks
        kvs_ref[:, LANES:] = vs
        kvw_ref[:, :LANES] = kw
        kvw_ref[:, LANES:] = vw
        gate_ref[...] = proj(C_GN, LANES)

    zn_ref[...] = _silu(proj(C_ZN, NSA_WIDTH))
    xr_ref[...] = proj(C_XR, D_RNN)
    zr_ref[...] = _silu(proj(C_ZR, D_RNN))
    for c in range(2):
        gm_ref[:, c * D_MODEL:(c + 1) * D_MODEL] = _sigmoid(proj(C_GM + c * D_MODEL, D_MODEL))


def _inproj(x2d, gnorm, wp, tabs, oh, gq, gks, gkw, seg, tm, t, prompt):
    rows = x2d.shape[0]
    nt = t // tm
    row = lambda n: pl.BlockSpec((tm, n), lambda i: (i, 0))
    col = lambda n: pl.BlockSpec((n, tm), lambda i: (0, i))
    tab = pl.BlockSpec((tm, LANES), lambda i: (i % nt, 0))
    const = lambda a: pl.BlockSpec(a.shape, lambda i: (0,) * a.ndim)
    rowshape = lambda n, dt: jax.ShapeDtypeStruct((rows, n), dt)
    colshape = lambda n, dt: jax.ShapeDtypeStruct((n, rows), dt)
    tail = [(row(NSA_WIDTH), rowshape(NSA_WIDTH, F32)), (row(D_RNN), rowshape(D_RNN, F32)), (row(D_RNN), rowshape(D_RNN, F32)),
            (row(2 * D_MODEL), rowshape(2 * D_MODEL, F32))]
    if prompt:
        leaf_t = (pl.BlockSpec((1, KV_COLS, tm), lambda i: (i // nt, 0, i % nt)), jax.ShapeDtypeStruct((rows // t, KV_COLS, t), F32))
        outs = [(row(KV_COLS), rowshape(KV_COLS, F32)), leaf_t, leaf_t, leaf_t,
                (col(N_Q_HEADS * LANES), colshape(N_Q_HEADS * LANES, BF16)),
                (row(2 * LANES), rowshape(2 * LANES, BF16)), (row(LANES), rowshape(LANES, BF16)),
                (col(V_ROWS), colshape(V_ROWS, BF16)), (col(V_ROWS), colshape(V_ROWS, BF16)),
                (col(LANES), colshape(LANES, F32))] + tail
    else:
        outs = [(row(KV_COLS), rowshape(KV_COLS, F32))] * 3 + [(row(NSA_WIDTH), rowshape(NSA_WIDTH, F32)),
                                                               (row(LANES), rowshape(LANES, F32))] + tail
    return pl.pallas_call(
        functools.partial(_inproj_body, prompt=prompt),
        grid=(rows // tm,),
        in_specs=[row(D_MODEL), const(gnorm), const(wp), tab, tab, tab, tab, const(gq), const(gks), const(gkw), const(seg)],
        out_specs=[o[0] for o in outs],
        out_shape=[o[1] for o in outs],
        compiler_params=pltpu.CompilerParams(dimension_semantics=("arbitrary",), vmem_limit_bytes=VMEM_LIMIT),
        name="inproj",
    )(x2d, gnorm, wp, *tabs, oh, gq, gks, gkw, seg)


def _compress_rows(load_rows, n_sub, wbd_ref, w2_ref, pe_ref, w1t_ref):
    acc = jnp.zeros((n_sub, 2 * LANES), F32)
    for j in range(CMP_STRIDE):
        acc = acc + _dot(load_rows(j).astype(BF16), wbd_ref[j])
    return _compress_finish(acc, n_sub, w2_ref, _pos_bias(pe_ref, w1t_ref))


def _pos_bias(pe_ref, w1t_ref):
    return jnp.sum(pe_ref[...] * w1t_ref[...], axis=0, keepdims=True)


def _compress_finish(acc, n_sub, w2_ref, pos_bias):
    lo, hi = acc[:, :LANES], acc[:, LANES:]
    hid = _silu(lo + pltpu.roll(hi, n_sub - 1, 0) + pos_bias)
    return _dot(hid.astype(BF16), w2_ref[...])


SEL_TK = 512
SEL_TC = 256
N_BACK = WINDOW // Q_BLOCK
WIN_KEYS = (N_BACK + 1) * Q_BLOCK


def _prompt_attn_body(kraw_ref, vraw_ref, qxt_ref, gatet_ref, ka_ref, vst_ref, kw_ref, vwt_ref,
                      wbdk_ref, w2k_ref, pek_ref, w1tk_ref, wbdv_ref, w2v_ref, pev_ref, w1tv_ref,
                      gkc_ref, cos_ref, s1_ref, s2_ref, seg_ref, mimp_ref,
                      o_ref, kc_scr, vct_scr, qat_scr, m_scr, acc_scr, sa_scr, sb_scr, osum_scr, *, n_sub, ns, n_top):
    qi = pl.program_id(1)
    ncol = N_Q_HEADS * Q_BLOCK

    @pl.when(qi == 0)
    def _():
        def rows(ref):
            return lambda j: ref[0, pl.ds(j, n_sub, stride=CMP_STRIDE), :]
        kc = _compress_rows(rows(kraw_ref), n_sub, wbdk_ref, w2k_ref, pek_ref, w1tk_ref)
        kc = _headnorm_rope(kc, gkc_ref[...], cos_ref[...], s1_ref[...], s2_ref[...], seg_ref[...])
        kc_scr[...] = kc.astype(BF16)
        vct_scr[...] = _compress_rows(rows(vraw_ref), n_sub, wbdv_ref, w2v_ref, pev_ref, w1tv_ref).T.astype(BF16)

    for i in range(N_Q_HEADS):
        qat_scr[:LANES, i * Q_BLOCK:(i + 1) * Q_BLOCK] = qxt_ref[i * LANES:(i + 1) * LANES, :]
    qt = qat_scr[:LANES, :]
    qpos = qi * Q_BLOCK + lax.broadcasted_iota(jnp.int32, (1, ncol), 1) % Q_BLOCK
    gates = _sigmoid(gatet_ref[...])

    def gate_row(k):
        return jnp.concatenate([gates[i * 3 + k:i * 3 + k + 1, :] for i in range(N_Q_HEADS)], axis=1)

    c_end = lax.broadcasted_iota(jnp.int32, (n_sub, 1), 0) * CMP_STRIDE + (CMP_BLOCK - 1)
    p = _masked_softmax_t(_dot(kc_scr[...], qt), c_end <= qpos)
    osum_scr[...] = _dot(vct_scr[...], p.astype(BF16)) * gate_row(0)

    jrow = lax.broadcasted_iota(jnp.int32, (ns, 1), 0)
    cur = (qi * Q_BLOCK + lax.broadcasted_iota(jnp.int32, (1, Q_BLOCK), 1)) // SEL_BLOCK
    forced = (jrow == 0) | (jrow == cur) | (jrow == cur - 1)
    zeros = jnp.zeros((HEAD_DIM, Q_BLOCK), F32)
    for h in range(N_KV_HEADS):
        pg = [p[:, (h * GROUP + g) * Q_BLOCK:(h * GROUP + g + 1) * Q_BLOCK] for g in range(GROUP)]
        psum = ((pg[0] + pg[1]) + pg[2]) + pg[3]
        v = _dot_01_exact(mimp_ref[...], psum) + jnp.where(forced, FORCE_BONUS, 0.0)
        rank = jnp.zeros((ns, Q_BLOCK), F32)
        for i in range(ns):
            vi = v[i:i + 1, :]
            before = (vi > v) | ((vi == v) & (jrow > i))
            rank = rank + jnp.where(before, 1.0, 0.0)
        neg = jnp.where(rank < n_top, 0.0, NEG)
        if ns < HEAD_DIM:
            neg = jnp.concatenate([neg, jnp.zeros((HEAD_DIM - ns, Q_BLOCK), F32)], axis=0)
        blk = jnp.concatenate([neg, zeros] if h == 0 else [zeros, neg], axis=0).astype(BF16)
        for g in range(GROUP):
            i = h * GROUP + g
            qat_scr[LANES:, i * Q_BLOCK:(i + 1) * Q_BLOCK] = blk

    m_scr[...] = jnp.full(m_scr.shape, -jnp.inf, F32)
    acc_scr[...] = jnp.zeros(acc_scr.shape, F32)
    last = (qi * Q_BLOCK) // SEL_TK

    def scores(st_ref, kt):
        k0 = pl.multiple_of(kt * SEL_TK, SEL_TK)
        st_ref[...] = _dot(ka_ref[pl.ds(k0, SEL_TK), :], qat_scr[...])

    def attend(st_ref, kt, on_diagonal):
        k0 = pl.multiple_of(kt * SEL_TK, SEL_TK)
        vt = vst_ref[:, pl.ds(k0, SEL_TK)]
        m_old = m_scr[...]
        m_news, pvs = [], []
        for c in range(ncol // SEL_TC):
            cs = slice(c * SEL_TC, (c + 1) * SEL_TC)
            st = st_ref[:, cs]
            if on_diagonal:
                kpos = k0 + lax.broadcasted_iota(jnp.int32, (SEL_TK, 1), 0)
                st = jnp.where(kpos <= qpos[:, cs], st, NEG)
            m_new = jnp.maximum(m_old[:, cs], jnp.max(st, axis=0, keepdims=True))
            pvs.append(_dot(vt, jnp.exp2(st - m_new).astype(BF16)))
            m_news.append(m_new)
        m_new = jnp.concatenate(m_news, axis=1)
        acc_scr[...] = jnp.exp2(m_old - m_new) * acc_scr[...] + jnp.concatenate(pvs, axis=1)
        m_scr[...] = m_new

    scores(sa_scr, 0)

    def tile_pair(pp, c):
        scores(sb_scr, 2 * pp + 1)
        attend(sa_scr, 2 * pp, False)
        scores(sa_scr, 2 * pp + 2)
        attend(sb_scr, 2 * pp + 1, False)
        return c
    lax.fori_loop(0, last // 2, tile_pair, 0)

    @pl.when(last % 2 == 0)
    def _():
        attend(sa_scr, last, True)

    @pl.when(last % 2 == 1)
    def _():
        scores(sb_scr, last)
        attend(sa_scr, last - 1, False)
        attend(sb_scr, last, True)

    osum_scr[...] = osum_scr[...] + acc_scr[:LANES, :] / acc_scr[LANES:LANES + 1, :] * gate_row(1)

    w0 = pl.multiple_of(jnp.maximum(qi - N_BACK, 0) * Q_BLOCK, Q_BLOCK)
    d = qpos - (w0 + lax.broadcasted_iota(jnp.int32, (WIN_KEYS, 1), 0))
    sw = jnp.where((d >= 0) & (d <= WINDOW), _dot(kw_ref[pl.ds(w0, WIN_KEYS), :], qt), -jnp.inf)
    e = jnp.exp2(sw - jnp.max(sw, axis=0, keepdims=True))
    ow = _dot(vwt_ref[:, pl.ds(w0, WIN_KEYS)], e.astype(BF16))
    ot = osum_scr[...] + ow[:LANES, :] / ow[LANES:LANES + 1, :] * gate_row(2)

    for g in range(GROUP):
        cols = lambda h: slice((h * GROUP + g) * Q_BLOCK, (h * GROUP + g + 1) * Q_BLOCK)
        blk = jnp.concatenate([ot[h * HEAD_DIM:(h + 1) * HEAD_DIM, cols(h)] for h in range(N_KV_HEADS)], axis=0)
        o_ref[0, g] = blk.T


def _prompt_attn(kvc, qxt, gatet, ka, vst, kw, vwt, cw, gkc, ctabs, seg, mimp, b, t):
    n_sub = t // CMP_STRIDE
    ns = t // SEL_BLOCK
    n_top = min(N_SEL, ns)
    nq = t // Q_BLOCK
    ncol = N_Q_HEADS * Q_BLOCK
    const = lambda a: pl.BlockSpec(a.shape, lambda bi, qi: (0,) * a.ndim)
    qcol = lambda n: pl.BlockSpec((n, Q_BLOCK), lambda bi, qi: (0, bi * nq + qi))
    seq_rows = lambda n: pl.BlockSpec((t, n), lambda bi, qi: (bi, 0))
    seq_cols = lambda n: pl.BlockSpec((n, t), lambda bi, qi: (0, bi))
    consts = [*cw, gkc, *ctabs, seg, mimp]
    kvc3 = kvc.reshape(b, t, KV_COLS)
    return pl.pallas_call(
        functools.partial(_prompt_attn_body, n_sub=n_sub, ns=ns, n_top=n_top),
        grid=(b, nq),
        in_specs=[pl.BlockSpec((1, t, LANES), lambda bi, qi: (bi, 0, 0)), pl.BlockSpec((1, t, LANES), lambda bi, qi: (bi, 0, 1)),
                  qcol(N_Q_HEADS * LANES), qcol(LANES), seq_rows(2 * LANES), seq_cols(V_ROWS), seq_rows(LANES), seq_cols(V_ROWS)]
                 + [const(a) for a in consts],
        out_specs=pl.BlockSpec((1, GROUP, Q_BLOCK, LANES), lambda bi, qi: (bi, 0, qi, 0)),
        out_shape=jax.ShapeDtypeStruct((b, GROUP, t, LANES), F32),
        scratch_shapes=[pltpu.VMEM((n_sub, LANES), BF16), pltpu.VMEM((LANES, n_sub), BF16),
                        pltpu.VMEM((2 * LANES, ncol), BF16), pltpu.VMEM((1, ncol), F32), pltpu.VMEM((V_ROWS, ncol), F32),
                        pltpu.VMEM((SEL_TK, ncol), F32), pltpu.VMEM((SEL_TK, ncol), F32), pltpu.VMEM((LANES, ncol), F32)],
        compiler_params=pltpu.CompilerParams(dimension_semantics=("arbitrary", "arbitrary"), vmem_limit_bytes=VMEM_LIMIT),
        name="prompt_attn",
    )(kvc3, kvc3, qxt, gatet, ka, vst, kw, vwt, *consts)


def _rglru_coeffs(xc, wra_ref, bra_ref, wrx_ref, brx_ref, lam_ref):
    rs, is_ = [], []
    for s in range(D_RNN // LANES):
        xs = xc[:, s * LANES:(s + 1) * LANES].astype(BF16)
        rs.append(_dot(xs, wra_ref[s]))
        is_.append(_dot(xs, wrx_ref[s]))
    r = _sigmoid(jnp.concatenate(rs, axis=1) + bra_ref[...])
    i = _sigmoid(jnp.concatenate(is_, axis=1) + brx_ref[...])
    z = -lam_ref[...]
    softplus = jnp.maximum(z, 0.0) + jnp.log1p(jnp.exp(-jnp.abs(z)))
    log_a = -RG_C * r * softplus
    a = jnp.exp(log_a)
    u = jnp.sqrt(-jnp.tanh(log_a) * (a * a + 1.0)) * (i * xc)
    return a, u


RNN_TC = 256


def _prompt_rnn_body(xr_ref, zr_ref, cw_ref, cb_ref, wra_ref, bra_ref, wrx_ref, brx_ref, lam_ref,
                     rb_ref, conv_ref, h_ref, xp_scr, hc_scr):
    tc = pl.program_id(1)

    @pl.when(tc == 0)
    def _():
        xp_scr[0:8, :] = jnp.zeros((8, D_RNN), F32)
        hc_scr[...] = jnp.zeros(hc_scr.shape, F32)

    x = xr_ref[...]
    xp_scr[8:8 + RNN_TC, :] = x
    xc = cb_ref[...]
    for k in range(CONV_W):
        off = 8 - (CONV_W - 1) + k
        xc = xc + xp_scr[off:off + RNN_TC, :] * cw_ref[k:k + 1, :]
    xp_scr[0:8, :] = x[RNN_TC - 8:, :]
    conv_ref[0] = x[RNN_TC - 8:, :]

    a, u = _rglru_coeffs(xc, wra_ref, bra_ref, wrx_ref, brx_ref, lam_ref)
    row = lax.broadcasted_iota(jnp.int32, (RNN_TC, 1), 0)
    d = 1
    while d < RNN_TC:
        keep = row >= d
        a_sh = jnp.where(keep, pltpu.roll(a, d, 0), 1.0)
        u_sh = jnp.where(keep, pltpu.roll(u, d, 0), 0.0)
        u = a * u_sh + u
        a = a * a_sh
        d *= 2
    h = u + a * hc_scr[...]
    hc_scr[...] = h[RNN_TC - 1:, :]
    h_ref[0] = h[RNN_TC - 1:, :]
    rb_ref[...] = (h * zr_ref[...]).astype(BF16)


def _prompt_rnn(xr, zr, rw, b, t):
    ntc = t // RNN_TC
    const = lambda a: pl.BlockSpec(a.shape, lambda bi, ti: (0,) * a.ndim)
    row = pl.BlockSpec((RNN_TC, D_RNN), lambda bi, ti: (bi * ntc + ti, 0))
    return pl.pallas_call(
        _prompt_rnn_body,
        grid=(b, ntc),
        in_specs=[row, row] + [const(a) for a in rw],
        out_specs=[row, pl.BlockSpec((1, 8, D_RNN), lambda bi, ti: (bi, 0, 0)),
                   pl.BlockSpec((1, 1, D_RNN), lambda bi, ti: (bi, 0, 0))],
        out_shape=[jax.ShapeDtypeStruct((b * t, D_RNN), BF16), jax.ShapeDtypeStruct((b, 8, D_RNN), F32),
                   jax.ShapeDtypeStruct((b, 1, D_RNN), F32)],
        scratch_shapes=[pltpu.VMEM((8 + RNN_TC, D_RNN), F32), pltpu.VMEM((1, D_RNN), F32)],
        compiler_params=pltpu.CompilerParams(dimension_semantics=("arbitrary", "arbitrary"), vmem_limit_bytes=VMEM_LIMIT),
        name="prompt_rnn",
    )(xr, zr, *rw)


def _sample_rnn_body(xr_ref, zr_ref, sc_ref, h0_ref, cw_ref, cb_ref, wra_ref, bra_ref, wrx_ref, brx_ref, lam_ref,
                     rb_ref, conv_ref, h_ref):
    x = xr_ref[...]
    xc = cb_ref[...]
    for k in range(CONV_W - 1):
        xc = xc + sc_ref[k] * cw_ref[k:k + 1, :]
    xc = xc + x * cw_ref[CONV_W - 1:CONV_W, :]
    for k in range(CONV_W - 2):
        conv_ref[k] = sc_ref[k + 1]
    conv_ref[CONV_W - 2] = x
    a, u = _rglru_coeffs(xc, wra_ref, bra_ref, wrx_ref, brx_ref, lam_ref)
    h = a * h0_ref[...] + u
    h_ref[...] = h
    rb_ref[...] = (h * zr_ref[...]).astype(BF16)


def _sample_rnn(xr, zr, sc, h0, rw):
    n = xr.shape[0]
    return pl.pallas_call(
        _sample_rnn_body,
        out_shape=[jax.ShapeDtypeStruct((n, D_RNN), BF16), jax.ShapeDtypeStruct((CONV_W - 1, n, D_RNN), F32),
                   jax.ShapeDtypeStruct((n, D_RNN), F32)],
        compiler_params=pltpu.CompilerParams(vmem_limit_bytes=VMEM_LIMIT),
        name="sample_rnn",
    )(xr, zr, sc, h0, *rw)


def _outproj_body(x_ref, *refs, n_branch):
    o_refs = refs[:n_branch]
    zn_ref, rb_ref, gm_ref, wpa_ref, wpb_ref, wo_ref, y_ref = refs[n_branch:]
    pa = None
    for g in range(GROUP):
        o = o_refs[0][0, g]
        for r in o_refs[1:]:
            o = o + r[0, g]
        a = (o * zn_ref[:, g * LANES:(g + 1) * LANES]).astype(BF16)
        term = _dot(a, wpa_ref[g])
        pa = term if pa is None else pa + term
    pb = _dot(rb_ref[...], wpb_ref[...])
    merged = gm_ref[:, :D_MODEL] * pa + gm_ref[:, D_MODEL:] * pb
    y_ref[...] = x_ref[...] + _dot(merged.astype(BF16), wo_ref[...])


def _outproj(x2d, os_, zn, rb, gm, wpa, wpb, wo, tm):
    b, _, t, _ = os_[0].shape
    nt = t // tm
    row = lambda n: pl.BlockSpec((tm, n), lambda i: (i, 0))
    oblk = pl.BlockSpec((1, GROUP, tm, LANES), lambda i: (i // nt, 0, i % nt, 0))
    const = lambda a: pl.BlockSpec(a.shape, lambda i: (0,) * a.ndim)
    return pl.pallas_call(
        functools.partial(_outproj_body, n_branch=len(os_)),
        grid=(b * nt,),
        in_specs=[row(D_MODEL)] + [oblk] * len(os_) + [row(NSA_WIDTH), row(D_RNN), row(2 * D_MODEL), const(wpa), const(wpb), const(wo)],
        out_specs=row(D_MODEL),
        out_shape=jax.ShapeDtypeStruct(x2d.shape, F32),
        compiler_params=pltpu.CompilerParams(dimension_semantics=("arbitrary",), vmem_limit_bytes=VMEM_LIMIT),
        name="outproj",
    )(x2d, *os_, zn, rb, gm, wpa, wpb, wo)


def _sample_cmp_body(pt_ref, cache_ref, qbd_ref, gate_ref, wbdk_ref, w2k_ref, pek_ref, w1tk_ref, wbdv_ref, w2v_ref, pev_ref,
                     w1tv_ref, gkc_ref, cos_ref, s1_ref, s2_ref, seg_ref, mimp_ref, perm_ref,
                     oc_ref, idx_ref, buf, sem, rk, rv, pb_scr, *, n_pages, past, ns, n_top):
    s = pl.program_id(0)
    n_seq = pl.num_programs(0)
    slot = s % 2
    sub_per_page = PAGE_SIZE // CMP_STRIDE
    n_sub = n_pages * sub_per_page
    nsp = mimp_ref.shape[1]

    def page_copy(sl, seq, p):
        return pltpu.make_async_copy(cache_ref.at[pt_ref[seq * n_pages + p]], buf.at[sl, p], sem.at[sl])

    def start_all(sl, seq):
        lax.fori_loop(0, n_pages, lambda p, c: (page_copy(sl, seq, p).start(), c)[1], 0)

    @pl.when(s == 0)
    def _():
        start_all(0, 0)
        pb_scr[0:1, :] = _pos_bias(pek_ref, w1tk_ref)
        pb_scr[1:2, :] = _pos_bias(pev_ref, w1tv_ref)

    @pl.when(s + 1 < n_seq)
    def _():
        start_all(1 - slot, s + 1)

    lax.fori_loop(0, n_pages, lambda p, c: (page_copy(slot, s, p).wait(), c)[1], 0)

    def regroup(pp, c):
        ya = _dot_nt(perm_ref[...], buf[slot, 2 * pp].astype(BF16))
        yb = _dot_nt(perm_ref[...], buf[slot, 2 * pp + 1].astype(BF16))
        r0 = pl.multiple_of(pp * (2 * sub_per_page), 2 * sub_per_page)
        for j in range(CMP_STRIDE):
            rows = slice(j * sub_per_page, (j + 1) * sub_per_page)
            y = jnp.concatenate([ya[rows], yb[rows]], axis=0).astype(BF16)
            rk[pl.ds(r0, 2 * sub_per_page), j * LANES:(j + 1) * LANES] = y[:, :LANES]
            rv[pl.ds(r0, 2 * sub_per_page), j * LANES:(j + 1) * LANES] = y[:, LANES:]
        return c
    lax.fori_loop(0, n_pages // 2, regroup, 0, unroll=4)

    acc_k = _dot(rk[...], wbdk_ref[...])
    acc_v = _dot(rv[...], wbdv_ref[...])
    kc = _compress_finish(acc_k, n_sub, w2k_ref, pb_scr[0:1, :])
    kc = _headnorm_rope(kc, gkc_ref[...], cos_ref[...], s1_ref[...], s2_ref[...], seg_ref[...]).astype(BF16)
    vc = _compress_finish(acc_v, n_sub, w2v_ref, pb_scr[1:2, :]).astype(BF16)

    nr = N_Q_HEADS
    c_end = lax.broadcasted_iota(jnp.int32, (1, n_sub), 1) * CMP_STRIDE + (CMP_BLOCK - 1)
    p = _masked_softmax(_dot_nt(qbd_ref[0].astype(BF16), kc), jnp.broadcast_to(c_end <= past, (nr, n_sub)))
    o = _dot(p.astype(BF16), vc)
    oc_ref[0] = _own_half(o, gate_ref[0])

    psum = jnp.concatenate([jnp.sum(p[h * GROUP:(h + 1) * GROUP], axis=0, keepdims=True) for h in range(N_KV_HEADS)]
                           + [jnp.zeros((nr - N_KV_HEADS, n_sub), F32)], axis=0)
    jl = lax.broadcasted_iota(jnp.int32, (1, nsp), 1)
    cur = past // SEL_BLOCK
    forced = (jl == 0) | (jl == cur) | (jl == cur - 1)
    v = _dot_exact01(psum, mimp_ref[...]) + jnp.where(forced, FORCE_BONUS, 0.0)
    v = jnp.where(jl < ns, v, -1.0)
    vt = v.T
    js = lax.broadcasted_iota(jnp.int32, (nsp, 1), 0)
    rl = lax.broadcasted_iota(jnp.int32, (1, LANES), 1)
    idx_rows = []
    for h in range(N_KV_HEADS):
        vrow, vcol = v[h:h + 1, :], vt[:, h:h + 1]
        before = (vrow > vcol) | ((vrow == vcol) & (jl < js))
        rank = jnp.sum(jnp.where(before, 1.0, 0.0), axis=1, keepdims=True)
        hit = rank == rl.astype(F32)
        idx_rows.append(jnp.sum(jnp.where(hit, js, 0), axis=0, keepdims=True))
    idx_ref[0] = jnp.concatenate(idx_rows + [jnp.zeros((nr - N_KV_HEADS, LANES), jnp.int32)], axis=0)


def _sample_cmp(pt_flat, cache_t, qbd, gate_c, cw, gkc, ctabs, seg, mimp, n_seq, n_pages, past, ns, n_top):
    sub_per_page = PAGE_SIZE // CMP_STRIDE
    n_sub = n_pages * sub_per_page
    const = lambda a: pl.BlockSpec(a.shape, lambda s, pt: (0,) * a.ndim)
    seqblk = pl.BlockSpec((1, N_Q_HEADS, LANES), lambda s, pt: (s, 0, 0))
    tokens = np.arange(PAGE_SIZE)
    dst_row = (tokens % CMP_STRIDE) * sub_per_page + tokens // CMP_STRIDE
    perm = jnp.asarray(np.arange(PAGE_SIZE)[:, None] == dst_row[None, :], BF16)
    flat = lambda w: w.reshape(CMP_STRIDE * LANES, 2 * LANES)
    cw = (flat(cw[0]), *cw[1:4], flat(cw[4]), *cw[5:])
    consts = [*cw, gkc, *ctabs, seg, mimp, perm]
    return pl.pallas_call(
        functools.partial(_sample_cmp_body, n_pages=n_pages, past=past, ns=ns, n_top=n_top),
        grid_spec=pltpu.PrefetchScalarGridSpec(
            num_scalar_prefetch=1,
            grid=(n_seq,),
            in_specs=[pl.BlockSpec(memory_space=pl.ANY), seqblk, seqblk] + [const(a) for a in consts],
            out_specs=[seqblk, seqblk],
            scratch_shapes=[pltpu.VMEM((2, n_pages, KV_COLS, PAGE_SIZE), F32), pltpu.SemaphoreType.DMA((2,)),
                            pltpu.VMEM((n_sub, CMP_STRIDE * LANES), BF16), pltpu.VMEM((n_sub, CMP_STRIDE * LANES), BF16),
                            pltpu.VMEM((8, LANES), F32)],
        ),
        out_shape=[jax.ShapeDtypeStruct((n_seq, N_Q_HEADS, LANES), F32), jax.ShapeDtypeStruct((n_seq, N_Q_HEADS, LANES), jnp.int32)],
        compiler_params=pltpu.CompilerParams(dimension_semantics=("arbitrary",), vmem_limit_bytes=VMEM_LIMIT),
        name="sample_cmp",
    )(pt_flat, cache_t, qbd, gate_c, *consts)


def _decode_attend(qbd, k_t, v_t, bias, new_row):
    s = _dot(qbd.astype(BF16), k_t)
    if bias is not None:
        s = s + bias
    s_new = jnp.sum(qbd * new_row[:, :LANES], axis=1, keepdims=True)
    m = jnp.maximum(jnp.max(s, axis=1, keepdims=True), s_new)
    e = jnp.exp(s - m)
    e_new = jnp.exp(s_new - m)
    den = jnp.sum(e, axis=1, keepdims=True) + e_new
    num = _dot_nt(e.astype(BF16), v_t) + e_new * new_row[:, LANES:]
    return num / den


def _own_half(o, gate):
    lane = lax.broadcasted_iota(jnp.int32, o.shape, 1)
    rowh = lax.broadcasted_iota(jnp.int32, o.shape, 0) // GROUP
    return jnp.where(lane // HEAD_DIM == rowh, o * _sigmoid(gate), 0.0)


def _sample_sel_body(pg_ref, meta_ref, cache_ref, qbd_ref, kvrow_ref, gate_ref, o_ref, buf, sem, *, n_top):
    s = pl.program_id(0)
    n_seq = pl.num_programs(0)
    slot = s % 2
    n_slots = N_KV_HEADS * n_top

    def page_copy(sl, seq, i):
        return pltpu.make_async_copy(cache_ref.at[pg_ref[seq * n_slots + i]], buf.at[sl, i], sem.at[sl])

    def start_all(sl, seq):
        lax.fori_loop(0, n_slots, lambda i, c: (page_copy(sl, seq, i).start(), c)[1], 0)

    @pl.when(s == 0)
    def _():
        start_all(0, 0)

    @pl.when(s + 1 < n_seq)
    def _():
        start_all(1 - slot, s + 1)

    lax.fori_loop(0, n_slots, lambda i, c: (page_copy(slot, s, i).wait(), c)[1], 0)

    qbd = qbd_ref[0]
    new_row = kvrow_ref[0]
    lane = lax.broadcasted_iota(jnp.int32, (1, PAGE_SIZE), 1)
    outs = []
    for h in range(N_KV_HEADS):
        bias, k_parts, v_parts = [], [], []
        for r in range(n_top):
            lo = meta_ref[s * n_slots + h * n_top + r] * SEL_BLOCK
            bias.append(jnp.where((lane >= lo) & (lane < lo + SEL_BLOCK), 0.0, -jnp.inf))
            k_parts.append(buf[slot, h * n_top + r, :LANES, :].astype(BF16))
            v_parts.append(buf[slot, h * n_top + r, LANES:, :].astype(BF16))
        outs.append(_decode_attend(qbd, jnp.concatenate(k_parts, axis=1), jnp.concatenate(v_parts, axis=1),
                                   jnp.concatenate(bias, axis=1), new_row))
    rowh = lax.broadcasted_iota(jnp.int32, outs[0].shape, 0) // GROUP
    o_ref[0] = _own_half(jnp.where(rowh == 0, outs[0], outs[1]), gate_ref[0])


def _sample_sel(pg, meta, cache_t, qbd, kvrow, gate, n_seq, n_top):
    seqblk = pl.BlockSpec((1, N_Q_HEADS, LANES), lambda s, pg, meta: (s, 0, 0))
    return pl.pallas_call(
        functools.partial(_sample_sel_body, n_top=n_top),
        grid_spec=pltpu.PrefetchScalarGridSpec(
            num_scalar_prefetch=2,
            grid=(n_seq,),
            in_specs=[pl.BlockSpec(memory_space=pl.ANY), seqblk, pl.BlockSpec((1, 1, KV_COLS), lambda s, pg, meta: (s, 0, 0)), seqblk],
            out_specs=seqblk,
            scratch_shapes=[pltpu.VMEM((2, N_KV_HEADS * n_top, KV_COLS, PAGE_SIZE), F32), pltpu.SemaphoreType.DMA((2,))],
        ),
        out_shape=jax.ShapeDtypeStruct((n_seq, N_Q_HEADS, LANES), F32),
        compiler_params=pltpu.CompilerParams(dimension_semantics=("arbitrary",), vmem_limit_bytes=VMEM_LIMIT),
        name="sample_sel",
    )(pg, meta, cache_t, qbd, kvrow, gate)


WIN_SEQS = 4


def _sample_win_body(win_ref, qbd_ref, kvrow_ref, kvcol_ref, gate_ref, o_ref, wout_ref):
    lane = lax.broadcasted_iota(jnp.int32, (1, WINDOW), 1)
    for i in range(win_ref.shape[0]):
        w = win_ref[i]
        wout_ref[i] = jnp.where(lane == WINDOW - 1, kvcol_ref[i], pltpu.roll(w, WINDOW - 1, 1))
        o = _decode_attend(qbd_ref[i], w[:LANES, :].astype(BF16), w[LANES:, :].astype(BF16), None, kvrow_ref[i])
        o_ref[i] = _own_half(o, gate_ref[i])


def _sample_win(win_t, qbd, kvrow, kvcol, gate):
    n_seq = win_t.shape[0]
    sps = WIN_SEQS if n_seq % WIN_SEQS == 0 else 1
    blk = lambda *dims: pl.BlockSpec((sps, *dims), lambda s: (s, 0, 0))
    return pl.pallas_call(
        _sample_win_body,
        grid=(n_seq // sps,),
        in_specs=[blk(KV_COLS, WINDOW), blk(N_Q_HEADS, LANES), blk(1, KV_COLS), blk(KV_COLS, 1), blk(N_Q_HEADS, LANES)],
        out_specs=[blk(N_Q_HEADS, LANES), blk(KV_COLS, WINDOW)],
        out_shape=[jax.ShapeDtypeStruct((n_seq, N_Q_HEADS, LANES), F32), jax.ShapeDtypeStruct(win_t.shape, F32)],
        compiler_params=pltpu.CompilerParams(dimension_semantics=("arbitrary",), vmem_limit_bytes=VMEM_LIMIT),
        name="sample_win",
    )(win_t, qbd, kvrow, kvcol, gate)


def _prep_inproj_weight(w_in):
    sizes = (NSA_WIDTH, 3 * KV_COLS, 3 * N_Q_HEADS, NSA_WIDTH, D_RNN, D_RNN, 2 * D_MODEL)
    offs = np.concatenate([[0], np.cumsum(sizes)])
    q, kv, gn, zn, xr, zr, gm = [w_in[:, offs[i]:offs[i + 1]] for i in range(len(sizes))]
    zn = zn.reshape(D_MODEL, N_KV_HEADS, GROUP, HEAD_DIM).transpose(0, 2, 1, 3).reshape(D_MODEL, NSA_WIDTH)
    pad = jnp.zeros((D_MODEL, LANES - 3 * N_Q_HEADS), w_in.dtype)
    return jnp.concatenate([q, kv, zn, xr, zr, gm, gn, pad], axis=1).astype(BF16)


def _prep_compress(pe, w1, w2):
    eye = jnp.eye(N_KV_HEADS, dtype=F32)
    bd = lambda w: jnp.einsum('jde,hk->jhdke', w, eye).reshape(CMP_STRIDE, LANES, LANES)
    wbd = jnp.concatenate([bd(w1[:CMP_STRIDE]), bd(w1[CMP_STRIDE:])], axis=2).astype(BF16)
    w2bd = jnp.einsum('ed,hk->hekd', w2, eye).reshape(LANES, LANES).astype(BF16)
    pe_col = pe.reshape(CMP_BLOCK * HEAD_DIM, 1)
    w1t = jnp.tile(w1.reshape(CMP_BLOCK * HEAD_DIM, CMP_HIDDEN), (1, N_KV_HEADS))
    return wbd, w2bd, pe_col, w1t


def _prep_rnn(conv_w, conv_b, w_ra, b_ra, w_rx, b_rx, lam):
    def pairs(w):
        w = w.reshape(RNN_HEADS // 2, 2, RNN_HD, RNN_HD)
        eye = jnp.eye(2, dtype=F32)
        return jnp.einsum('shij,hk->shikj', w, eye).reshape(RNN_HEADS // 2, LANES, LANES).astype(BF16)
    return (conv_w, conv_b.reshape(1, D_RNN), pairs(w_ra), b_ra.reshape(1, D_RNN), pairs(w_rx), b_rx.reshape(1, D_RNN),
            lam.reshape(1, D_RNN))


def _feature_major(cache):
    n, r = cache.shape[:2]
    return cache.transpose(0, 2, 3, 4, 1).reshape(n, KV_COLS, r)


def kernel(x_prompt, x_sample, cache_kv_cmp, cache_kv_sel, cache_kv_win, state_conv, state_h, page_table, g_norm, w_in, g_q, g_kc, g_ks, g_kw, pe_k, w1_k, w2_k, pe_v, w1_v, w2_v, conv_w, conv_b, w_ra, b_ra, w_rx, b_rx, lam, w_pa, w_pb, w_out):
    b, t, _ = x_prompt.shape
    n_seq = x_sample.shape[0]
    n_pages = page_table.shape[1]
    past = n_pages * PAGE_SIZE
    assert x_sample.shape[1] == 1 and cache_kv_win.shape[1] == WINDOW and past >= WINDOW
    assert t % SEL_TK == 0 and t // SEL_BLOCK <= HEAD_DIM and t >= WIN_KEYS and n_pages % 2 == 0

    wp = _prep_inproj_weight(w_in)
    seg = _seg01()
    tile2 = lambda g: jnp.tile(g.reshape(1, HEAD_DIM), (1, 2))
    gq, gks, gkw, gkc = tile2(g_q), tile2(g_ks), tile2(g_kw), tile2(g_kc)
    gnorm = g_norm.reshape(1, D_MODEL)
    cw = (*_prep_compress(pe_k, w1_k, w2_k), *_prep_compress(pe_v, w1_v, w2_v))
    rw = _prep_rnn(conv_w, conv_b, w_ra, b_ra, w_rx, b_rx, lam)
    wpa = w_pa.reshape(N_KV_HEADS, GROUP, HEAD_DIM, D_MODEL).transpose(1, 0, 2, 3).reshape(GROUP, LANES, D_MODEL).astype(BF16)
    wpb, wo = w_pb.astype(BF16), w_out.astype(BF16)

    tm = 256
    tok = np.arange(t)
    oh = jnp.asarray((tok[:, None] // SEL_BLOCK) == (np.arange(LANES)[None, :] % HEAD_DIM), BF16)
    xp2 = x_prompt.reshape(b * t, D_MODEL)
    (kvc, kvct, kvst, kvwt, qxt, ka, kw, vst, vwt, gatet, zn, xr, zr, gm) = _inproj(
        xp2, gnorm, wp, _rope_tables(tok), oh, gq, gks, gkw, seg, tm, t, True)
    n_sub = t // CMP_STRIDE
    ns = t // SEL_BLOCK
    mimp = jnp.asarray(_imp_matrix(n_sub, ns, n_sub - 1, ns).T, BF16)
    ctabs = _rope_tables(np.arange(n_sub) * CMP_STRIDE + (CMP_BLOCK - 1))
    o_attn = _prompt_attn(kvc, qxt, gatet, ka, vst, kw, vwt, cw, gkc, ctabs, seg, mimp, b, t)
    rb, conv_tail, h_last = _prompt_rnn(xr, zr, rw, b, t)
    y_prompt = _outproj(xp2, [o_attn], zn, rb, gm, wpa, wpb, wo, tm).reshape(b, t, D_MODEL)
    kv5t = lambda a: a.reshape(a.shape[0], 2, N_KV_HEADS, HEAD_DIM, a.shape[2]).transpose(0, 4, 1, 2, 3)
    kv_cmp_prompt, kv_sel_prompt = kv5t(kvct), kv5t(kvst)
    kv_win_prompt = kv5t(kvwt[:, :, t - min(WINDOW, t):])
    conv_prompt = conv_tail[:, 8 - (CONV_W - 1):]
    h_prompt = h_last.reshape(b, D_RNN)

    xs2 = x_sample.reshape(n_seq, D_MODEL)
    oh_s = jnp.zeros((n_seq, LANES), BF16)
    (kvc_s, kvs_s, kvw_s, qf_s, gate_s, zn_s, xr_s, zr_s, gm_s) = _inproj(
        xs2, gnorm, wp, _rope_tables(np.full((n_seq,), past)), oh_s, gq, gks, gkw, seg, n_seq, n_seq, False)
    kv5 = lambda a, rows: a.reshape(-1, rows, 2, N_KV_HEADS, HEAD_DIM)
    eye = jnp.eye(N_KV_HEADS, dtype=F32)
    q4 = qf_s.reshape(n_seq, N_KV_HEADS, GROUP, HEAD_DIM)
    qbd = (q4[:, :, :, None, :] * eye[None, :, None, :, None]).reshape(n_seq, N_Q_HEADS, LANES)
    gates3 = gate_s[:, :3 * N_Q_HEADS].reshape(n_seq, N_Q_HEADS, 3)
    gate_b = lambda k: jnp.broadcast_to(gates3[:, :, k:k + 1], (n_seq, N_Q_HEADS, LANES))

    n_sub_s = past // CMP_STRIDE
    ns_s = past // SEL_BLOCK + 1
    n_top_s = min(N_SEL, ns_s)
    nsp = -(-ns_s // LANES) * LANES
    mimp_s = jnp.asarray(_imp_matrix(n_sub_s, nsp, n_sub_s - 1, ns_s), BF16)
    ctabs_s = _rope_tables(np.arange(n_sub_s) * CMP_STRIDE + (CMP_BLOCK - 1))
    oc_s, idx = _sample_cmp(page_table.reshape(-1), _feature_major(cache_kv_cmp), qbd, gate_b(0), cw, gkc, ctabs_s, seg, mimp_s,
                            n_seq, n_pages, past, ns_s, n_top_s)
    blocks = idx[:, :N_KV_HEADS, :n_top_s]
    per_page = PAGE_SIZE // SEL_BLOCK
    in_cache = blocks < past // SEL_BLOCK
    pg = jnp.take_along_axis(page_table, jnp.minimum(blocks // per_page, n_pages - 1).reshape(n_seq, -1), axis=1)
    meta = jnp.where(in_cache, blocks % per_page, per_page).reshape(-1).astype(jnp.int32)
    o_sel_s = _sample_sel(pg.reshape(-1).astype(jnp.int32), meta, _feature_major(cache_kv_sel), qbd,
                          kvs_s.reshape(n_seq, 1, KV_COLS), gate_b(1), n_seq, n_top_s)
    o_win_s, win_new = _sample_win(_feature_major(cache_kv_win), qbd, kvw_s.reshape(n_seq, 1, KV_COLS),
                                   kvw_s.reshape(n_seq, KV_COLS, 1), gate_b(2))
    sc_t = state_conv.transpose(1, 0, 2)
    rb_s, conv_new, h_sample = _sample_rnn(xr_s, zr_s, sc_t, state_h, rw)
    to_o = lambda o: (o[:, :GROUP] + o[:, GROUP:]).transpose(1, 0, 2)[None]
    y_sample = _outproj(xs2, [to_o(oc_s), to_o(o_sel_s), to_o(o_win_s)], zn_s, rb_s, gm_s, wpa, wpb, wo, n_seq).reshape(n_seq, 1, D_MODEL)
    kv_cmp_sample, kv_sel_sample = kv5(kvc_s, 1), kv5(kvs_s, 1)
    kv_win_sample = win_new.reshape(n_seq, 2, N_KV_HEADS, HEAD_DIM, WINDOW).transpose(0, 4, 1, 2, 3)
    conv_sample = conv_new.transpose(1, 0, 2)

    return (y_prompt, y_sample, kv_cmp_prompt, kv_cmp_sample, kv_sel_prompt, kv_sel_sample,
            kv_win_prompt, kv_win_sample, conv_prompt, conv_sample, h_prompt, h_sample)
```

```python
import functools

import numpy as np
import jax
import jax.numpy as jnp
from jax import lax
from jax.experimental import pallas as pl
from jax.experimental.pallas import tpu as pltpu

D_MODEL = 1024
HEAD_DIM = 64
N_Q_HEADS = 8
N_KV_HEADS = 2
GROUP = N_Q_HEADS // N_KV_HEADS
NSA_WIDTH = N_Q_HEADS * HEAD_DIM
SCALE = HEAD_DIM ** -0.5
ROPE_DIM = HEAD_DIM // 4
ROPE_HALF = ROPE_DIM // 2
ROPE_THETA = 500000.0
CMP_BLOCK = 32
CMP_STRIDE = 16
CMP_HIDDEN = 64
SEL_BLOCK = 64
N_SEL = 16
WINDOW = 512
Q_BLOCK = 256
FORCE_BONUS = 1.0e4
D_RNN = D_MODEL // 2
RNN_HEADS = 8
RNN_HD = D_RNN // RNN_HEADS
CONV_W = 4
RG_C = 8.0
EPS = 1e-6
PAGE_SIZE = 128

LANES = 128
KV_COLS = 2 * N_KV_HEADS * HEAD_DIM
NEG = -(2.0 ** 100)
LOG2E = 1.4426950408889634
V_ROWS = LANES + 16
VMEM_LIMIT = 56 * 1024 * 1024
F32 = jnp.float32
BF16 = jnp.bfloat16

C_Q = 0
C_KV = C_Q + NSA_WIDTH
C_ZN = C_KV + 3 * KV_COLS
C_XR = C_ZN + NSA_WIDTH
C_ZR = C_XR + D_RNN
C_GM = C_ZR + D_RNN
C_GN = C_GM + 2 * D_MODEL
N_WCOLS = C_GN + LANES


def _dot(a, b):
    return jnp.dot(a, b, preferred_element_type=F32)


def _dot_nt(a, b):
    return lax.dot_general(a, b, (((1,), (1,)), ((), ())), preferred_element_type=F32)


def _split3(a):
    a1 = a.astype(BF16)
    r1 = a - a1.astype(F32)
    a2 = r1.astype(BF16)
    a3 = (r1 - a2.astype(F32)).astype(BF16)
    return a1, a2, a3


def _dot_exact01(a, b01):
    a1, a2, a3 = _split3(a)
    return _dot(a1, b01) + _dot(a2, b01) + _dot(a3, b01)


def _dot_01_exact(b01, a):
    a1, a2, a3 = _split3(a)
    return _dot(b01, a1) + _dot(b01, a2) + _dot(b01, a3)


def _sigmoid(x):
    return 1.0 / (1.0 + jnp.exp(-x))


def _silu(x):
    return x * _sigmoid(x)


def _headnorm_rope(xs, gain, cos, s1, s2, seg01):
    ss = _dot_exact01(xs * xs, seg01)
    y = xs * lax.rsqrt(ss * (1.0 / HEAD_DIM) + EPS) * gain
    return y * cos + pltpu.roll(y, ROPE_HALF, 1) * s1 + pltpu.roll(y, LANES - ROPE_HALF, 1) * s2


def _masked_softmax(s, mask):
    s = jnp.where(mask, s, -jnp.inf)
    m = jnp.max(s, axis=-1, keepdims=True)
    m = jnp.where(m == -jnp.inf, 0.0, m)
    e = jnp.where(mask, jnp.exp(s - m), 0.0)
    d = jnp.sum(e, axis=-1, keepdims=True)
    return e / jnp.where(d > 0, d, 1.0)


def _masked_softmax_t(s, mask):
    s = jnp.where(mask, s, -jnp.inf)
    m = jnp.max(s, axis=0, keepdims=True)
    m = jnp.where(m == -jnp.inf, 0.0, m)
    e = jnp.where(mask, jnp.exp2(s - m), 0.0)
    d = jnp.sum(e, axis=0, keepdims=True)
    return e * (1.0 / jnp.where(d > 0, d, 1.0))


def _rope_tables(pos):
    pos = np.asarray(pos, np.float64)
    inv = ROPE_THETA ** (-np.arange(ROPE_HALF, dtype=np.float64) / ROPE_HALF)
    ang = (pos.astype(np.float32)[:, None] * inv.astype(np.float32)[None, :]).astype(np.float32).astype(np.float64)
    cos, sin = np.cos(ang), np.sin(ang)
    n = pos.shape[0]
    c = np.ones((n, HEAD_DIM)); s1 = np.zeros((n, HEAD_DIM)); s2 = np.zeros((n, HEAD_DIM))
    c[:, :ROPE_HALF] = cos; c[:, ROPE_HALF:ROPE_DIM] = cos
    s1[:, ROPE_HALF:ROPE_DIM] = sin
    s2[:, :ROPE_HALF] = -sin
    t = lambda a: jnp.asarray(np.tile(a, (1, 2)), F32)
    return t(c), t(s1), t(s2)


def _seg01():
    lane = np.arange(LANES)
    return jnp.asarray((lane[:, None] // HEAD_DIM) == (lane[None, :] // HEAD_DIM), BF16)


def _imp_matrix(nc_pad, ns_pad, nc, ns):
    r = SEL_BLOCK // CMP_STRIDE
    lead = CMP_BLOCK // CMP_STRIDE - 1
    c = np.arange(nc_pad)[:, None]
    j = np.arange(ns_pad)[None, :]
    m = (c >= r * j - lead) & (c <= r * j + r - 1) & (c < nc) & (j < ns)
    return m


def _values_t(v, ref):
    ref[:LANES, :] = v.T.astype(BF16)
    ones_row = lax.broadcasted_iota(jnp.int32, (V_ROWS - LANES, v.shape[0]), 0) == 0
    ref[LANES:, :] = jnp.where(ones_row, 1.0, 0.0).astype(BF16)


def _inproj_body(x_ref, gn_ref, w_ref, cos_ref, s1_ref, s2_ref, oh_ref, gq_ref, gks_ref, gkw_ref, seg_ref, *out_refs, prompt):
    if prompt:
        (kvc_ref, kvct_ref, kvst_ref, kvwt_ref, qxt_ref, ka_ref, kw_ref, vst_ref, vwt_ref, gatet_ref,
         zn_ref, xr_ref, zr_ref, gm_ref) = out_refs
    else:
        kvc_ref, kvs_ref, kvw_ref, qf_ref, gate_ref, zn_ref, xr_ref, zr_ref, gm_ref = out_refs
    x = x_ref[...]
    r = lax.rsqrt(jnp.mean(x * x, axis=-1, keepdims=True) + EPS)
    u = (x * r * gn_ref[...]).astype(BF16)
    cos, s1, s2, seg = cos_ref[...], s1_ref[...], s2_ref[...], seg_ref[...]
    lane = lax.broadcasted_iota(jnp.int32, (1, LANES), 1)

    def proj(c0, n):
        return _dot(u, w_ref[:, c0:c0 + n])

    q = proj(C_Q, NSA_WIDTH)
    for s in range(NSA_WIDTH // LANES):
        qs = _headnorm_rope(q[:, s * LANES:(s + 1) * LANES], gq_ref[...], cos, s1, s2, seg) * SCALE
        if not prompt:
            qf_ref[:, s * LANES:(s + 1) * LANES] = qs
            continue
        qs = qs * LOG2E
        qs_sw = pltpu.roll(qs, HEAD_DIM, 1)
        for half in range(2):
            i = 2 * s + half
            h = i // GROUP
            src = qs if half == h else qs_sw
            keep = (lane >= h * HEAD_DIM) & (lane < (h + 1) * HEAD_DIM)
            qxt_ref[i * LANES:(i + 1) * LANES, :] = jnp.where(keep, src, 0.0).T.astype(BF16)

    kv = proj(C_KV, 3 * KV_COLS)
    kc, vc = kv[:, :LANES], kv[:, LANES:KV_COLS]
    ks = _headnorm_rope(kv[:, KV_COLS:KV_COLS + LANES], gks_ref[...], cos, s1, s2, seg)
    vs = kv[:, KV_COLS + LANES:2 * KV_COLS]
    kw = _headnorm_rope(kv[:, 2 * KV_COLS:2 * KV_COLS + LANES], gkw_ref[...], cos, s1, s2, seg)
    vw = kv[:, 2 * KV_COLS + LANES:]
    kvc_ref[...] = kv[:, :KV_COLS]
    if prompt:
        for ref, k, v in ((kvct_ref, kc, vc), (kvst_ref, ks, vs), (kvwt_ref, kw, vw)):
            ref[0, :LANES, :] = k.T
            ref[0, LANES:, :] = v.T
        ka_ref[:, :LANES] = ks.astype(BF16)
        ka_ref[:, LANES:] = oh_ref[...]
        kw_ref[...] = kw.astype(BF16)
        _values_t(vs, vst_ref)
        _values_t(vw, vwt_ref)
        gatet_ref[...] = proj(C_GN, LANES).T
    else:
        kvs_ref[:, :LANES] = ks
        kvs_ref[:, LANES:] = vs
        kvw_ref[:, :LANES] = kw
        kvw_ref[:, LANES:] = vw
        gate_ref[...] = proj(C_GN, LANES)

    zn_ref[...] = _silu(proj(C_ZN, NSA_WIDTH)).astype(BF16)
    xr_ref[...] = proj(C_XR, D_RNN)
    zr_ref[...] = _silu(proj(C_ZR, D_RNN)).astype(BF16)
    for c in range(2):
        gm_ref[:, c * D_MODEL:(c + 1) * D_MODEL] = _sigmoid(proj(C_GM + c * D_MODEL, D_MODEL)).astype(BF16)


def _inproj(x2d, gnorm, wp, tabs, oh, gq, gks, gkw, seg, tm, t, prompt):
    rows = x2d.shape[0]
    nt = t // tm
    row = lambda n: pl.BlockSpec((tm, n), lambda i: (i, 0))
    col = lambda n: pl.BlockSpec((n, tm), lambda i: (0, i))
    tab = pl.BlockSpec((tm, LANES), lambda i: (i % nt, 0))
    const = lambda a: pl.BlockSpec(a.shape, lambda i: (0,) * a.ndim)
    rowshape = lambda n, dt: jax.ShapeDtypeStruct((rows, n), dt)
    colshape = lambda n, dt: jax.ShapeDtypeStruct((n, rows), dt)
    tail = [(row(NSA_WIDTH), rowshape(NSA_WIDTH, BF16)), (row(D_RNN), rowshape(D_RNN, F32)), (row(D_RNN), rowshape(D_RNN, BF16)),
            (row(2 * D_MODEL), rowshape(2 * D_MODEL, BF16))]
    if prompt:
        leaf_t = (pl.BlockSpec((1, KV_COLS, tm), lambda i: (i // nt, 0, i % nt)), jax.ShapeDtypeStruct((rows // t, KV_COLS, t), F32))
        outs = [(row(KV_COLS), rowshape(KV_COLS, F32)), leaf_t, leaf_t, leaf_t,
                (col(N_Q_HEADS * LANES), colshape(N_Q_HEADS * LANES, BF16)),
                (row(2 * LANES), rowshape(2 * LANES, BF16)), (row(LANES), rowshape(LANES, BF16)),
                (col(V_ROWS), colshape(V_ROWS, BF16)), (col(V_ROWS), colshape(V_ROWS, BF16)),
                (col(LANES), colshape(LANES, F32))] + tail
    else:
        outs = [(row(KV_COLS), rowshape(KV_COLS, F32))] * 3 + [(row(NSA_WIDTH), rowshape(NSA_WIDTH, F32)),
                                                               (row(LANES), rowshape(LANES, F32))] + tail
    return pl.pallas_call(
        functools.partial(_inproj_body, prompt=prompt),
        grid=(rows // tm,),
        in_specs=[row(D_MODEL), const(gnorm), const(wp), tab, tab, tab, tab, const(gq), const(gks), const(gkw), const(seg)],
        out_specs=[o[0] for o in outs],
        out_shape=[o[1] for o in outs],
        compiler_params=pltpu.CompilerParams(dimension_semantics=("arbitrary",), vmem_limit_bytes=VMEM_LIMIT),
        name="inproj",
    )(x2d, gnorm, wp, *tabs, oh, gq, gks, gkw, seg)


def _compress_rows(load_rows, n_sub, wbd_ref, w2_ref, pe_ref, w1t_ref):
    acc = jnp.zeros((n_sub, 2 * LANES), F32)
    for j in range(CMP_STRIDE):
        acc = acc + _dot(load_rows(j).astype(BF16), wbd_ref[j])
    return _compress_finish(acc, n_sub, w2_ref, _pos_bias(pe_ref, w1t_ref))


def _pos_bias(pe_ref, w1t_ref):
    return jnp.sum(pe_ref[...] * w1t_ref[...], axis=0, keepdims=True)


def _compress_finish(acc, n_sub, w2_ref, pos_bias):
    lo, hi = acc[:, :LANES], acc[:, LANES:]
    hid = _silu(lo + pltpu.roll(hi, n_sub - 1, 0) + pos_bias)
    return _dot(hid.astype(BF16), w2_ref[...])


SEL_TK = 512
SEL_TC = 256
N_BACK = WINDOW // Q_BLOCK
WIN_KEYS = (N_BACK + 1) * Q_BLOCK


def _prompt_attn_body(kraw_ref, vraw_ref, qxt_ref, gatet_ref, ka_ref, vst_ref, kw_ref, vwt_ref,
                      wbdk_ref, w2k_ref, pek_ref, w1tk_ref, wbdv_ref, w2v_ref, pev_ref, w1tv_ref,
                      gkc_ref, cos_ref, s1_ref, s2_ref, seg_ref, mimp_ref,
                      o_ref, kc_scr, vct_scr, qat_scr, m_scr, acc_scr, sa_scr, sb_scr, osum_scr, ow_scr, *, n_sub, ns, n_top):
    qi = pl.program_id(1)
    ncol = N_Q_HEADS * Q_BLOCK

    @pl.when(qi == 0)
    def _():
        def rows(ref):
            return lambda j: ref[0, pl.ds(j, n_sub, stride=CMP_STRIDE), :]
        kc = _compress_rows(rows(kraw_ref), n_sub, wbdk_ref, w2k_ref, pek_ref, w1tk_ref)
        kc = _headnorm_rope(kc, gkc_ref[...], cos_ref[...], s1_ref[...], s2_ref[...], seg_ref[...])
        kc_scr[...] = kc.astype(BF16)
        vct_scr[...] = _compress_rows(rows(vraw_ref), n_sub, wbdv_ref, w2v_ref, pev_ref, w1tv_ref).T.astype(BF16)

    for i in range(N_Q_HEADS):
        qat_scr[:LANES, i * Q_BLOCK:(i + 1) * Q_BLOCK] = qxt_ref[i * LANES:(i + 1) * LANES, :]
    qt = qat_scr[:LANES, :]
    qpos = qi * Q_BLOCK + lax.broadcasted_iota(jnp.int32, (1, ncol), 1) % Q_BLOCK
    gates = _sigmoid(gatet_ref[...])

    def gate_row(k):
        return jnp.concatenate([gates[i * 3 + k:i * 3 + k + 1, :] for i in range(N_Q_HEADS)], axis=1)

    w0 = pl.multiple_of(jnp.maximum(qi - N_BACK, 0) * Q_BLOCK, Q_BLOCK)
    sw = _dot(kw_ref[pl.ds(w0, WIN_KEYS), :], qt)
    r = lax.broadcasted_iota(jnp.int32, (Q_BLOCK, 1), 0)
    c = lax.broadcasted_iota(jnp.int32, (1, Q_BLOCK), 1)
    newest = jnp.where(r <= c, 0.0, -jnp.inf)
    oldest = jnp.where(c <= r, 0.0, -jnp.inf)
    biased = []
    for blk in range(N_BACK + 1):
        back = qi - (w0 // Q_BLOCK + blk)
        inner = jnp.where((back > 0) & (back < N_BACK), 0.0, -jnp.inf)
        bias = jnp.where(back == 0, newest, jnp.where(back == N_BACK, oldest, inner))
        biased.append(sw[blk * Q_BLOCK:(blk + 1) * Q_BLOCK] + jnp.concatenate([bias] * N_Q_HEADS, axis=1))
    sw = jnp.concatenate(biased, axis=0)
    e = jnp.exp2(sw - jnp.max(sw, axis=0, keepdims=True))
    ow = _dot(vwt_ref[:, pl.ds(w0, WIN_KEYS)], e.astype(BF16))
    ow_scr[...] = ow[:LANES, :] * (gate_row(2) / ow[LANES:LANES + 1, :])

    c_end = lax.broadcasted_iota(jnp.int32, (n_sub, 1), 0) * CMP_STRIDE + (CMP_BLOCK - 1)
    p = _masked_softmax_t(_dot(kc_scr[...], qt), c_end <= qpos)
    osum_scr[...] = _dot(vct_scr[...], p.astype(BF16)) * gate_row(0)

    jrow = lax.broadcasted_iota(jnp.int32, (ns, 1), 0)
    cur = (qi * Q_BLOCK + lax.broadcasted_iota(jnp.int32, (1, Q_BLOCK), 1)) // SEL_BLOCK
    forced = (jrow == 0) | (jrow == cur) | (jrow == cur - 1)
    zeros = jnp.zeros((HEAD_DIM, Q_BLOCK), F32)
    for h in range(N_KV_HEADS):
        pg = [p[:, (h * GROUP + g) * Q_BLOCK:(h * GROUP + g + 1) * Q_BLOCK] for g in range(GROUP)]
        psum = ((pg[0] + pg[1]) + pg[2]) + pg[3]
        v = _dot_01_exact(mimp_ref[...], psum) + jnp.where(forced, FORCE_BONUS, 0.0)
        groups = [v[r * 8:(r + 1) * 8, :] for r in range(ns // 8)]
        ranks = [jnp.zeros((8, Q_BLOCK), F32) for _ in groups]
        sub = lax.broadcasted_iota(jnp.int32, (8, 1), 0)
        for i in range(ns):
            vi = v[i:i + 1, :]
            for r, vg in enumerate(groups):
                if r * 8 + 7 < i:
                    cnt = jnp.where(vi > vg, 1.0, 0.0)
                elif r * 8 > i:
                    cnt = jnp.where(vi >= vg, 1.0, 0.0)
                else:
                    tie = jnp.where(sub + r * 8 > i, 1.0, 0.0)
                    cnt = jnp.where(vi > vg, 1.0, 0.0) + jnp.where(vi == vg, tie, 0.0)
                ranks[r] = ranks[r] + cnt
        neg = jnp.where(jnp.concatenate(ranks, axis=0) < n_top, 0.0, NEG)
        if ns < HEAD_DIM:
            neg = jnp.concatenate([neg, jnp.zeros((HEAD_DIM - ns, Q_BLOCK), F32)], axis=0)
        blk = jnp.concatenate([neg, zeros] if h == 0 else [zeros, neg], axis=0).astype(BF16)
        for g in range(GROUP):
            i = h * GROUP + g
            qat_scr[LANES:, i * Q_BLOCK:(i + 1) * Q_BLOCK] = blk

    m_scr[...] = jnp.full(m_scr.shape, -jnp.inf, F32)
    acc_scr[...] = jnp.zeros(acc_scr.shape, F32)
    last = (qi * Q_BLOCK) // SEL_TK

    def scores(st_ref, kt):
        k0 = pl.multiple_of(kt * SEL_TK, SEL_TK)
        st_ref[...] = _dot(ka_ref[pl.ds(k0, SEL_TK), :], qat_scr[...])

    def attend(st_ref, kt, on_diagonal):
        k0 = pl.multiple_of(kt * SEL_TK, SEL_TK)
        vt = vst_ref[:, pl.ds(k0, SEL_TK)]
        m_old = m_scr[...]
        m_news, pvs = [], []
        for c in range(ncol // SEL_TC):
            cs = slice(c * SEL_TC, (c + 1) * SEL_TC)
            st = st_ref[:, cs]
            if on_diagonal:
                kpos = k0 + lax.broadcasted_iota(jnp.int32, (SEL_TK, 1), 0)
                st = jnp.where(kpos <= qpos[:, cs], st, NEG)
            m_new = jnp.maximum(m_old[:, cs], jnp.max(st, axis=0, keepdims=True))
            pvs.append(_dot(vt, jnp.exp2(st - m_new).astype(BF16)))
            m_news.append(m_new)
        m_new = jnp.concatenate(m_news, axis=1)
        acc_scr[...] = jnp.exp2(m_old - m_new) * acc_scr[...] + jnp.concatenate(pvs, axis=1)
        m_scr[...] = m_new

    scores(sa_scr, 0)

    def tile_pair(pp, c):
        scores(sb_scr, 2 * pp + 1)
        attend(sa_scr, 2 * pp, False)
        scores(sa_scr, 2 * pp + 2)
        attend(sb_scr, 2 * pp + 1, False)
        return c
    lax.fori_loop(0, last // 2, tile_pair, 0)

    @pl.when(last % 2 == 0)
    def _():
        attend(sa_scr, last, True)

    @pl.when(last % 2 == 1)
    def _():
        scores(sb_scr, last)
        attend(sa_scr, last - 1, False)
        attend(sb_scr, last, True)

    ot = (osum_scr[...] + acc_scr[:LANES, :] * (gate_row(1) / acc_scr[LANES:LANES + 1, :])) + ow_scr[...]

    for g in range(GROUP):
        cols = lambda h: slice((h * GROUP + g) * Q_BLOCK, (h * GROUP + g + 1) * Q_BLOCK)
        blk = jnp.concatenate([ot[h * HEAD_DIM:(h + 1) * HEAD_DIM, cols(h)] for h in range(N_KV_HEADS)], axis=0)
        o_ref[0, g] = blk.T


def _prompt_attn(kvc, qxt, gatet, ka, vst, kw, vwt, cw, gkc, ctabs, seg, mimp, b, t):
    n_sub = t // CMP_STRIDE
    ns = t // SEL_BLOCK
    n_top = min(N_SEL, ns)
    nq = t // Q_BLOCK
    ncol = N_Q_HEADS * Q_BLOCK
    const = lambda a: pl.BlockSpec(a.shape, lambda bi, qi: (0,) * a.ndim)
    qcol = lambda n: pl.BlockSpec((n, Q_BLOCK), lambda bi, qi: (0, bi * nq + qi))
    seq_rows = lambda n: pl.BlockSpec((t, n), lambda bi, qi: (bi, 0))
    seq_cols = lambda n: pl.BlockSpec((n, t), lambda bi, qi: (0, bi))
    consts = [*cw, gkc, *ctabs, seg, mimp]
    kvc3 = kvc.reshape(b, t, KV_COLS)
    return pl.pallas_call(
        functools.partial(_prompt_attn_body, n_sub=n_sub, ns=ns, n_top=n_top),
        grid=(b, nq),
        in_specs=[pl.BlockSpec((1, t, LANES), lambda bi, qi: (bi, 0, 0)), pl.BlockSpec((1, t, LANES), lambda bi, qi: (bi, 0, 1)),
                  qcol(N_Q_HEADS * LANES), qcol(LANES), seq_rows(2 * LANES), seq_cols(V_ROWS), seq_rows(LANES), seq_cols(V_ROWS)]
                 + [const(a) for a in consts],
        out_specs=pl.BlockSpec((1, GROUP, Q_BLOCK, LANES), lambda bi, qi: (bi, 0, qi, 0)),
        out_shape=jax.ShapeDtypeStruct((b, GROUP, t, LANES), F32),
        scratch_shapes=[pltpu.VMEM((n_sub, LANES), BF16), pltpu.VMEM((LANES, n_sub), BF16),
                        pltpu.VMEM((2 * LANES, ncol), BF16), pltpu.VMEM((1, ncol), F32), pltpu.VMEM((V_ROWS, ncol), F32),
                        pltpu.VMEM((SEL_TK, ncol), F32), pltpu.VMEM((SEL_TK, ncol), F32), pltpu.VMEM((LANES, ncol), F32),
                        pltpu.VMEM((LANES, ncol), F32)],
        compiler_params=pltpu.CompilerParams(dimension_semantics=("arbitrary", "arbitrary"), vmem_limit_bytes=VMEM_LIMIT),
        name="prompt_attn",
    )(kvc3, kvc3, qxt, gatet, ka, vst, kw, vwt, *consts)


def _rglru_coeffs(xc, wra_ref, bra_ref, wrx_ref, brx_ref, lam_ref):
    rs, is_ = [], []
    for s in range(D_RNN // LANES):
        xs = xc[:, s * LANES:(s + 1) * LANES].astype(BF16)
        rs.append(_dot(xs, wra_ref[s]))
        is_.append(_dot(xs, wrx_ref[s]))
    r = _sigmoid(jnp.concatenate(rs, axis=1) + bra_ref[...])
    i = _sigmoid(jnp.concatenate(is_, axis=1) + brx_ref[...])
    z = -lam_ref[...]
    softplus = jnp.maximum(z, 0.0) + jnp.log1p(jnp.exp(-jnp.abs(z)))
    log_a = -RG_C * r * softplus
    a = jnp.exp(log_a)
    u = jnp.sqrt(-jnp.tanh(log_a) * (a * a + 1.0)) * (i * xc)
    return a, u


RNN_TC = 256


def _prompt_rnn_body(xr_ref, zr_ref, cw_ref, cb_ref, wra_ref, bra_ref, wrx_ref, brx_ref, lam_ref,
                     rb_ref, conv_ref, h_ref, xp_scr, hc_scr):
    tc = pl.program_id(1)

    @pl.when(tc == 0)
    def _():
        xp_scr[0:8, :] = jnp.zeros((8, D_RNN), F32)
        hc_scr[...] = jnp.zeros(hc_scr.shape, F32)

    x = xr_ref[...]
    xp_scr[8:8 + RNN_TC, :] = x
    xc = cb_ref[...]
    for k in range(CONV_W):
        off = 8 - (CONV_W - 1) + k
        xc = xc + xp_scr[off:off + RNN_TC, :] * cw_ref[k:k + 1, :]
    xp_scr[0:8, :] = x[RNN_TC - 8:, :]
    conv_ref[0] = x[RNN_TC - 8:, :]

    a, u = _rglru_coeffs(xc, wra_ref, bra_ref, wrx_ref, brx_ref, lam_ref)
    row = lax.broadcasted_iota(jnp.int32, (RNN_TC, 1), 0)
    d = 1
    while d < RNN_TC:
        keep = row >= d
        a_sh = jnp.where(keep, pltpu.roll(a, d, 0), 1.0)
        u_sh = jnp.where(keep, pltpu.roll(u, d, 0), 0.0)
        u = a * u_sh + u
        a = a * a_sh
        d *= 2
    h = u + a * hc_scr[...]
    hc_scr[...] = h[RNN_TC - 1:, :]
    h_ref[0] = h[RNN_TC - 1:, :]
    rb_ref[...] = (h * zr_ref[...]).astype(BF16)


def _prompt_rnn(xr, zr, rw, b, t):
    ntc = t // RNN_TC
    const = lambda a: pl.BlockSpec(a.shape, lambda bi, ti: (0,) * a.ndim)
    row = pl.BlockSpec((RNN_TC, D_RNN), lambda bi, ti: (bi * ntc + ti, 0))
    return pl.pallas_call(
        _prompt_rnn_body,
        grid=(b, ntc),
        in_specs=[row, row] + [const(a) for a in rw],
        out_specs=[row, pl.BlockSpec((1, 8, D_RNN), lambda bi, ti: (bi, 0, 0)),
                   pl.BlockSpec((1, 1, D_RNN), lambda bi, ti: (bi, 0, 0))],
        out_shape=[jax.ShapeDtypeStruct((b * t, D_RNN), BF16), jax.ShapeDtypeStruct((b, 8, D_RNN), F32),
                   jax.ShapeDtypeStruct((b, 1, D_RNN), F32)],
        scratch_shapes=[pltpu.VMEM((8 + RNN_TC, D_RNN), F32), pltpu.VMEM((1, D_RNN), F32)],
        compiler_params=pltpu.CompilerParams(dimension_semantics=("arbitrary", "arbitrary"), vmem_limit_bytes=VMEM_LIMIT),
        name="prompt_rnn",
    )(xr, zr, *rw)


def _sample_rnn_body(xr_ref, zr_ref, sc_ref, h0_ref, cw_ref, cb_ref, wra_ref, bra_ref, wrx_ref, brx_ref, lam_ref,
                     rb_ref, conv_ref, h_ref):
    x = xr_ref[...]
    xc = cb_ref[...]
    for k in range(CONV_W - 1):
        xc = xc + sc_ref[k] * cw_ref[k:k + 1, :]
    xc = xc + x * cw_ref[CONV_W - 1:CONV_W, :]
    for k in range(CONV_W - 2):
        conv_ref[k] = sc_ref[k + 1]
    conv_ref[CONV_W - 2] = x
    a, u = _rglru_coeffs(xc, wra_ref, bra_ref, wrx_ref, brx_ref, lam_ref)
    h = a * h0_ref[...] + u
    h_ref[...] = h
    rb_ref[...] = (h * zr_ref[...]).astype(BF16)


def _sample_rnn(xr, zr, sc, h0, rw):
    n = xr.shape[0]
    return pl.pallas_call(
        _sample_rnn_body,
        out_shape=[jax.ShapeDtypeStruct((n, D_RNN), BF16), jax.ShapeDtypeStruct((CONV_W - 1, n, D_RNN), F32),
                   jax.ShapeDtypeStruct((n, D_RNN), F32)],
        compiler_params=pltpu.CompilerParams(vmem_limit_bytes=VMEM_LIMIT),
        name="sample_rnn",
    )(xr, zr, sc, h0, *rw)


def _outproj_body(x_ref, *refs, n_branch):
    o_refs = refs[:n_branch]
    zn_ref, rb_ref, gm_ref, wpa_ref, wpb_ref, wo_ref, y_ref = refs[n_branch:]
    pa = None
    for g in range(GROUP):
        o = o_refs[0][0, g]
        for r in o_refs[1:]:
            o = o + r[0, g]
        a = (o * zn_ref[:, g * LANES:(g + 1) * LANES]).astype(BF16)
        term = _dot(a, wpa_ref[g])
        pa = term if pa is None else pa + term
    pb = _dot(rb_ref[...], wpb_ref[...])
    merged = gm_ref[:, :D_MODEL] * pa + gm_ref[:, D_MODEL:] * pb
    y_ref[...] = x_ref[...] + _dot(merged.astype(BF16), wo_ref[...])


def _outproj(x2d, os_, zn, rb, gm, wpa, wpb, wo, tm):
    b, _, t, _ = os_[0].shape
    nt = t // tm
    row = lambda n: pl.BlockSpec((tm, n), lambda i: (i, 0))
    oblk = pl.BlockSpec((1, GROUP, tm, LANES), lambda i: (i // nt, 0, i % nt, 0))
    const = lambda a: pl.BlockSpec(a.shape, lambda i: (0,) * a.ndim)
    return pl.pallas_call(
        functools.partial(_outproj_body, n_branch=len(os_)),
        grid=(b * nt,),
        in_specs=[row(D_MODEL)] + [oblk] * len(os_) + [row(NSA_WIDTH), row(D_RNN), row(2 * D_MODEL), const(wpa), const(wpb), const(wo)],
        out_specs=row(D_MODEL),
        out_shape=jax.ShapeDtypeStruct(x2d.shape, F32),
        compiler_params=pltpu.CompilerParams(dimension_semantics=("arbitrary",), vmem_limit_bytes=VMEM_LIMIT),
        name="outproj",
    )(x2d, *os_, zn, rb, gm, wpa, wpb, wo)


def _sample_cmp_body(pt_ref, cache_ref, qbd_ref, gate_ref, wbdk_ref, w2k_ref, pek_ref, w1tk_ref, wbdv_ref, w2v_ref, pev_ref,
                     w1tv_ref, gkc_ref, cos_ref, s1_ref, s2_ref, seg_ref, mimp_ref, perm_ref,
                     oc_ref, idx_ref, buf, sem, rk, rv, pb_scr, acck_scr, accv_scr, *, n_pages, past, ns, n_top):
    s = pl.program_id(0)
    n_seq = pl.num_programs(0) - 1
    slot = s % 2
    sub_per_page = PAGE_SIZE // CMP_STRIDE
    n_sub = n_pages * sub_per_page
    nsp = mimp_ref.shape[1]

    def page_copy(sl, seq, p):
        return pltpu.make_async_copy(cache_ref.at[pt_ref[seq * n_pages + p]], buf.at[sl, p], sem.at[sl])

    def start_all(sl, seq):
        lax.fori_loop(0, n_pages, lambda p, c: (page_copy(sl, seq, p).start(), c)[1], 0, unroll=8)

    @pl.when(s == 0)
    def _():
        start_all(0, 0)
        pb_scr[0:1, :] = _pos_bias(pek_ref, w1tk_ref)
        pb_scr[1:2, :] = _pos_bias(pev_ref, w1tv_ref)
        acck_scr[...] = jnp.zeros(acck_scr.shape, F32)
        accv_scr[...] = jnp.zeros(accv_scr.shape, F32)

    @pl.when(s + 1 < n_seq)
    def _():
        start_all(1 - slot, s + 1)

    @pl.when(s < n_seq)
    def _():
        lax.fori_loop(0, n_pages, lambda p, c: (page_copy(slot, s, p).wait(), c)[1], 0, unroll=8)

    def regroup(pp, c):
        ya = _dot_nt(perm_ref[...], buf[slot, 2 * pp].astype(BF16))
        yb = _dot_nt(perm_ref[...], buf[slot, 2 * pp + 1].astype(BF16))
        r0 = pl.multiple_of(pp * (2 * sub_per_page), 2 * sub_per_page)
        for j in range(CMP_STRIDE):
            rows = slice(j * sub_per_page, (j + 1) * sub_per_page)
            y = jnp.concatenate([ya[rows], yb[rows]], axis=0).astype(BF16)
            rk[pl.ds(r0, 2 * sub_per_page), j * LANES:(j + 1) * LANES] = y[:, :LANES]
            rv[pl.ds(r0, 2 * sub_per_page), j * LANES:(j + 1) * LANES] = y[:, LANES:]
        return c
    lax.fori_loop(0, n_pages // 2, regroup, 0, unroll=4)

    acc_k = acck_scr[...]
    acc_v = accv_scr[...]

    n_chunk = 4
    rows_c = n_sub // n_chunk

    def first_layer(c):
        rs = slice(c * rows_c, (c + 1) * rows_c)
        acck_scr[rs, :] = _dot(rk[rs, :], wbdk_ref[...])
        accv_scr[rs, :] = _dot(rv[rs, :], wbdv_ref[...])

    first_layer(0)
    kc = _compress_finish(acc_k, n_sub, w2k_ref, pb_scr[0:1, :])
    vc = _compress_finish(acc_v, n_sub, w2v_ref, pb_scr[1:2, :]).astype(BF16)
    first_layer(1)
    kc = _headnorm_rope(kc, gkc_ref[...], cos_ref[...], s1_ref[...], s2_ref[...], seg_ref[...]).astype(BF16)

    nr = N_Q_HEADS
    c_end = lax.broadcasted_iota(jnp.int32, (1, n_sub), 1) * CMP_STRIDE + (CMP_BLOCK - 1)
    p = _masked_softmax(_dot_nt(qbd_ref[0].astype(BF16), kc), jnp.broadcast_to(c_end <= past, (nr, n_sub)))
    first_layer(2)
    o = _dot(p.astype(BF16), vc)
    oc_ref[0] = _own_half(o, gate_ref[0])

    psum = jnp.concatenate([jnp.sum(p[h * GROUP:(h + 1) * GROUP], axis=0, keepdims=True) for h in range(N_KV_HEADS)]
                           + [jnp.zeros((nr - N_KV_HEADS, n_sub), F32)], axis=0)
    jl = lax.broadcasted_iota(jnp.int32, (1, nsp), 1)
    cur = past // SEL_BLOCK
    forced = (jl == 0) | (jl == cur) | (jl == cur - 1)
    v = _dot_exact01(psum, mimp_ref[...]) + jnp.where(forced, FORCE_BONUS, 0.0)
    first_layer(3)
    v = jnp.where(jl < ns, v, -1.0)
    vt = v.T
    js = lax.broadcasted_iota(jnp.int32, (nsp, 1), 0)
    rl = lax.broadcasted_iota(jnp.int32, (1, LANES), 1)
    earlier = jnp.where(jl < js, 1.0, 0.0)
    idx_rows = []
    for h in range(N_KV_HEADS):
        vrow, vcol = v[h:h + 1, :], vt[:, h:h + 1]
        before = jnp.where(vrow > vcol, 1.0, 0.0) + jnp.where(vrow == vcol, earlier, 0.0)
        rank = jnp.sum(before, axis=1, keepdims=True)
        hit = rank == rl.astype(F32)
        idx_rows.append(jnp.sum(jnp.where(hit, js, 0), axis=0, keepdims=True))
    idx_ref[0] = jnp.concatenate(idx_rows + [jnp.zeros((nr - N_KV_HEADS, LANES), jnp.int32)], axis=0)


def _sample_cmp(pt_flat, cache_t, qbd, gate_c, cw, gkc, ctabs, seg, mimp, n_seq, n_pages, past, ns, n_top):
    sub_per_page = PAGE_SIZE // CMP_STRIDE
    n_sub = n_pages * sub_per_page
    const = lambda a: pl.BlockSpec(a.shape, lambda s, pt: (0,) * a.ndim)
    seqblk = pl.BlockSpec((1, N_Q_HEADS, LANES), lambda s, pt: (jnp.maximum(s - 1, 0), 0, 0))
    tokens = np.arange(PAGE_SIZE)
    dst_row = (tokens % CMP_STRIDE) * sub_per_page + tokens // CMP_STRIDE
    perm = jnp.asarray(np.arange(PAGE_SIZE)[:, None] == dst_row[None, :], BF16)
    flat = lambda w: w.reshape(CMP_STRIDE * LANES, 2 * LANES)
    cw = (flat(cw[0]), *cw[1:4], flat(cw[4]), *cw[5:])
    consts = [*cw, gkc, *ctabs, seg, mimp, perm]
    return pl.pallas_call(
        functools.partial(_sample_cmp_body, n_pages=n_pages, past=past, ns=ns, n_top=n_top),
        grid_spec=pltpu.PrefetchScalarGridSpec(
            num_scalar_prefetch=1,
            grid=(n_seq + 1,),
            in_specs=[pl.BlockSpec(memory_space=pl.ANY), seqblk, seqblk] + [const(a) for a in consts],
            out_specs=[seqblk, seqblk],
            scratch_shapes=[pltpu.VMEM((2, n_pages, KV_COLS, PAGE_SIZE), F32), pltpu.SemaphoreType.DMA((2,)),
                            pltpu.VMEM((n_sub, CMP_STRIDE * LANES), BF16), pltpu.VMEM((n_sub, CMP_STRIDE * LANES), BF16),
                            pltpu.VMEM((8, LANES), F32), pltpu.VMEM((n_sub, 2 * LANES), F32), pltpu.VMEM((n_sub, 2 * LANES), F32)],
        ),
        out_shape=[jax.ShapeDtypeStruct((n_seq, N_Q_HEADS, LANES), F32), jax.ShapeDtypeStruct((n_seq, N_Q_HEADS, LANES), jnp.int32)],
        compiler_params=pltpu.CompilerParams(dimension_semantics=("arbitrary",), vmem_limit_bytes=VMEM_LIMIT),
        name="sample_cmp",
    )(pt_flat, cache_t, qbd, gate_c, *consts)


def _decode_attend(qbd, k_t, v_t, bias, new_row):
    s = _dot(qbd.astype(BF16), k_t)
    if bias is not None:
        s = s + bias
    s_new = jnp.sum(qbd * new_row[:, :LANES], axis=1, keepdims=True)
    m = jnp.maximum(jnp.max(s, axis=1, keepdims=True), s_new)
    e = jnp.exp(s - m)
    e_new = jnp.exp(s_new - m)
    den = jnp.sum(e, axis=1, keepdims=True) + e_new
    num = _dot_nt(e.astype(BF16), v_t) + e_new * new_row[:, LANES:]
    return num / den


def _own_half(o, gate):
    lane = lax.broadcasted_iota(jnp.int32, o.shape, 1)
    rowh = lax.broadcasted_iota(jnp.int32, o.shape, 0) // GROUP
    return jnp.where(lane // HEAD_DIM == rowh, o * _sigmoid(gate), 0.0)


def _sample_sel_body(pg_ref, meta_ref, cache_ref, qbd_ref, kvrow_ref, gate_ref, o_ref, buf, sem, *, n_top):
    s = pl.program_id(0)
    n_seq = pl.num_programs(0)
    slot = s % 2
    n_slots = N_KV_HEADS * n_top

    def page_copy(sl, seq, i):
        return pltpu.make_async_copy(cache_ref.at[pg_ref[seq * n_slots + i]], buf.at[sl, i], sem.at[sl])

    def start_all(sl, seq):
        lax.fori_loop(0, n_slots, lambda i, c: (page_copy(sl, seq, i).start(), c)[1], 0, unroll=8)

    @pl.when(s == 0)
    def _():
        start_all(0, 0)

    @pl.when(s + 1 < n_seq)
    def _():
        start_all(1 - slot, s + 1)

    lax.fori_loop(0, n_slots, lambda i, c: (page_copy(slot, s, i).wait(), c)[1], 0, unroll=8)

    qbd = qbd_ref[0]
    new_row = kvrow_ref[0]
    lane = lax.broadcasted_iota(jnp.int32, (1, PAGE_SIZE), 1)
    outs = []
    for h in range(N_KV_HEADS):
        bias, k_parts, v_parts = [], [], []
        for r in range(n_top):
            lo = meta_ref[s * n_slots + h * n_top + r] * SEL_BLOCK
            bias.append(jnp.where((lane >= lo) & (lane < lo + SEL_BLOCK), 0.0, -jnp.inf))
            k_parts.append(buf[slot, h * n_top + r, :LANES, :].astype(BF16))
            v_parts.append(buf[slot, h * n_top + r, LANES:, :].astype(BF16))
        outs.append(_decode_attend(qbd, jnp.concatenate(k_parts, axis=1), jnp.concatenate(v_parts, axis=1),
                                   jnp.concatenate(bias, axis=1), new_row))
    rowh = lax.broadcasted_iota(jnp.int32, outs[0].shape, 0) // GROUP
    o_ref[0] = _own_half(jnp.where(rowh == 0, outs[0], outs[1]), gate_ref[0])


def _sample_sel(pg, meta, cache_t, qbd, kvrow, gate, n_seq, n_top):
    seqblk = pl.BlockSpec((1, N_Q_HEADS, LANES), lambda s, pg, meta: (s, 0, 0))
    return pl.pallas_call(
        functools.partial(_sample_sel_body, n_top=n_top),
        grid_spec=pltpu.PrefetchScalarGridSpec(
            num_scalar_prefetch=2,
            grid=(n_seq,),
            in_specs=[pl.BlockSpec(memory_space=pl.ANY), seqblk, pl.BlockSpec((1, 1, KV_COLS), lambda s, pg, meta: (s, 0, 0)), seqblk],
            out_specs=seqblk,
            scratch_shapes=[pltpu.VMEM((2, N_KV_HEADS * n_top, KV_COLS, PAGE_SIZE), F32), pltpu.SemaphoreType.DMA((2,))],
        ),
        out_shape=jax.ShapeDtypeStruct((n_seq, N_Q_HEADS, LANES), F32),
        compiler_params=pltpu.CompilerParams(dimension_semantics=("arbitrary",), vmem_limit_bytes=VMEM_LIMIT),
        name="sample_sel",
    )(pg, meta, cache_t, qbd, kvrow, gate)


WIN_SEQS = 4


def _sample_win_body(win_ref, qbd_ref, kvrow_ref, kvcol_ref, gate_ref, o_ref, wout_ref):
    lane = lax.broadcasted_iota(jnp.int32, (1, WINDOW), 1)
    for i in range(win_ref.shape[0]):
        w = win_ref[i]
        wout_ref[i] = jnp.where(lane == WINDOW - 1, kvcol_ref[i], pltpu.roll(w, WINDOW - 1, 1))
        o = _decode_attend(qbd_ref[i], w[:LANES, :].astype(BF16), w[LANES:, :].astype(BF16), None, kvrow_ref[i])
        o_ref[i] = _own_half(o, gate_ref[i])


def _sample_win(win_t, qbd, kvrow, kvcol, gate):
    n_seq = win_t.shape[0]
    sps = WIN_SEQS if n_seq % WIN_SEQS == 0 else 1
    blk = lambda *dims: pl.BlockSpec((sps, *dims), lambda s: (s, 0, 0))
    return pl.pallas_call(
        _sample_win_body,
        grid=(n_seq // sps,),
        in_specs=[blk(KV_COLS, WINDOW), blk(N_Q_HEADS, LANES), blk(1, KV_COLS), blk(KV_COLS, 1), blk(N_Q_HEADS, LANES)],
        out_specs=[blk(N_Q_HEADS, LANES), blk(KV_COLS, WINDOW)],
        out_shape=[jax.ShapeDtypeStruct((n_seq, N_Q_HEADS, LANES), F32), jax.ShapeDtypeStruct(win_t.shape, F32)],
        compiler_params=pltpu.CompilerParams(dimension_semantics=("arbitrary",), vmem_limit_bytes=VMEM_LIMIT),
        name="sample_win",
    )(win_t, qbd, kvrow, kvcol, gate)


def _prep_inproj_weight(w_in):
    sizes = (NSA_WIDTH, 3 * KV_COLS, 3 * N_Q_HEADS, NSA_WIDTH, D_RNN, D_RNN, 2 * D_MODEL)
    offs = np.concatenate([[0], np.cumsum(sizes)])
    q, kv, gn, zn, xr, zr, gm = [w_in[:, offs[i]:offs[i + 1]] for i in range(len(sizes))]
    zn = zn.reshape(D_MODEL, N_KV_HEADS, GROUP, HEAD_DIM).transpose(0, 2, 1, 3).reshape(D_MODEL, NSA_WIDTH)
    pad = jnp.zeros((D_MODEL, LANES - 3 * N_Q_HEADS), w_in.dtype)
    return jnp.concatenate([q, kv, zn, xr, zr, gm, gn, pad], axis=1).astype(BF16)


def _prep_compress(pe, w1, w2):
    eye = jnp.eye(N_KV_HEADS, dtype=F32)
    bd = lambda w: jnp.einsum('jde,hk->jhdke', w, eye).reshape(CMP_STRIDE, LANES, LANES)
    wbd = jnp.concatenate([bd(w1[:CMP_STRIDE]), bd(w1[CMP_STRIDE:])], axis=2).astype(BF16)
    w2bd = jnp.einsum('ed,hk->hekd', w2, eye).reshape(LANES, LANES).astype(BF16)
    pe_col = pe.reshape(CMP_BLOCK * HEAD_DIM, 1)
    w1t = jnp.tile(w1.reshape(CMP_BLOCK * HEAD_DIM, CMP_HIDDEN), (1, N_KV_HEADS))
    return wbd, w2bd, pe_col, w1t


def _prep_rnn(conv_w, conv_b, w_ra, b_ra, w_rx, b_rx, lam):
    def pairs(w):
        w = w.reshape(RNN_HEADS // 2, 2, RNN_HD, RNN_HD)
        eye = jnp.eye(2, dtype=F32)
        return jnp.einsum('shij,hk->shikj', w, eye).reshape(RNN_HEADS // 2, LANES, LANES).astype(BF16)
    return (conv_w, conv_b.reshape(1, D_RNN), pairs(w_ra), b_ra.reshape(1, D_RNN), pairs(w_rx), b_rx.reshape(1, D_RNN),
            lam.reshape(1, D_RNN))


def _feature_major(cache):
    n, r = cache.shape[:2]
    return cache.transpose(0, 2, 3, 4, 1).reshape(n, KV_COLS, r)


def kernel(x_prompt, x_sample, cache_kv_cmp, cache_kv_sel, cache_kv_win, state_conv, state_h, page_table, g_norm, w_in, g_q, g_kc, g_ks, g_kw, pe_k, w1_k, w2_k, pe_v, w1_v, w2_v, conv_w, conv_b, w_ra, b_ra, w_rx, b_rx, lam, w_pa, w_pb, w_out):
    b, t, _ = x_prompt.shape
    n_seq = x_sample.shape[0]
    n_pages = page_table.shape[1]
    past = n_pages * PAGE_SIZE
    assert x_sample.shape[1] == 1 and cache_kv_win.shape[1] == WINDOW and past >= WINDOW
    assert t % SEL_TK == 0 and t // SEL_BLOCK <= HEAD_DIM and t >= WIN_KEYS and n_pages % 2 == 0

    wp = _prep_inproj_weight(w_in)
    seg = _seg01()
    tile2 = lambda g: jnp.tile(g.reshape(1, HEAD_DIM), (1, 2))
    gq, gks, gkw, gkc = tile2(g_q), tile2(g_ks), tile2(g_kw), tile2(g_kc)
    gnorm = g_norm.reshape(1, D_MODEL)
    cw = (*_prep_compress(pe_k, w1_k, w2_k), *_prep_compress(pe_v, w1_v, w2_v))
    rw = _prep_rnn(conv_w, conv_b, w_ra, b_ra, w_rx, b_rx, lam)
    wpa = w_pa.reshape(N_KV_HEADS, GROUP, HEAD_DIM, D_MODEL).transpose(1, 0, 2, 3).reshape(GROUP, LANES, D_MODEL).astype(BF16)
    wpb, wo = w_pb.astype(BF16), w_out.astype(BF16)

    tm = 256
    tok = np.arange(t)
    oh = jnp.asarray((tok[:, None] // SEL_BLOCK) == (np.arange(LANES)[None, :] % HEAD_DIM), BF16)
    xp2 = x_prompt.reshape(b * t, D_MODEL)
    (kvc, kvct, kvst, kvwt, qxt, ka, kw, vst, vwt, gatet, zn, xr, zr, gm) = _inproj(
        xp2, gnorm, wp, _rope_tables(tok), oh, gq, gks, gkw, seg, tm, t, True)
    n_sub = t // CMP_STRIDE
    ns = t // SEL_BLOCK
    mimp = jnp.asarray(_imp_matrix(n_sub, ns, n_sub - 1, ns).T, BF16)
    ctabs = _rope_tables(np.arange(n_sub) * CMP_STRIDE + (CMP_BLOCK - 1))
    o_attn = _prompt_attn(kvc, qxt, gatet, ka, vst, kw, vwt, cw, gkc, ctabs, seg, mimp, b, t)
    rb, conv_tail, h_last = _prompt_rnn(xr, zr, rw, b, t)
    y_prompt = _outproj(xp2, [o_attn], zn, rb, gm, wpa, wpb, wo, tm).reshape(b, t, D_MODEL)
    kv5t = lambda a: a.reshape(a.shape[0], 2, N_KV_HEADS, HEAD_DIM, a.shape[2]).transpose(0, 4, 1, 2, 3)
    kv_cmp_prompt, kv_sel_prompt = kv5t(kvct), kv5t(kvst)
    kv_win_prompt = kv5t(kvwt[:, :, t - min(WINDOW, t):])
    conv_prompt = conv_tail[:, 8 - (CONV_W - 1):]
    h_prompt = h_last.reshape(b, D_RNN)

    xs2 = x_sample.reshape(n_seq, D_MODEL)
    oh_s = jnp.zeros((n_seq, LANES), BF16)
    (kvc_s, kvs_s, kvw_s, qf_s, gate_s, zn_s, xr_s, zr_s, gm_s) = _inproj(
        xs2, gnorm, wp, _rope_tables(np.full((n_seq,), past)), oh_s, gq, gks, gkw, seg, n_seq, n_seq, False)
    kv5 = lambda a, rows: a.reshape(-1, rows, 2, N_KV_HEADS, HEAD_DIM)
    eye = jnp.eye(N_KV_HEADS, dtype=F32)
    q4 = qf_s.reshape(n_seq, N_KV_HEADS, GROUP, HEAD_DIM)
    qbd = (q4[:, :, :, None, :] * eye[None, :, None, :, None]).reshape(n_seq, N_Q_HEADS, LANES)
    gates3 = gate_s[:, :3 * N_Q_HEADS].reshape(n_seq, N_Q_HEADS, 3)
    gate_b = lambda k: jnp.broadcast_to(gates3[:, :, k:k + 1], (n_seq, N_Q_HEADS, LANES))

    n_sub_s = past // CMP_STRIDE
    ns_s = past // SEL_BLOCK + 1
    n_top_s = min(N_SEL, ns_s)
    nsp = -(-ns_s // LANES) * LANES
    mimp_s = jnp.asarray(_imp_matrix(n_sub_s, nsp, n_sub_s - 1, ns_s), BF16)
    ctabs_s = _rope_tables(np.arange(n_sub_s) * CMP_STRIDE + (CMP_BLOCK - 1))
    oc_s, idx = _sample_cmp(page_table.reshape(-1), _feature_major(cache_kv_cmp), qbd, gate_b(0), cw, gkc, ctabs_s, seg, mimp_s,
                            n_seq, n_pages, past, ns_s, n_top_s)
    blocks = idx[:, :N_KV_HEADS, :n_top_s]
    per_page = PAGE_SIZE // SEL_BLOCK
    in_cache = blocks < past // SEL_BLOCK
    pg = jnp.take_along_axis(page_table, jnp.minimum(blocks // per_page, n_pages - 1).reshape(n_seq, -1), axis=1)
    meta = jnp.where(in_cache, blocks % per_page, per_page).reshape(-1).astype(jnp.int32)
    o_sel_s = _sample_sel(pg.reshape(-1).astype(jnp.int32), meta, _feature_major(cache_kv_sel), qbd,
                          kvs_s.reshape(n_seq, 1, KV_COLS), gate_b(1), n_seq, n_top_s)
    o_win_s, win_new = _sample_win(_feature_major(cache_kv_win), qbd, kvw_s.reshape(n_seq, 1, KV_COLS),
                                   kvw_s.reshape(n_seq, KV_COLS, 1), gate_b(2))
    sc_t = state_conv.transpose(1, 0, 2)
    rb_s, conv_new, h_sample = _sample_rnn(xr_s, zr_s, sc_t, state_h, rw)
    to_o = lambda o: (o[:, :GROUP] + o[:, GROUP:]).transpose(1, 0, 2)[None]
    y_sample = _outproj(xs2, [to_o(oc_s), to_o(o_sel_s), to_o(o_win_s)], zn_s, rb_s, gm_s, wpa, wpb, wo, n_seq).reshape(n_seq, 1, D_MODEL)
    kv_cmp_sample, kv_sel_sample = kv5(kvc_s, 1), kv5(kvs_s, 1)
    kv_win_sample = win_new.reshape(n_seq, 2, N_KV_HEADS, HEAD_DIM, WINDOW).transpose(0, 4, 1, 2, 3)
    conv_sample = conv_new.transpose(1, 0, 2)

    return (y_prompt, y_sample, kv_cmp_prompt, kv_cmp_sample, kv_sel_prompt, kv_sel_sample,
            kv_win_prompt, kv_win_sample, conv_prompt, conv_sample, h_prompt, h_sample)
```

```python
import functools

import numpy as np
import jax
import jax.numpy as jnp
from jax import lax
from jax.experimental import pallas as pl
from jax.experimental.pallas import tpu as pltpu

D_MODEL = 1024
HEAD_DIM = 64
N_Q_HEADS = 8
N_KV_HEADS = 2
GROUP = N_Q_HEADS // N_KV_HEADS
NSA_WIDTH = N_Q_HEADS * HEAD_DIM
SCALE = HEAD_DIM ** -0.5
ROPE_DIM = HEAD_DIM // 4
ROPE_HALF = ROPE_DIM // 2
ROPE_THETA = 500000.0
CMP_BLOCK = 32
CMP_STRIDE = 16
CMP_HIDDEN = 64
SEL_BLOCK = 64
N_SEL = 16
WINDOW = 512
Q_BLOCK = 256
FORCE_BONUS = 1.0e4
D_RNN = D_MODEL // 2
RNN_HEADS = 8
RNN_HD = D_RNN // RNN_HEADS
CONV_W = 4
RG_C = 8.0
EPS = 1e-6
PAGE_SIZE = 128

LANES = 128
KV_COLS = 2 * N_KV_HEADS * HEAD_DIM
NEG = -(2.0 ** 100)
LOG2E = 1.4426950408889634
V_ROWS = LANES + 16
VMEM_LIMIT = 56 * 1024 * 1024
F32 = jnp.float32
BF16 = jnp.bfloat16

C_Q = 0
C_KV = C_Q + NSA_WIDTH
C_ZN = C_KV + 3 * KV_COLS
C_XR = C_ZN + NSA_WIDTH
C_ZR = C_XR + D_RNN
C_GM = C_ZR + D_RNN
C_GN = C_GM + 2 * D_MODEL
N_WCOLS = C_GN + LANES


def _dot(a, b):
    return jnp.dot(a, b, preferred_element_type=F32)


def _dot_nt(a, b):
    return lax.dot_general(a, b, (((1,), (1,)), ((), ())), preferred_element_type=F32)


def _split3(a):
    a1 = a.astype(BF16)
    r1 = a - a1.astype(F32)
    a2 = r1.astype(BF16)
    a3 = (r1 - a2.astype(F32)).astype(BF16)
    return a1, a2, a3


def _dot_exact01(a, b01):
    a1, a2, a3 = _split3(a)
    return _dot(a1, b01) + _dot(a2, b01) + _dot(a3, b01)


def _dot_01_exact(b01, a):
    a1, a2, a3 = _split3(a)
    return _dot(b01, a1) + _dot(b01, a2) + _dot(b01, a3)


def _sigmoid(x):
    return 1.0 / (1.0 + jnp.exp(-x))


def _silu(x):
    return x * _sigmoid(x)


def _headnorm_rope(xs, gain, cos, s1, s2, seg01):
    ss = _dot_exact01(xs * xs, seg01)
    y = xs * lax.rsqrt(ss * (1.0 / HEAD_DIM) + EPS) * gain
    return y * cos + pltpu.roll(y, ROPE_HALF, 1) * s1 + pltpu.roll(y, LANES - ROPE_HALF, 1) * s2


def _masked_softmax(s, mask):
    s = jnp.where(mask, s, -jnp.inf)
    m = jnp.max(s, axis=-1, keepdims=True)
    m = jnp.where(m == -jnp.inf, 0.0, m)
    e = jnp.where(mask, jnp.exp(s - m), 0.0)
    d = jnp.sum(e, axis=-1, keepdims=True)
    return e / jnp.where(d > 0, d, 1.0)


def _masked_softmax_t(s, mask):
    s = jnp.where(mask, s, -jnp.inf)
    m = jnp.max(s, axis=0, keepdims=True)
    m = jnp.where(m == -jnp.inf, 0.0, m)
    e = jnp.where(mask, jnp.exp2(s - m), 0.0)
    d = jnp.sum(e, axis=0, keepdims=True)
    return e * (1.0 / jnp.where(d > 0, d, 1.0))


def _rope_tables(pos):
    pos = np.asarray(pos, np.float64)
    inv = ROPE_THETA ** (-np.arange(ROPE_HALF, dtype=np.float64) / ROPE_HALF)
    ang = (pos.astype(np.float32)[:, None] * inv.astype(np.float32)[None, :]).astype(np.float32).astype(np.float64)
    cos, sin = np.cos(ang), np.sin(ang)
    n = pos.shape[0]
    c = np.ones((n, HEAD_DIM)); s1 = np.zeros((n, HEAD_DIM)); s2 = np.zeros((n, HEAD_DIM))
    c[:, :ROPE_HALF] = cos; c[:, ROPE_HALF:ROPE_DIM] = cos
    s1[:, ROPE_HALF:ROPE_DIM] = sin
    s2[:, :ROPE_HALF] = -sin
    t = lambda a: jnp.asarray(np.tile(a, (1, 2)), F32)
    return t(c), t(s1), t(s2)


def _seg01():
    lane = np.arange(LANES)
    return jnp.asarray((lane[:, None] // HEAD_DIM) == (lane[None, :] // HEAD_DIM), BF16)


def _imp_matrix(nc_pad, ns_pad, nc, ns):
    r = SEL_BLOCK // CMP_STRIDE
    lead = CMP_BLOCK // CMP_STRIDE - 1
    c = np.arange(nc_pad)[:, None]
    j = np.arange(ns_pad)[None, :]
    m = (c >= r * j - lead) & (c <= r * j + r - 1) & (c < nc) & (j < ns)
    return m


def _values_t(v, ref):
    ref[:LANES, :] = v.T.astype(BF16)
    ones_row = lax.broadcasted_iota(jnp.int32, (V_ROWS - LANES, v.shape[0]), 0) == 0
    ref[LANES:, :] = jnp.where(ones_row, 1.0, 0.0).astype(BF16)


def _inproj_body(x_ref, gn_ref, w_ref, cos_ref, s1_ref, s2_ref, oh_ref, gq_ref, gks_ref, gkw_ref, seg_ref, *out_refs, prompt):
    if prompt:
        (kvc_ref, kvct_ref, kvst_ref, kvwt_ref, qxt_ref, ka_ref, kw_ref, vst_ref, vwt_ref, gatet_ref,
         zn_ref, xr_ref, zr_ref, gm_ref) = out_refs
    else:
        kvc_ref, kvs_ref, kvw_ref, qf_ref, gate_ref, zn_ref, xr_ref, zr_ref, gm_ref = out_refs
    x = x_ref[...]
    r = lax.rsqrt(jnp.mean(x * x, axis=-1, keepdims=True) + EPS)
    u = (x * r * gn_ref[...]).astype(BF16)
    cos, s1, s2, seg = cos_ref[...], s1_ref[...], s2_ref[...], seg_ref[...]
    lane = lax.broadcasted_iota(jnp.int32, (1, LANES), 1)

    def proj(c0, n):
        return _dot(u, w_ref[:, c0:c0 + n])

    q = proj(C_Q, NSA_WIDTH)
    kv = proj(C_KV, 3 * KV_COLS)
    zn_ref[...] = _silu(proj(C_ZN, NSA_WIDTH)).astype(BF16)

    for s in range(NSA_WIDTH // LANES):
        qs = _headnorm_rope(q[:, s * LANES:(s + 1) * LANES], gq_ref[...], cos, s1, s2, seg) * SCALE
        if not prompt:
            qf_ref[:, s * LANES:(s + 1) * LANES] = qs
            continue
        qs = qs * LOG2E
        qs_sw = pltpu.roll(qs, HEAD_DIM, 1)
        for half in range(2):
            i = 2 * s + half
            h = i // GROUP
            src = qs if half == h else qs_sw
            keep = (lane >= h * HEAD_DIM) & (lane < (h + 1) * HEAD_DIM)
            qxt_ref[i * LANES:(i + 1) * LANES, :] = jnp.where(keep, src, 0.0).T.astype(BF16)

    xr_ref[...] = proj(C_XR, D_RNN)
    zr_ref[...] = _silu(proj(C_ZR, D_RNN)).astype(BF16)

    kc, vc = kv[:, :LANES], kv[:, LANES:KV_COLS]
    ks = _headnorm_rope(kv[:, KV_COLS:KV_COLS + LANES], gks_ref[...], cos, s1, s2, seg)
    vs = kv[:, KV_COLS + LANES:2 * KV_COLS]
    gm_ref[:, :D_MODEL] = _sigmoid(proj(C_GM, D_MODEL)).astype(BF16)
    kw = _headnorm_rope(kv[:, 2 * KV_COLS:2 * KV_COLS + LANES], gkw_ref[...], cos, s1, s2, seg)
    vw = kv[:, 2 * KV_COLS + LANES:]
    gm_ref[:, D_MODEL:] = _sigmoid(proj(C_GM + D_MODEL, D_MODEL)).astype(BF16)
    kvc_ref[...] = kv[:, :KV_COLS]
    if prompt:
        for ref, k, v in ((kvct_ref, kc, vc), (kvst_ref, ks, vs), (kvwt_ref, kw, vw)):
            ref[0, :LANES, :] = k.T
            ref[0, LANES:, :] = v.T
        ka_ref[:, :LANES] = ks.astype(BF16)
        ka_ref[:, LANES:] = oh_ref[...]
        kw_ref[...] = kw.astype(BF16)
        _values_t(vs, vst_ref)
        _values_t(vw, vwt_ref)
        gatet_ref[...] = proj(C_GN, LANES).T
    else:
        kvs_ref[:, :LANES] = ks
        kvs_ref[:, LANES:] = vs
        kvw_ref[:, :LANES] = kw
        kvw_ref[:, LANES:] = vw
        gate_ref[...] = proj(C_GN, LANES)


def _inproj(x2d, gnorm, wp, tabs, oh, gq, gks, gkw, seg, tm, t, prompt):
    rows = x2d.shape[0]
    nt = t // tm
    row = lambda n: pl.BlockSpec((tm, n), lambda i: (i, 0))
    col = lambda n: pl.BlockSpec((n, tm), lambda i: (0, i))
    tab = pl.BlockSpec((tm, LANES), lambda i: (i % nt, 0))
    const = lambda a: pl.BlockSpec(a.shape, lambda i: (0,) * a.ndim)
    rowshape = lambda n, dt: jax.ShapeDtypeStruct((rows, n), dt)
    colshape = lambda n, dt: jax.ShapeDtypeStruct((n, rows), dt)
    tail = [(row(NSA_WIDTH), rowshape(NSA_WIDTH, BF16)), (row(D_RNN), rowshape(D_RNN, F32)), (row(D_RNN), rowshape(D_RNN, BF16)),
            (row(2 * D_MODEL), rowshape(2 * D_MODEL, BF16))]
    if prompt:
        leaf_t = (pl.BlockSpec((1, KV_COLS, tm), lambda i: (i // nt, 0, i % nt)), jax.ShapeDtypeStruct((rows // t, KV_COLS, t), F32))
        outs = [(row(KV_COLS), rowshape(KV_COLS, F32)), leaf_t, leaf_t, leaf_t,
                (col(N_Q_HEADS * LANES), colshape(N_Q_HEADS * LANES, BF16)),
                (row(2 * LANES), rowshape(2 * LANES, BF16)), (row(LANES), rowshape(LANES, BF16)),
                (col(V_ROWS), colshape(V_ROWS, BF16)), (col(V_ROWS), colshape(V_ROWS, BF16)),
                (col(LANES), colshape(LANES, F32))] + tail
    else:
        outs = [(row(KV_COLS), rowshape(KV_COLS, F32))] * 3 + [(row(NSA_WIDTH), rowshape(NSA_WIDTH, F32)),
                                                               (row(LANES), rowshape(LANES, F32))] + tail
    return pl.pallas_call(
        functools.partial(_inproj_body, prompt=prompt),
        grid=(rows // tm,),
        in_specs=[row(D_MODEL), const(gnorm), const(wp), tab, tab, tab, tab, const(gq), const(gks), const(gkw), const(seg)],
        out_specs=[o[0] for o in outs],
        out_shape=[o[1] for o in outs],
        compiler_params=pltpu.CompilerParams(dimension_semantics=("arbitrary",), vmem_limit_bytes=VMEM_LIMIT),
        name="inproj",
    )(x2d, gnorm, wp, *tabs, oh, gq, gks, gkw, seg)


def _compress_rows(load_rows, n_sub, wbd_ref, w2_ref, pe_ref, w1t_ref):
    acc = jnp.zeros((n_sub, 2 * LANES), F32)
    for j in range(CMP_STRIDE):
        acc = acc + _dot(load_rows(j).astype(BF16), wbd_ref[j])
    return _compress_finish(acc, n_sub, w2_ref, _pos_bias(pe_ref, w1t_ref))


def _pos_bias(pe_ref, w1t_ref):
    return jnp.sum(pe_ref[...] * w1t_ref[...], axis=0, keepdims=True)


def _compress_finish(acc, n_sub, w2_ref, pos_bias):
    lo, hi = acc[:, :LANES], acc[:, LANES:]
    hid = _silu(lo + pltpu.roll(hi, n_sub - 1, 0) + pos_bias)
    return _dot(hid.astype(BF16), w2_ref[...])


SEL_TK = 512
SEL_TC = 256
N_BACK = WINDOW // Q_BLOCK
WIN_KEYS = (N_BACK + 1) * Q_BLOCK


def _prompt_attn_body(kraw_ref, vraw_ref, qxt_ref, gatet_ref, ka_ref, vst_ref, kw_ref, vwt_ref,
                      wbdk_ref, w2k_ref, pek_ref, w1tk_ref, wbdv_ref, w2v_ref, pev_ref, w1tv_ref,
                      gkc_ref, cos_ref, s1_ref, s2_ref, seg_ref, mimp_ref,
                      o_ref, kc_scr, vct_scr, qat_scr, m_scr, acc_scr, sa_scr, sb_scr, osum_scr, ow_scr, *, n_sub, ns, n_top):
    qi = pl.program_id(1)
    ncol = N_Q_HEADS * Q_BLOCK

    @pl.when(qi == 0)
    def _():
        def rows(ref):
            return lambda j: ref[0, pl.ds(j, n_sub, stride=CMP_STRIDE), :]
        kc = _compress_rows(rows(kraw_ref), n_sub, wbdk_ref, w2k_ref, pek_ref, w1tk_ref)
        kc = _headnorm_rope(kc, gkc_ref[...], cos_ref[...], s1_ref[...], s2_ref[...], seg_ref[...])
        kc_scr[...] = kc.astype(BF16)
        vct_scr[...] = _compress_rows(rows(vraw_ref), n_sub, wbdv_ref, w2v_ref, pev_ref, w1tv_ref).T.astype(BF16)

    for i in range(N_Q_HEADS):
        qat_scr[:LANES, i * Q_BLOCK:(i + 1) * Q_BLOCK] = qxt_ref[i * LANES:(i + 1) * LANES, :]
    qt = qat_scr[:LANES, :]
    qpos = qi * Q_BLOCK + lax.broadcasted_iota(jnp.int32, (1, ncol), 1) % Q_BLOCK
    gates = _sigmoid(gatet_ref[...])

    def gate_row(k):
        return jnp.concatenate([gates[i * 3 + k:i * 3 + k + 1, :] for i in range(N_Q_HEADS)], axis=1)

    w0 = pl.multiple_of(jnp.maximum(qi - N_BACK, 0) * Q_BLOCK, Q_BLOCK)
    sw = _dot(kw_ref[pl.ds(w0, WIN_KEYS), :], qt)
    r = lax.broadcasted_iota(jnp.int32, (Q_BLOCK, 1), 0)
    c = lax.broadcasted_iota(jnp.int32, (1, Q_BLOCK), 1)
    newest = jnp.where(r <= c, 0.0, -jnp.inf)
    oldest = jnp.where(c <= r, 0.0, -jnp.inf)
    biased = []
    for blk in range(N_BACK + 1):
        back = qi - (w0 // Q_BLOCK + blk)
        inner = jnp.where((back > 0) & (back < N_BACK), 0.0, -jnp.inf)
        bias = jnp.where(back == 0, newest, jnp.where(back == N_BACK, oldest, inner))
        biased.append(sw[blk * Q_BLOCK:(blk + 1) * Q_BLOCK] + jnp.concatenate([bias] * N_Q_HEADS, axis=1))
    sw = jnp.concatenate(biased, axis=0)
    e = jnp.exp2(sw - jnp.max(sw, axis=0, keepdims=True))
    ow = _dot(vwt_ref[:, pl.ds(w0, WIN_KEYS)], e.astype(BF16))
    ow_scr[...] = ow[:LANES, :] * (gate_row(2) / ow[LANES:LANES + 1, :])

    c_end = lax.broadcasted_iota(jnp.int32, (n_sub, 1), 0) * CMP_STRIDE + (CMP_BLOCK - 1)
    p = _masked_softmax_t(_dot(kc_scr[...], qt), c_end <= qpos)
    osum_scr[...] = _dot(vct_scr[...], p.astype(BF16)) * gate_row(0)

    jrow = lax.broadcasted_iota(jnp.int32, (ns, 1), 0)
    cur = (qi * Q_BLOCK + lax.broadcasted_iota(jnp.int32, (1, Q_BLOCK), 1)) // SEL_BLOCK
    forced = (jrow == 0) | (jrow == cur) | (jrow == cur - 1)
    zeros = jnp.zeros((HEAD_DIM, Q_BLOCK), F32)
    for h in range(N_KV_HEADS):
        pg = [p[:, (h * GROUP + g) * Q_BLOCK:(h * GROUP + g + 1) * Q_BLOCK] for g in range(GROUP)]
        psum = ((pg[0] + pg[1]) + pg[2]) + pg[3]
        v = _dot_01_exact(mimp_ref[...], psum) + jnp.where(forced, FORCE_BONUS, 0.0)
        groups = [v[r * 8:(r + 1) * 8, :] for r in range(ns // 8)]
        ranks = [jnp.zeros((8, Q_BLOCK), F32) for _ in groups]
        sub = lax.broadcasted_iota(jnp.int32, (8, 1), 0)
        for i in range(ns):
            vi = v[i:i + 1, :]
            for r, vg in enumerate(groups):
                if r * 8 + 7 < i:
                    cnt = jnp.where(vi > vg, 1.0, 0.0)
                elif r * 8 > i:
                    cnt = jnp.where(vi >= vg, 1.0, 0.0)
                else:
                    tie = jnp.where(sub + r * 8 > i, 1.0, 0.0)
                    cnt = jnp.where(vi > vg, 1.0, 0.0) + jnp.where(vi == vg, tie, 0.0)
                ranks[r] = ranks[r] + cnt
        neg = jnp.where(jnp.concatenate(ranks, axis=0) < n_top, 0.0, NEG)
        if ns < HEAD_DIM:
            neg = jnp.concatenate([neg, jnp.zeros((HEAD_DIM - ns, Q_BLOCK), F32)], axis=0)
        blk = jnp.concatenate([neg, zeros] if h == 0 else [zeros, neg], axis=0).astype(BF16)
        for g in range(GROUP):
            i = h * GROUP + g
            qat_scr[LANES:, i * Q_BLOCK:(i + 1) * Q_BLOCK] = blk

    m_scr[...] = jnp.full(m_scr.shape, -jnp.inf, F32)
    acc_scr[...] = jnp.zeros(acc_scr.shape, F32)
    last = (qi * Q_BLOCK) // SEL_TK

    def scores(st_ref, kt):
        k0 = pl.multiple_of(kt * SEL_TK, SEL_TK)
        st_ref[...] = _dot(ka_ref[pl.ds(k0, SEL_TK), :], qat_scr[...])

    def attend(st_ref, kt, on_diagonal):
        k0 = pl.multiple_of(kt * SEL_TK, SEL_TK)
        vt = vst_ref[:, pl.ds(k0, SEL_TK)]
        m_old = m_scr[...]
        m_news, pvs = [], []
        for c in range(ncol // SEL_TC):
            cs = slice(c * SEL_TC, (c + 1) * SEL_TC)
            st = st_ref[:, cs]
            if on_diagonal:
                kpos = k0 + lax.broadcasted_iota(jnp.int32, (SEL_TK, 1), 0)
                st = jnp.where(kpos <= qpos[:, cs], st, NEG)
            m_new = jnp.maximum(m_old[:, cs], jnp.max(st, axis=0, keepdims=True))
            pvs.append(_dot(vt, jnp.exp2(st - m_new).astype(BF16)))
            m_news.append(m_new)
        m_new = jnp.concatenate(m_news, axis=1)
        acc_scr[...] = jnp.exp2(m_old - m_new) * acc_scr[...] + jnp.concatenate(pvs, axis=1)
        m_scr[...] = m_new

    scores(sa_scr, 0)

    def tile_pair(pp, c):
        scores(sb_scr, 2 * pp + 1)
        attend(sa_scr, 2 * pp, False)
        scores(sa_scr, 2 * pp + 2)
        attend(sb_scr, 2 * pp + 1, False)
        return c
    lax.fori_loop(0, last // 2, tile_pair, 0)

    @pl.when(last % 2 == 0)
    def _():
        attend(sa_scr, last, True)

    @pl.when(last % 2 == 1)
    def _():
        scores(sb_scr, last)
        attend(sa_scr, last - 1, False)
        attend(sb_scr, last, True)

    ot = (osum_scr[...] + acc_scr[:LANES, :] * (gate_row(1) / acc_scr[LANES:LANES + 1, :])) + ow_scr[...]

    for g in range(GROUP):
        cols = lambda h: slice((h * GROUP + g) * Q_BLOCK, (h * GROUP + g + 1) * Q_BLOCK)
        blk = jnp.concatenate([ot[h * HEAD_DIM:(h + 1) * HEAD_DIM, cols(h)] for h in range(N_KV_HEADS)], axis=0)
        o_ref[0, g] = blk.T


def _prompt_attn(kvc, qxt, gatet, ka, vst, kw, vwt, cw, gkc, ctabs, seg, mimp, b, t):
    n_sub = t // CMP_STRIDE
    ns = t // SEL_BLOCK
    n_top = min(N_SEL, ns)
    nq = t // Q_BLOCK
    ncol = N_Q_HEADS * Q_BLOCK
    const = lambda a: pl.BlockSpec(a.shape, lambda bi, qi: (0,) * a.ndim)
    qcol = lambda n: pl.BlockSpec((n, Q_BLOCK), lambda bi, qi: (0, bi * nq + qi))
    seq_rows = lambda n: pl.BlockSpec((t, n), lambda bi, qi: (bi, 0))
    seq_cols = lambda n: pl.BlockSpec((n, t), lambda bi, qi: (0, bi))
    consts = [*cw, gkc, *ctabs, seg, mimp]
    kvc3 = kvc.reshape(b, t, KV_COLS)
    return pl.pallas_call(
        functools.partial(_prompt_attn_body, n_sub=n_sub, ns=ns, n_top=n_top),
        grid=(b, nq),
        in_specs=[pl.BlockSpec((1, t, LANES), lambda bi, qi: (bi, 0, 0)), pl.BlockSpec((1, t, LANES), lambda bi, qi: (bi, 0, 1)),
                  qcol(N_Q_HEADS * LANES), qcol(LANES), seq_rows(2 * LANES), seq_cols(V_ROWS), seq_rows(LANES), seq_cols(V_ROWS)]
                 + [const(a) for a in consts],
        out_specs=pl.BlockSpec((1, GROUP, Q_BLOCK, LANES), lambda bi, qi: (bi, 0, qi, 0)),
        out_shape=jax.ShapeDtypeStruct((b, GROUP, t, LANES), F32),
        scratch_shapes=[pltpu.VMEM((n_sub, LANES), BF16), pltpu.VMEM((LANES, n_sub), BF16),
                        pltpu.VMEM((2 * LANES, ncol), BF16), pltpu.VMEM((1, ncol), F32), pltpu.VMEM((V_ROWS, ncol), F32),
                        pltpu.VMEM((SEL_TK, ncol), F32), pltpu.VMEM((SEL_TK, ncol), F32), pltpu.VMEM((LANES, ncol), F32),
                        pltpu.VMEM((LANES, ncol), F32)],
        compiler_params=pltpu.CompilerParams(dimension_semantics=("arbitrary", "arbitrary"), vmem_limit_bytes=VMEM_LIMIT),
        name="prompt_attn",
    )(kvc3, kvc3, qxt, gatet, ka, vst, kw, vwt, *consts)


def _rglru_coeffs(xc, wra_ref, bra_ref, wrx_ref, brx_ref, lam_ref):
    rs, is_ = [], []
    for s in range(D_RNN // LANES):
        xs = xc[:, s * LANES:(s + 1) * LANES].astype(BF16)
        rs.append(_dot(xs, wra_ref[s]))
        is_.append(_dot(xs, wrx_ref[s]))
    r = _sigmoid(jnp.concatenate(rs, axis=1) + bra_ref[...])
    i = _sigmoid(jnp.concatenate(is_, axis=1) + brx_ref[...])
    z = -lam_ref[...]
    softplus = jnp.maximum(z, 0.0) + jnp.log1p(jnp.exp(-jnp.abs(z)))
    log_a = -RG_C * r * softplus
    a = jnp.exp(log_a)
    u = jnp.sqrt(-jnp.tanh(log_a) * (a * a + 1.0)) * (i * xc)
    return a, u


RNN_TC = 256


def _prompt_rnn_body(xr_ref, zr_ref, cw_ref, cb_ref, wra_ref, bra_ref, wrx_ref, brx_ref, lam_ref,
                     rb_ref, conv_ref, h_ref, xp_scr, hc_scr):
    tc = pl.program_id(1)

    @pl.when(tc == 0)
    def _():
        xp_scr[0:8, :] = jnp.zeros((8, D_RNN), F32)
        hc_scr[...] = jnp.zeros(hc_scr.shape, F32)

    x = xr_ref[...]
    xp_scr[8:8 + RNN_TC, :] = x
    xc = cb_ref[...]
    for k in range(CONV_W):
        off = 8 - (CONV_W - 1) + k
        xc = xc + xp_scr[off:off + RNN_TC, :] * cw_ref[k:k + 1, :]
    xp_scr[0:8, :] = x[RNN_TC - 8:, :]
    conv_ref[0] = x[RNN_TC - 8:, :]

    a, u = _rglru_coeffs(xc, wra_ref, bra_ref, wrx_ref, brx_ref, lam_ref)
    row = lax.broadcasted_iota(jnp.int32, (RNN_TC, 1), 0)
    d = 1
    while d < RNN_TC:
        keep = row >= d
        a_sh = jnp.where(keep, pltpu.roll(a, d, 0), 1.0)
        u_sh = jnp.where(keep, pltpu.roll(u, d, 0), 0.0)
        u = a * u_sh + u
        a = a * a_sh
        d *= 2
    h = u + a * hc_scr[...]
    hc_scr[...] = h[RNN_TC - 1:, :]
    h_ref[0] = h[RNN_TC - 1:, :]
    rb_ref[...] = (h * zr_ref[...]).astype(BF16)


def _prompt_rnn(xr, zr, rw, b, t):
    ntc = t // RNN_TC
    const = lambda a: pl.BlockSpec(a.shape, lambda bi, ti: (0,) * a.ndim)
    row = pl.BlockSpec((RNN_TC, D_RNN), lambda bi, ti: (bi * ntc + ti, 0))
    return pl.pallas_call(
        _prompt_rnn_body,
        grid=(b, ntc),
        in_specs=[row, row] + [const(a) for a in rw],
        out_specs=[row, pl.BlockSpec((1, 8, D_RNN), lambda bi, ti: (bi, 0, 0)),
                   pl.BlockSpec((1, 1, D_RNN), lambda bi, ti: (bi, 0, 0))],
        out_shape=[jax.ShapeDtypeStruct((b * t, D_RNN), BF16), jax.ShapeDtypeStruct((b, 8, D_RNN), F32),
                   jax.ShapeDtypeStruct((b, 1, D_RNN), F32)],
        scratch_shapes=[pltpu.VMEM((8 + RNN_TC, D_RNN), F32), pltpu.VMEM((1, D_RNN), F32)],
        compiler_params=pltpu.CompilerParams(dimension_semantics=("arbitrary", "arbitrary"), vmem_limit_bytes=VMEM_LIMIT),
        name="prompt_rnn",
    )(xr, zr, *rw)


def _sample_rnn_body(xr_ref, zr_ref, sc_ref, h0_ref, cw_ref, cb_ref, wra_ref, bra_ref, wrx_ref, brx_ref, lam_ref,
                     rb_ref, conv_ref, h_ref):
    x = xr_ref[...]
    xc = cb_ref[...]
    for k in range(CONV_W - 1):
        xc = xc + sc_ref[k] * cw_ref[k:k + 1, :]
    xc = xc + x * cw_ref[CONV_W - 1:CONV_W, :]
    for k in range(CONV_W - 2):
        conv_ref[k] = sc_ref[k + 1]
    conv_ref[CONV_W - 2] = x
    a, u = _rglru_coeffs(xc, wra_ref, bra_ref, wrx_ref, brx_ref, lam_ref)
    h = a * h0_ref[...] + u
    h_ref[...] = h
    rb_ref[...] = (h * zr_ref[...]).astype(BF16)


def _sample_rnn(xr, zr, sc, h0, rw):
    n = xr.shape[0]
    return pl.pallas_call(
        _sample_rnn_body,
        out_shape=[jax.ShapeDtypeStruct((n, D_RNN), BF16), jax.ShapeDtypeStruct((CONV_W - 1, n, D_RNN), F32),
                   jax.ShapeDtypeStruct((n, D_RNN), F32)],
        compiler_params=pltpu.CompilerParams(vmem_limit_bytes=VMEM_LIMIT),
        name="sample_rnn",
    )(xr, zr, sc, h0, *rw)


def _outproj_body(x_ref, *refs, n_branch):
    o_refs = refs[:n_branch]
    zn_ref, rb_ref, gm_ref, wpa_ref, wpb_ref, wo_ref, y_ref = refs[n_branch:]
    pa = None
    for g in range(GROUP):
        o = o_refs[0][0, g]
        for r in o_refs[1:]:
            o = o + r[0, g]
        a = (o * zn_ref[:, g * LANES:(g + 1) * LANES]).astype(BF16)
        term = _dot(a, wpa_ref[g])
        pa = term if pa is None else pa + term
    pb = _dot(rb_ref[...], wpb_ref[...])
    merged = gm_ref[:, :D_MODEL] * pa + gm_ref[:, D_MODEL:] * pb
    y_ref[...] = x_ref[...] + _dot(merged.astype(BF16), wo_ref[...])


def _outproj(x2d, os_, zn, rb, gm, wpa, wpb, wo, tm):
    b, _, t, _ = os_[0].shape
    nt = t // tm
    row = lambda n: pl.BlockSpec((tm, n), lambda i: (i, 0))
    oblk = pl.BlockSpec((1, GROUP, tm, LANES), lambda i: (i // nt, 0, i % nt, 0))
    const = lambda a: pl.BlockSpec(a.shape, lambda i: (0,) * a.ndim)
    return pl.pallas_call(
        functools.partial(_outproj_body, n_branch=len(os_)),
        grid=(b * nt,),
        in_specs=[row(D_MODEL)] + [oblk] * len(os_) + [row(NSA_WIDTH), row(D_RNN), row(2 * D_MODEL), const(wpa), const(wpb), const(wo)],
        out_specs=row(D_MODEL),
        out_shape=jax.ShapeDtypeStruct(x2d.shape, F32),
        compiler_params=pltpu.CompilerParams(dimension_semantics=("arbitrary",), vmem_limit_bytes=VMEM_LIMIT),
        name="outproj",
    )(x2d, *os_, zn, rb, gm, wpa, wpb, wo)


def _sample_cmp_body(pt_ref, cache_ref, qbd_ref, gate_ref, wbdk_ref, w2k_ref, pek_ref, w1tk_ref, wbdv_ref, w2v_ref, pev_ref,
                     w1tv_ref, gkc_ref, cos_ref, s1_ref, s2_ref, seg_ref, mimp_ref, perm_ref,
                     oc_ref, idx_ref, buf, sem, rk, rv, pb_scr, acck_scr, accv_scr, *, n_pages, past, ns, n_top):
    s = pl.program_id(0)
    n_seq = pl.num_programs(0) - 1
    slot = s % 2
    sub_per_page = PAGE_SIZE // CMP_STRIDE
    n_sub = n_pages * sub_per_page
    nsp = mimp_ref.shape[1]

    def page_copy(sl, seq, p):
        return pltpu.make_async_copy(cache_ref.at[pt_ref[seq * n_pages + p]], buf.at[sl, p], sem.at[sl])

    def start_all(sl, seq):
        lax.fori_loop(0, n_pages, lambda p, c: (page_copy(sl, seq, p).start(), c)[1], 0, unroll=8)

    @pl.when(s == 0)
    def _():
        start_all(0, 0)
        pb_scr[0:1, :] = _pos_bias(pek_ref, w1tk_ref)
        pb_scr[1:2, :] = _pos_bias(pev_ref, w1tv_ref)
        acck_scr[...] = jnp.zeros(acck_scr.shape, F32)
        accv_scr[...] = jnp.zeros(accv_scr.shape, F32)

    @pl.when(s + 1 < n_seq)
    def _():
        start_all(1 - slot, s + 1)

    @pl.when(s < n_seq)
    def _():
        lax.fori_loop(0, n_pages, lambda p, c: (page_copy(slot, s, p).wait(), c)[1], 0, unroll=8)

    def regroup(pp, c):
        ya = _dot_nt(perm_ref[...], buf[slot, 2 * pp].astype(BF16))
        yb = _dot_nt(perm_ref[...], buf[slot, 2 * pp + 1].astype(BF16))
        r0 = pl.multiple_of(pp * (2 * sub_per_page), 2 * sub_per_page)
        for j in range(CMP_STRIDE):
            rows = slice(j * sub_per_page, (j + 1) * sub_per_page)
            y = jnp.concatenate([ya[rows], yb[rows]], axis=0).astype(BF16)
            rk[pl.ds(r0, 2 * sub_per_page), j * LANES:(j + 1) * LANES] = y[:, :LANES]
            rv[pl.ds(r0, 2 * sub_per_page), j * LANES:(j + 1) * LANES] = y[:, LANES:]
        return c
    lax.fori_loop(0, n_pages // 2, regroup, 0, unroll=True)

    acc_k = acck_scr[...]
    acc_v = accv_scr[...]

    n_chunk = 4
    rows_c = n_sub // n_chunk

    def first_layer(c):
        rs = slice(c * rows_c, (c + 1) * rows_c)
        acck_scr[rs, :] = _dot(rk[rs, :], wbdk_ref[...])
        accv_scr[rs, :] = _dot(rv[rs, :], wbdv_ref[...])

    first_layer(0)
    kc = _compress_finish(acc_k, n_sub, w2k_ref, pb_scr[0:1, :])
    vc = _compress_finish(acc_v, n_sub, w2v_ref, pb_scr[1:2, :]).astype(BF16)
    first_layer(1)
    kc = _headnorm_rope(kc, gkc_ref[...], cos_ref[...], s1_ref[...], s2_ref[...], seg_ref[...]).astype(BF16)

    nr = N_Q_HEADS
    c_end = lax.broadcasted_iota(jnp.int32, (1, n_sub), 1) * CMP_STRIDE + (CMP_BLOCK - 1)
    p = _masked_softmax(_dot_nt(qbd_ref[0].astype(BF16), kc), jnp.broadcast_to(c_end <= past, (nr, n_sub)))
    first_layer(2)
    o = _dot(p.astype(BF16), vc)
    oc_ref[0] = _own_half(o, gate_ref[0])

    psum = jnp.concatenate([jnp.sum(p[h * GROUP:(h + 1) * GROUP], axis=0, keepdims=True) for h in range(N_KV_HEADS)]
                           + [jnp.zeros((nr - N_KV_HEADS, n_sub), F32)], axis=0)
    jl = lax.broadcasted_iota(jnp.int32, (1, nsp), 1)
    cur = past // SEL_BLOCK
    forced = (jl == 0) | (jl == cur) | (jl == cur - 1)
    v = _dot_exact01(psum, mimp_ref[...]) + jnp.where(forced, FORCE_BONUS, 0.0)
    first_layer(3)
    v = jnp.where(jl < ns, v, -1.0)
    vt = v.T
    js = lax.broadcasted_iota(jnp.int32, (nsp, 1), 0)
    rl = lax.broadcasted_iota(jnp.int32, (1, LANES), 1)
    earlier = jnp.where(jl < js, 1.0, 0.0)
    idx_rows = []
    for h in range(N_KV_HEADS):
        vrow, vcol = v[h:h + 1, :], vt[:, h:h + 1]
        before = jnp.where(vrow > vcol, 1.0, 0.0) + jnp.where(vrow == vcol, earlier, 0.0)
        rank = jnp.sum(before, axis=1, keepdims=True)
        hit = rank == rl.astype(F32)
        idx_rows.append(jnp.sum(jnp.where(hit, js, 0), axis=0, keepdims=True))
    idx_ref[0] = jnp.concatenate(idx_rows + [jnp.zeros((nr - N_KV_HEADS, LANES), jnp.int32)], axis=0)


def _sample_cmp(pt_flat, cache_t, qbd, gate_c, cw, gkc, ctabs, seg, mimp, n_seq, n_pages, past, ns, n_top):
    sub_per_page = PAGE_SIZE // CMP_STRIDE
    n_sub = n_pages * sub_per_page
    const = lambda a: pl.BlockSpec(a.shape, lambda s, pt: (0,) * a.ndim)
    seqblk = pl.BlockSpec((1, N_Q_HEADS, LANES), lambda s, pt: (jnp.maximum(s - 1, 0), 0, 0))
    tokens = np.arange(PAGE_SIZE)
    dst_row = (tokens % CMP_STRIDE) * sub_per_page + tokens // CMP_STRIDE
    perm = jnp.asarray(np.arange(PAGE_SIZE)[:, None] == dst_row[None, :], BF16)
    flat = lambda w: w.reshape(CMP_STRIDE * LANES, 2 * LANES)
    cw = (flat(cw[0]), *cw[1:4], flat(cw[4]), *cw[5:])
    consts = [*cw, gkc, *ctabs, seg, mimp, perm]
    return pl.pallas_call(
        functools.partial(_sample_cmp_body, n_pages=n_pages, past=past, ns=ns, n_top=n_top),
        grid_spec=pltpu.PrefetchScalarGridSpec(
            num_scalar_prefetch=1,
            grid=(n_seq + 1,),
            in_specs=[pl.BlockSpec(memory_space=pl.ANY), seqblk, seqblk] + [const(a) for a in consts],
            out_specs=[seqblk, seqblk],
            scratch_shapes=[pltpu.VMEM((2, n_pages, KV_COLS, PAGE_SIZE), F32), pltpu.SemaphoreType.DMA((2,)),
                            pltpu.VMEM((n_sub, CMP_STRIDE * LANES), BF16), pltpu.VMEM((n_sub, CMP_STRIDE * LANES), BF16),
                            pltpu.VMEM((8, LANES), F32), pltpu.VMEM((n_sub, 2 * LANES), F32), pltpu.VMEM((n_sub, 2 * LANES), F32)],
        ),
        out_shape=[jax.ShapeDtypeStruct((n_seq, N_Q_HEADS, LANES), F32), jax.ShapeDtypeStruct((n_seq, N_Q_HEADS, LANES), jnp.int32)],
        compiler_params=pltpu.CompilerParams(dimension_semantics=("arbitrary",), vmem_limit_bytes=VMEM_LIMIT),
        name="sample_cmp",
    )(pt_flat, cache_t, qbd, gate_c, *consts)


def _decode_attend(q, k_t, v_t, bias, s_new, v_new):
    s = _dot(q.astype(BF16), k_t)
    if bias is not None:
        s = s + bias
    m = jnp.maximum(jnp.max(s, axis=1, keepdims=True), s_new)
    e = jnp.exp(s - m)
    e_new = jnp.exp(s_new - m)
    den = jnp.sum(e, axis=1, keepdims=True) + e_new
    num = _dot_nt(e.astype(BF16), v_t) + e_new * v_new
    return num / den


def _own_half(o, gate):
    lane = lax.broadcasted_iota(jnp.int32, o.shape, 1)
    rowh = lax.broadcasted_iota(jnp.int32, o.shape, 0) // GROUP
    return jnp.where(lane // HEAD_DIM == rowh, o * _sigmoid(gate), 0.0)


def _sample_sel_body(pg_ref, meta_ref, cache_ref, qbd_ref, kvrow_ref, gate_ref, o_ref, buf, sem, *, n_top):
    s = pl.program_id(0)
    n_seq = pl.num_programs(0)
    slot = s % 2
    n_slots = N_KV_HEADS * n_top

    def copies(sl, seq, i):
        page = pg_ref[seq * n_slots + i]
        row0 = pl.multiple_of((i // n_top) * HEAD_DIM, HEAD_DIM)
        return [pltpu.make_async_copy(cache_ref.at[page, pl.ds(kv * LANES + row0, HEAD_DIM), :], buf.at[sl, i, kv], sem.at[sl])
                for kv in range(2)]

    def start_all(sl, seq):
        def body(i, c):
            for cp in copies(sl, seq, i):
                cp.start()
            return c
        lax.fori_loop(0, n_slots, body, 0, unroll=8)

    @pl.when(s == 0)
    def _():
        start_all(0, 0)

    @pl.when(s + 1 < n_seq)
    def _():
        start_all(1 - slot, s + 1)

    def wait_body(i, c):
        for cp in copies(slot, s, i):
            cp.wait()
        return c
    lax.fori_loop(0, n_slots, wait_body, 0, unroll=8)

    qbd = qbd_ref[0]
    new_row = kvrow_ref[0]
    s_new = jnp.sum(qbd * new_row[:, :LANES], axis=1, keepdims=True)
    lane = lax.broadcasted_iota(jnp.int32, (1, PAGE_SIZE), 1)
    outs = []
    for h in range(N_KV_HEADS):
        bias, k_parts, v_parts = [], [], []
        for r in range(n_top):
            lo = meta_ref[s * n_slots + h * n_top + r] * SEL_BLOCK
            bias.append(jnp.where((lane >= lo) & (lane < lo + SEL_BLOCK), 0.0, -jnp.inf))
            k_parts.append(buf[slot, h * n_top + r, 0].astype(BF16))
            v_parts.append(buf[slot, h * n_top + r, 1].astype(BF16))
        hd = slice(h * HEAD_DIM, (h + 1) * HEAD_DIM)
        outs.append(_decode_attend(qbd[:, hd], jnp.concatenate(k_parts, axis=1), jnp.concatenate(v_parts, axis=1),
                                   jnp.concatenate(bias, axis=1), s_new, new_row[:, LANES:][:, hd]))
    o_ref[0] = _own_half(jnp.concatenate(outs, axis=1), gate_ref[0])


def _sample_sel(pg, meta, cache_t, qbd, kvrow, gate, n_seq, n_top):
    seqblk = pl.BlockSpec((1, N_Q_HEADS, LANES), lambda s, pg, meta: (s, 0, 0))
    return pl.pallas_call(
        functools.partial(_sample_sel_body, n_top=n_top),
        grid_spec=pltpu.PrefetchScalarGridSpec(
            num_scalar_prefetch=2,
            grid=(n_seq,),
            in_specs=[pl.BlockSpec(memory_space=pl.ANY), seqblk, pl.BlockSpec((1, 1, KV_COLS), lambda s, pg, meta: (s, 0, 0)), seqblk],
            out_specs=seqblk,
            scratch_shapes=[pltpu.VMEM((2, N_KV_HEADS * n_top, 2, HEAD_DIM, PAGE_SIZE), F32), pltpu.SemaphoreType.DMA((2,))],
        ),
        out_shape=jax.ShapeDtypeStruct((n_seq, N_Q_HEADS, LANES), F32),
        compiler_params=pltpu.CompilerParams(dimension_semantics=("arbitrary",), vmem_limit_bytes=VMEM_LIMIT),
        name="sample_sel",
    )(pg, meta, cache_t, qbd, kvrow, gate)


WIN_SEQS = 4


def _sample_win_body(win_ref, qbd_ref, kvrow_ref, kvcol_ref, gate_ref, o_ref, wout_ref):
    lane = lax.broadcasted_iota(jnp.int32, (1, WINDOW), 1)
    for i in range(win_ref.shape[0]):
        w = win_ref[i]
        wout_ref[i] = jnp.where(lane == WINDOW - 1, kvcol_ref[i], pltpu.roll(w, WINDOW - 1, 1))
        qbd, new_row = qbd_ref[i], kvrow_ref[i]
        s_new = jnp.sum(qbd * new_row[:, :LANES], axis=1, keepdims=True)
        o = _decode_attend(qbd, w[:LANES, :].astype(BF16), w[LANES:, :].astype(BF16), None, s_new, new_row[:, LANES:])
        o_ref[i] = _own_half(o, gate_ref[i])


def _sample_win(win_t, qbd, kvrow, kvcol, gate):
    n_seq = win_t.shape[0]
    sps = WIN_SEQS if n_seq % WIN_SEQS == 0 else 1
    blk = lambda *dims: pl.BlockSpec((sps, *dims), lambda s: (s, 0, 0))
    return pl.pallas_call(
        _sample_win_body,
        grid=(n_seq // sps,),
        in_specs=[blk(KV_COLS, WINDOW), blk(N_Q_HEADS, LANES), blk(1, KV_COLS), blk(KV_COLS, 1), blk(N_Q_HEADS, LANES)],
        out_specs=[blk(N_Q_HEADS, LANES), blk(KV_COLS, WINDOW)],
        out_shape=[jax.ShapeDtypeStruct((n_seq, N_Q_HEADS, LANES), F32), jax.ShapeDtypeStruct(win_t.shape, F32)],
        compiler_params=pltpu.CompilerParams(dimension_semantics=("arbitrary",), vmem_limit_bytes=VMEM_LIMIT),
        name="sample_win",
    )(win_t, qbd, kvrow, kvcol, gate)


def _prep_inproj_weight(w_in):
    sizes = (NSA_WIDTH, 3 * KV_COLS, 3 * N_Q_HEADS, NSA_WIDTH, D_RNN, D_RNN, 2 * D_MODEL)
    offs = np.concatenate([[0], np.cumsum(sizes)])
    q, kv, gn, zn, xr, zr, gm = [w_in[:, offs[i]:offs[i + 1]] for i in range(len(sizes))]
    zn = zn.reshape(D_MODEL, N_KV_HEADS, GROUP, HEAD_DIM).transpose(0, 2, 1, 3).reshape(D_MODEL, NSA_WIDTH)
    pad = jnp.zeros((D_MODEL, LANES - 3 * N_Q_HEADS), w_in.dtype)
    return jnp.concatenate([q, kv, zn, xr, zr, gm, gn, pad], axis=1).astype(BF16)


def _prep_compress(pe, w1, w2):
    eye = jnp.eye(N_KV_HEADS, dtype=F32)
    bd = lambda w: jnp.einsum('jde,hk->jhdke', w, eye).reshape(CMP_STRIDE, LANES, LANES)
    wbd = jnp.concatenate([bd(w1[:CMP_STRIDE]), bd(w1[CMP_STRIDE:])], axis=2).astype(BF16)
    w2bd = jnp.einsum('ed,hk->hekd', w2, eye).reshape(LANES, LANES).astype(BF16)
    pe_col = pe.reshape(CMP_BLOCK * HEAD_DIM, 1)
    w1t = jnp.tile(w1.reshape(CMP_BLOCK * HEAD_DIM, CMP_HIDDEN), (1, N_KV_HEADS))
    return wbd, w2bd, pe_col, w1t


def _prep_rnn(conv_w, conv_b, w_ra, b_ra, w_rx, b_rx, lam):
    def pairs(w):
        w = w.reshape(RNN_HEADS // 2, 2, RNN_HD, RNN_HD)
        eye = jnp.eye(2, dtype=F32)
        return jnp.einsum('shij,hk->shikj', w, eye).reshape(RNN_HEADS // 2, LANES, LANES).astype(BF16)
    return (conv_w, conv_b.reshape(1, D_RNN), pairs(w_ra), b_ra.reshape(1, D_RNN), pairs(w_rx), b_rx.reshape(1, D_RNN),
            lam.reshape(1, D_RNN))


def _feature_major(cache):
    n, r = cache.shape[:2]
    return cache.transpose(0, 2, 3, 4, 1).reshape(n, KV_COLS, r)


def kernel(x_prompt, x_sample, cache_kv_cmp, cache_kv_sel, cache_kv_win, state_conv, state_h, page_table, g_norm, w_in, g_q, g_kc, g_ks, g_kw, pe_k, w1_k, w2_k, pe_v, w1_v, w2_v, conv_w, conv_b, w_ra, b_ra, w_rx, b_rx, lam, w_pa, w_pb, w_out):
    b, t, _ = x_prompt.shape
    n_seq = x_sample.shape[0]
    n_pages = page_table.shape[1]
    past = n_pages * PAGE_SIZE
    assert x_sample.shape[1] == 1 and cache_kv_win.shape[1] == WINDOW and past >= WINDOW
    assert t % SEL_TK == 0 and t // SEL_BLOCK <= HEAD_DIM and t >= WIN_KEYS and n_pages % 2 == 0

    wp = _prep_inproj_weight(w_in)
    seg = _seg01()
    tile2 = lambda g: jnp.tile(g.reshape(1, HEAD_DIM), (1, 2))
    gq, gks, gkw, gkc = tile2(g_q), tile2(g_ks), tile2(g_kw), tile2(g_kc)
    gnorm = g_norm.reshape(1, D_MODEL)
    cw = (*_prep_compress(pe_k, w1_k, w2_k), *_prep_compress(pe_v, w1_v, w2_v))
    rw = _prep_rnn(conv_w, conv_b, w_ra, b_ra, w_rx, b_rx, lam)
    wpa = w_pa.reshape(N_KV_HEADS, GROUP, HEAD_DIM, D_MODEL).transpose(1, 0, 2, 3).reshape(GROUP, LANES, D_MODEL).astype(BF16)
    wpb, wo = w_pb.astype(BF16), w_out.astype(BF16)

    tm = 512
    tok = np.arange(t)
    oh = jnp.asarray((tok[:, None] // SEL_BLOCK) == (np.arange(LANES)[None, :] % HEAD_DIM), BF16)
    xp2 = x_prompt.reshape(b * t, D_MODEL)
    (kvc, kvct, kvst, kvwt, qxt, ka, kw, vst, vwt, gatet, zn, xr, zr, gm) = _inproj(
        xp2, gnorm, wp, _rope_tables(tok), oh, gq, gks, gkw, seg, tm, t, True)
    n_sub = t // CMP_STRIDE
    ns = t // SEL_BLOCK
    mimp = jnp.asarray(_imp_matrix(n_sub, ns, n_sub - 1, ns).T, BF16)
    ctabs = _rope_tables(np.arange(n_sub) * CMP_STRIDE + (CMP_BLOCK - 1))
    o_attn = _prompt_attn(kvc, qxt, gatet, ka, vst, kw, vwt, cw, gkc, ctabs, seg, mimp, b, t)
    rb, conv_tail, h_last = _prompt_rnn(xr, zr, rw, b, t)
    y_prompt = _outproj(xp2, [o_attn], zn, rb, gm, wpa, wpb, wo, tm).reshape(b, t, D_MODEL)
    kv5t = lambda a: a.reshape(a.shape[0], 2, N_KV_HEADS, HEAD_DIM, a.shape[2]).transpose(0, 4, 1, 2, 3)
    kv_cmp_prompt, kv_sel_prompt = kv5t(kvct), kv5t(kvst)
    kv_win_prompt = kv5t(kvwt[:, :, t - min(WINDOW, t):])
    conv_prompt = conv_tail[:, 8 - (CONV_W - 1):]
    h_prompt = h_last.reshape(b, D_RNN)

    xs2 = x_sample.reshape(n_seq, D_MODEL)
    oh_s = jnp.zeros((n_seq, LANES), BF16)
    (kvc_s, kvs_s, kvw_s, qf_s, gate_s, zn_s, xr_s, zr_s, gm_s) = _inproj(
        xs2, gnorm, wp, _rope_tables(np.full((n_seq,), past)), oh_s, gq, gks, gkw, seg, n_seq, n_seq, False)
    kv5 = lambda a, rows: a.reshape(-1, rows, 2, N_KV_HEADS, HEAD_DIM)
    eye = jnp.eye(N_KV_HEADS, dtype=F32)
    q4 = qf_s.reshape(n_seq, N_KV_HEADS, GROUP, HEAD_DIM)
    qbd = (q4[:, :, :, None, :] * eye[None, :, None, :, None]).reshape(n_seq, N_Q_HEADS, LANES)
    gates3 = gate_s[:, :3 * N_Q_HEADS].reshape(n_seq, N_Q_HEADS, 3)
    gate_b = lambda k: jnp.broadcast_to(gates3[:, :, k:k + 1], (n_seq, N_Q_HEADS, LANES))

    n_sub_s = past // CMP_STRIDE
    ns_s = past // SEL_BLOCK + 1
    n_top_s = min(N_SEL, ns_s)
    nsp = -(-ns_s // LANES) * LANES
    mimp_s = jnp.asarray(_imp_matrix(n_sub_s, nsp, n_sub_s - 1, ns_s), BF16)
    ctabs_s = _rope_tables(np.arange(n_sub_s) * CMP_STRIDE + (CMP_BLOCK - 1))
    oc_s, idx = _sample_cmp(page_table.reshape(-1), _feature_major(cache_kv_cmp), qbd, gate_b(0), cw, gkc, ctabs_s, seg, mimp_s,
                            n_seq, n_pages, past, ns_s, n_top_s)
    blocks = idx[:, :N_KV_HEADS, :n_top_s]
    per_page = PAGE_SIZE // SEL_BLOCK
    in_cache = blocks < past // SEL_BLOCK
    pg = jnp.take_along_axis(page_table, jnp.minimum(blocks // per_page, n_pages - 1).reshape(n_seq, -1), axis=1)
    meta = jnp.where(in_cache, blocks % per_page, per_page).reshape(-1).astype(jnp.int32)
    o_sel_s = _sample_sel(pg.reshape(-1).astype(jnp.int32), meta, _feature_major(cache_kv_sel), qbd,
                          kvs_s.reshape(n_seq, 1, KV_COLS), gate_b(1), n_seq, n_top_s)
    o_win_s, win_new = _sample_win(_feature_major(cache_kv_win), qbd, kvw_s.reshape(n_seq, 1, KV_COLS),
                                   kvw_s.reshape(n_seq, KV_COLS, 1), gate_b(2))
    sc_t = state_conv.transpose(1, 0, 2)
    rb_s, conv_new, h_sample = _sample_rnn(xr_s, zr_s, sc_t, state_h, rw)
    to_o = lambda o: (o[:, :GROUP] + o[:, GROUP:]).transpose(1, 0, 2)[None]
    y_sample = _outproj(xs2, [to_o(oc_s), to_o(o_sel_s), to_o(o_win_s)], zn_s, rb_s, gm_s, wpa, wpb, wo, n_seq).reshape(n_seq, 1, D_MODEL)
    kv_cmp_sample, kv_sel_sample = kv5(kvc_s, 1), kv5(kvs_s, 1)
    kv_win_sample = win_new.reshape(n_seq, 2, N_KV_HEADS, HEAD_DIM, WINDOW).transpose(0, 4, 1, 2, 3)
    conv_sample = conv_new.transpose(1, 0, 2)

    return (y_prompt, y_sample, kv_cmp_prompt, kv_cmp_sample, kv_sel_prompt, kv_sel_sample,
            kv_win_prompt, kv_win_sample, conv_prompt, conv_sample, h_prompt, h_sample)
```

```python
import functools

import numpy as np
import jax
import jax.numpy as jnp
from jax import lax
from jax.experimental import pallas as pl
from jax.experimental.pallas import tpu as pltpu

D_MODEL = 1024
HEAD_DIM = 64
N_Q_HEADS = 8
N_KV_HEADS = 2
GROUP = N_Q_HEADS // N_KV_HEADS
NSA_WIDTH = N_Q_HEADS * HEAD_DIM
SCALE = HEAD_DIM ** -0.5
ROPE_DIM = HEAD_DIM // 4
ROPE_HALF = ROPE_DIM // 2
ROPE_THETA = 500000.0
CMP_BLOCK = 32
CMP_STRIDE = 16
CMP_HIDDEN = 64
SEL_BLOCK = 64
N_SEL = 16
WINDOW = 512
Q_BLOCK = 256
FORCE_BONUS = 1.0e4
D_RNN = D_MODEL // 2
RNN_HEADS = 8
RNN_HD = D_RNN // RNN_HEADS
CONV_W = 4
RG_C = 8.0
EPS = 1e-6
PAGE_SIZE = 128

LANES = 128
KV_COLS = 2 * N_KV_HEADS * HEAD_DIM
NEG = -(2.0 ** 100)
LOG2E = 1.4426950408889634
V_ROWS = LANES + 16
VMEM_LIMIT = 56 * 1024 * 1024
F32 = jnp.float32
BF16 = jnp.bfloat16

C_Q = 0
C_KV = C_Q + NSA_WIDTH
C_ZN = C_KV + 3 * KV_COLS
C_XR = C_ZN + NSA_WIDTH
C_ZR = C_XR + D_RNN
C_GM = C_ZR + D_RNN
C_GN = C_GM + 2 * D_MODEL
N_WCOLS = C_GN + LANES


def _dot(a, b):
    return jnp.dot(a, b, preferred_element_type=F32)


def _dot_nt(a, b):
    return lax.dot_general(a, b, (((1,), (1,)), ((), ())), preferred_element_type=F32)


def _split3(a):
    a1 = a.astype(BF16)
    r1 = a - a1.astype(F32)
    a2 = r1.astype(BF16)
    a3 = (r1 - a2.astype(F32)).astype(BF16)
    return a1, a2, a3


def _dot_exact01(a, b01):
    a1, a2, a3 = _split3(a)
    return _dot(a1, b01) + _dot(a2, b01) + _dot(a3, b01)


def _dot_01_exact(b01, a):
    a1, a2, a3 = _split3(a)
    return _dot(b01, a1) + _dot(b01, a2) + _dot(b01, a3)


def _sigmoid(x):
    return 1.0 / (1.0 + jnp.exp(-x))


def _silu(x):
    return x * _sigmoid(x)


def _headnorm_rope(xs, gain, cos, s1, s2, seg01):
    ss = _dot_exact01(xs * xs, seg01)
    y = xs * lax.rsqrt(ss * (1.0 / HEAD_DIM) + EPS) * gain
    return y * cos + pltpu.roll(y, ROPE_HALF, 1) * s1 + pltpu.roll(y, LANES - ROPE_HALF, 1) * s2


def _masked_softmax(s, mask):
    s = jnp.where(mask, s, -jnp.inf)
    m = jnp.max(s, axis=-1, keepdims=True)
    m = jnp.where(m == -jnp.inf, 0.0, m)
    e = jnp.where(mask, jnp.exp(s - m), 0.0)
    d = jnp.sum(e, axis=-1, keepdims=True)
    return e / jnp.where(d > 0, d, 1.0)


def _masked_softmax_t(s, mask):
    s = jnp.where(mask, s, -jnp.inf)
    m = jnp.max(s, axis=0, keepdims=True)
    m = jnp.where(m == -jnp.inf, 0.0, m)
    e = jnp.where(mask, jnp.exp2(s - m), 0.0)
    d = jnp.sum(e, axis=0, keepdims=True)
    return e * (1.0 / jnp.where(d > 0, d, 1.0))


def _rope_tables(pos):
    pos = np.asarray(pos, np.float64)
    inv = ROPE_THETA ** (-np.arange(ROPE_HALF, dtype=np.float64) / ROPE_HALF)
    ang = (pos.astype(np.float32)[:, None] * inv.astype(np.float32)[None, :]).astype(np.float32).astype(np.float64)
    cos, sin = np.cos(ang), np.sin(ang)
    n = pos.shape[0]
    c = np.ones((n, HEAD_DIM)); s1 = np.zeros((n, HEAD_DIM)); s2 = np.zeros((n, HEAD_DIM))
    c[:, :ROPE_HALF] = cos; c[:, ROPE_HALF:ROPE_DIM] = cos
    s1[:, ROPE_HALF:ROPE_DIM] = sin
    s2[:, :ROPE_HALF] = -sin
    t = lambda a: jnp.asarray(np.tile(a, (1, 2)), F32)
    return t(c), t(s1), t(s2)


def _seg01():
    lane = np.arange(LANES)
    return jnp.asarray((lane[:, None] // HEAD_DIM) == (lane[None, :] // HEAD_DIM), BF16)


def _imp_matrix(nc_pad, ns_pad, nc, ns):
    r = SEL_BLOCK // CMP_STRIDE
    lead = CMP_BLOCK // CMP_STRIDE - 1
    c = np.arange(nc_pad)[:, None]
    j = np.arange(ns_pad)[None, :]
    m = (c >= r * j - lead) & (c <= r * j + r - 1) & (c < nc) & (j < ns)
    return m


def _values_t(v, ref):
    ref[:LANES, :] = v.T.astype(BF16)
    ones_row = lax.broadcasted_iota(jnp.int32, (V_ROWS - LANES, v.shape[0]), 0) == 0
    ref[LANES:, :] = jnp.where(ones_row, 1.0, 0.0).astype(BF16)


def _inproj_body(x_ref, gn_ref, w_ref, cos_ref, s1_ref, s2_ref, oh_ref, gq_ref, gks_ref, gkw_ref, seg_ref, *out_refs, prompt):
    if prompt:
        (kvc_ref, kvct_ref, kvst_ref, kvwt_ref, qxt_ref, ka_ref, kw_ref, vst_ref, vwt_ref, gatet_ref,
         zn_ref, xr_ref, zr_ref, gm_ref) = out_refs
    else:
        kvc_ref, kvs_ref, kvw_ref, qf_ref, gate_ref, zn_ref, xr_ref, zr_ref, gm_ref = out_refs
    x = x_ref[...]
    r = lax.rsqrt(jnp.mean(x * x, axis=-1, keepdims=True) + EPS)
    u = (x * r * gn_ref[...]).astype(BF16)
    cos, s1, s2, seg = cos_ref[...], s1_ref[...], s2_ref[...], seg_ref[...]
    lane = lax.broadcasted_iota(jnp.int32, (1, LANES), 1)

    def proj(c0, n):
        return _dot(u, w_ref[:, c0:c0 + n])

    q = proj(C_Q, NSA_WIDTH)
    kv = proj(C_KV, 3 * KV_COLS)
    zn_ref[...] = _silu(proj(C_ZN, NSA_WIDTH)).astype(BF16)

    for s in range(NSA_WIDTH // LANES):
        qs = _headnorm_rope(q[:, s * LANES:(s + 1) * LANES], gq_ref[...], cos, s1, s2, seg) * SCALE
        if not prompt:
            qf_ref[:, s * LANES:(s + 1) * LANES] = qs
            continue
        qs = qs * LOG2E
        qs_sw = pltpu.roll(qs, HEAD_DIM, 1)
        for half in range(2):
            i = 2 * s + half
            h = i // GROUP
            src = qs if half == h else qs_sw
            keep = (lane >= h * HEAD_DIM) & (lane < (h + 1) * HEAD_DIM)
            qxt_ref[i * LANES:(i + 1) * LANES, :] = jnp.where(keep, src, 0.0).T.astype(BF16)

    xr_ref[...] = proj(C_XR, D_RNN)
    zr_ref[...] = _silu(proj(C_ZR, D_RNN)).astype(BF16)

    kc, vc = kv[:, :LANES], kv[:, LANES:KV_COLS]
    ks = _headnorm_rope(kv[:, KV_COLS:KV_COLS + LANES], gks_ref[...], cos, s1, s2, seg)
    vs = kv[:, KV_COLS + LANES:2 * KV_COLS]
    gm_ref[:, :D_MODEL] = _sigmoid(proj(C_GM, D_MODEL)).astype(BF16)
    kw = _headnorm_rope(kv[:, 2 * KV_COLS:2 * KV_COLS + LANES], gkw_ref[...], cos, s1, s2, seg)
    vw = kv[:, 2 * KV_COLS + LANES:]
    gm_ref[:, D_MODEL:] = _sigmoid(proj(C_GM + D_MODEL, D_MODEL)).astype(BF16)
    kvc_ref[...] = kv[:, :KV_COLS]
    if prompt:
        for ref, k, v in ((kvct_ref, kc, vc), (kvst_ref, ks, vs), (kvwt_ref, kw, vw)):
            ref[0, :LANES, :] = k.T
            ref[0, LANES:, :] = v.T
        ka_ref[:, :LANES] = ks.astype(BF16)
        ka_ref[:, LANES:] = oh_ref[...]
        kw_ref[...] = kw.astype(BF16)
        _values_t(vs, vst_ref)
        _values_t(vw, vwt_ref)
        gatet_ref[...] = proj(C_GN, LANES).T
    else:
        kvs_ref[:, :LANES] = ks
        kvs_ref[:, LANES:] = vs
        kvw_ref[:, :LANES] = kw
        kvw_ref[:, LANES:] = vw
        gate_ref[...] = proj(C_GN, LANES)


def _inproj(x2d, gnorm, wp, tabs, oh, gq, gks, gkw, seg, tm, t, prompt):
    rows = x2d.shape[0]
    nt = t // tm
    row = lambda n: pl.BlockSpec((tm, n), lambda i: (i, 0))
    col = lambda n: pl.BlockSpec((n, tm), lambda i: (0, i))
    tab = pl.BlockSpec((tm, LANES), lambda i: (i % nt, 0))
    const = lambda a: pl.BlockSpec(a.shape, lambda i: (0,) * a.ndim)
    rowshape = lambda n, dt: jax.ShapeDtypeStruct((rows, n), dt)
    colshape = lambda n, dt: jax.ShapeDtypeStruct((n, rows), dt)
    tail = [(row(NSA_WIDTH), rowshape(NSA_WIDTH, BF16)), (row(D_RNN), rowshape(D_RNN, F32)), (row(D_RNN), rowshape(D_RNN, BF16)),
            (row(2 * D_MODEL), rowshape(2 * D_MODEL, BF16))]
    if prompt:
        leaf_t = (pl.BlockSpec((1, KV_COLS, tm), lambda i: (i // nt, 0, i % nt)), jax.ShapeDtypeStruct((rows // t, KV_COLS, t), F32))
        outs = [(row(KV_COLS), rowshape(KV_COLS, F32)), leaf_t, leaf_t, leaf_t,
                (col(N_Q_HEADS * LANES), colshape(N_Q_HEADS * LANES, BF16)),
                (row(2 * LANES), rowshape(2 * LANES, BF16)), (row(LANES), rowshape(LANES, BF16)),
                (col(V_ROWS), colshape(V_ROWS, BF16)), (col(V_ROWS), colshape(V_ROWS, BF16)),
                (col(LANES), colshape(LANES, F32))] + tail
    else:
        outs = [(row(KV_COLS), rowshape(KV_COLS, F32))] * 3 + [(row(NSA_WIDTH), rowshape(NSA_WIDTH, F32)),
                                                               (row(LANES), rowshape(LANES, F32))] + tail
    return pl.pallas_call(
        functools.partial(_inproj_body, prompt=prompt),
        grid=(rows // tm,),
        in_specs=[row(D_MODEL), const(gnorm), const(wp), tab, tab, tab, tab, const(gq), const(gks), const(gkw), const(seg)],
        out_specs=[o[0] for o in outs],
        out_shape=[o[1] for o in outs],
        compiler_params=pltpu.CompilerParams(dimension_semantics=("arbitrary",), vmem_limit_bytes=VMEM_LIMIT),
        name="inproj",
    )(x2d, gnorm, wp, *tabs, oh, gq, gks, gkw, seg)


def _compress_rows(load_rows, n_sub, wbd_ref, w2_ref, pe_ref, w1t_ref):
    acc = jnp.zeros((n_sub, 2 * LANES), F32)
    for j in range(CMP_STRIDE):
        acc = acc + _dot(load_rows(j).astype(BF16), wbd_ref[j])
    return _compress_finish(acc, n_sub, w2_ref, _pos_bias(pe_ref, w1t_ref))


def _pos_bias(pe_ref, w1t_ref):
    return jnp.sum(pe_ref[...] * w1t_ref[...], axis=0, keepdims=True)


def _compress_finish(acc, n_sub, w2_ref, pos_bias):
    lo, hi = acc[:, :LANES], acc[:, LANES:]
    hid = _silu(lo + pltpu.roll(hi, n_sub - 1, 0) + pos_bias)
    return _dot(hid.astype(BF16), w2_ref[...])


SEL_TK = 512
SEL_TC = 256
N_BACK = WINDOW // Q_BLOCK
WIN_KEYS = (N_BACK + 1) * Q_BLOCK


def _prompt_attn_body(kraw_ref, vraw_ref, qxt_ref, gatet_ref, ka_ref, vst_ref, kw_ref, vwt_ref,
                      wbdk_ref, w2k_ref, pek_ref, w1tk_ref, wbdv_ref, w2v_ref, pev_ref, w1tv_ref,
                      gkc_ref, cos_ref, s1_ref, s2_ref, seg_ref, mimp_ref,
                      o_ref, kc_scr, vct_scr, qat_scr, m_scr, acc_scr, sa_scr, sb_scr, osum_scr, ow_scr, *, n_sub, ns, n_top):
    qi = pl.program_id(1)
    ncol = N_Q_HEADS * Q_BLOCK

    @pl.when(qi == 0)
    def _():
        def rows(ref):
            return lambda j: ref[0, pl.ds(j, n_sub, stride=CMP_STRIDE), :]
        kc = _compress_rows(rows(kraw_ref), n_sub, wbdk_ref, w2k_ref, pek_ref, w1tk_ref)
        kc = _headnorm_rope(kc, gkc_ref[...], cos_ref[...], s1_ref[...], s2_ref[...], seg_ref[...])
        kc_scr[...] = kc.astype(BF16)
        vct_scr[...] = _compress_rows(rows(vraw_ref), n_sub, wbdv_ref, w2v_ref, pev_ref, w1tv_ref).T.astype(BF16)

    for i in range(N_Q_HEADS):
        qat_scr[:LANES, i * Q_BLOCK:(i + 1) * Q_BLOCK] = qxt_ref[i * LANES:(i + 1) * LANES, :]
    qt = qat_scr[:LANES, :]
    qpos = qi * Q_BLOCK + lax.broadcasted_iota(jnp.int32, (1, ncol), 1) % Q_BLOCK
    gates = _sigmoid(gatet_ref[...])

    def gate_row(k):
        return jnp.concatenate([gates[i * 3 + k:i * 3 + k + 1, :] for i in range(N_Q_HEADS)], axis=1)

    w0 = pl.multiple_of(jnp.maximum(qi - N_BACK, 0) * Q_BLOCK, Q_BLOCK)
    sw = _dot(kw_ref[pl.ds(w0, WIN_KEYS), :], qt)
    r = lax.broadcasted_iota(jnp.int32, (Q_BLOCK, 1), 0)
    c = lax.broadcasted_iota(jnp.int32, (1, Q_BLOCK), 1)
    newest = jnp.where(r <= c, 0.0, -jnp.inf)
    oldest = jnp.where(c <= r, 0.0, -jnp.inf)
    biased = []
    for blk in range(N_BACK + 1):
        back = qi - (w0 // Q_BLOCK + blk)
        inner = jnp.where((back > 0) & (back < N_BACK), 0.0, -jnp.inf)
        bias = jnp.where(back == 0, newest, jnp.where(back == N_BACK, oldest, inner))
        biased.append(sw[blk * Q_BLOCK:(blk + 1) * Q_BLOCK] + jnp.concatenate([bias] * N_Q_HEADS, axis=1))
    sw = jnp.concatenate(biased, axis=0)
    e = jnp.exp2(sw - jnp.max(sw, axis=0, keepdims=True))
    ow = _dot(vwt_ref[:, pl.ds(w0, WIN_KEYS)], e.astype(BF16))
    ow_scr[...] = ow[:LANES, :] * (gate_row(2) / ow[LANES:LANES + 1, :])

    c_end = lax.broadcasted_iota(jnp.int32, (n_sub, 1), 0) * CMP_STRIDE + (CMP_BLOCK - 1)
    p = _masked_softmax_t(_dot(kc_scr[...], qt), c_end <= qpos)
    osum_scr[...] = _dot(vct_scr[...], p.astype(BF16)) * gate_row(0)

    jrow = lax.broadcasted_iota(jnp.int32, (ns, 1), 0)
    cur = (qi * Q_BLOCK + lax.broadcasted_iota(jnp.int32, (1, Q_BLOCK), 1)) // SEL_BLOCK
    forced = (jrow == 0) | (jrow == cur) | (jrow == cur - 1)
    zeros = jnp.zeros((HEAD_DIM, Q_BLOCK), F32)
    for h in range(N_KV_HEADS):
        pg = [p[:, (h * GROUP + g) * Q_BLOCK:(h * GROUP + g + 1) * Q_BLOCK] for g in range(GROUP)]
        psum = ((pg[0] + pg[1]) + pg[2]) + pg[3]
        v = _dot_01_exact(mimp_ref[...], psum) + jnp.where(forced, FORCE_BONUS, 0.0)
        groups = [v[r * 8:(r + 1) * 8, :] for r in range(ns // 8)]
        ranks = [jnp.zeros((8, Q_BLOCK), F32) for _ in groups]
        sub = lax.broadcasted_iota(jnp.int32, (8, 1), 0)
        for i in range(ns):
            vi = v[i:i + 1, :]
            for r, vg in enumerate(groups):
                if r * 8 + 7 < i:
                    cnt = jnp.where(vi > vg, 1.0, 0.0)
                elif r * 8 > i:
                    cnt = jnp.where(vi >= vg, 1.0, 0.0)
                else:
                    tie = jnp.where(sub + r * 8 > i, 1.0, 0.0)
                    cnt = jnp.where(vi > vg, 1.0, 0.0) + jnp.where(vi == vg, tie, 0.0)
                ranks[r] = ranks[r] + cnt
        neg = jnp.where(jnp.concatenate(ranks, axis=0) < n_top, 0.0, NEG)
        if ns < HEAD_DIM:
            neg = jnp.concatenate([neg, jnp.zeros((HEAD_DIM - ns, Q_BLOCK), F32)], axis=0)
        blk = jnp.concatenate([neg, zeros] if h == 0 else [zeros, neg], axis=0).astype(BF16)
        for g in range(GROUP):
            i = h * GROUP + g
            qat_scr[LANES:, i * Q_BLOCK:(i + 1) * Q_BLOCK] = blk

    m_scr[...] = jnp.full(m_scr.shape, -jnp.inf, F32)
    acc_scr[...] = jnp.zeros(acc_scr.shape, F32)
    last = (qi * Q_BLOCK) // SEL_TK

    def scores(st_ref, kt):
        k0 = pl.multiple_of(kt * SEL_TK, SEL_TK)
        st_ref[...] = _dot(ka_ref[pl.ds(k0, SEL_TK), :], qat_scr[...])

    def attend(st_ref, kt, on_diagonal):
        k0 = pl.multiple_of(kt * SEL_TK, SEL_TK)
        vt = vst_ref[:, pl.ds(k0, SEL_TK)]
        m_old = m_scr[...]
        m_news, pvs = [], []
        for c in range(ncol // SEL_TC):
            cs = slice(c * SEL_TC, (c + 1) * SEL_TC)
            st = st_ref[:, cs]
            if on_diagonal:
                kpos = k0 + lax.broadcasted_iota(jnp.int32, (SEL_TK, 1), 0)
                st = jnp.where(kpos <= qpos[:, cs], st, NEG)
            m_new = jnp.maximum(m_old[:, cs], jnp.max(st, axis=0, keepdims=True))
            pvs.append(_dot(vt, jnp.exp2(st - m_new).astype(BF16)))
            m_news.append(m_new)
        m_new = jnp.concatenate(m_news, axis=1)
        acc_scr[...] = jnp.exp2(m_old - m_new) * acc_scr[...] + jnp.concatenate(pvs, axis=1)
        m_scr[...] = m_new

    scores(sa_scr, 0)

    def tile_pair(pp, c):
        scores(sb_scr, 2 * pp + 1)
        attend(sa_scr, 2 * pp, False)
        scores(sa_scr, 2 * pp + 2)
        attend(sb_scr, 2 * pp + 1, False)
        return c
    lax.fori_loop(0, last // 2, tile_pair, 0)

    @pl.when(last % 2 == 0)
    def _():
        attend(sa_scr, last, True)

    @pl.when(last % 2 == 1)
    def _():
        scores(sb_scr, last)
        attend(sa_scr, last - 1, False)
        attend(sb_scr, last, True)

    ot = (osum_scr[...] + acc_scr[:LANES, :] * (gate_row(1) / acc_scr[LANES:LANES + 1, :])) + ow_scr[...]

    for g in range(GROUP):
        cols = lambda h: slice((h * GROUP + g) * Q_BLOCK, (h * GROUP + g + 1) * Q_BLOCK)
        blk = jnp.concatenate([ot[h * HEAD_DIM:(h + 1) * HEAD_DIM, cols(h)] for h in range(N_KV_HEADS)], axis=0)
        o_ref[0, g] = blk.T


def _prompt_attn(kvc, qxt, gatet, ka, vst, kw, vwt, cw, gkc, ctabs, seg, mimp, b, t):
    n_sub = t // CMP_STRIDE
    ns = t // SEL_BLOCK
    n_top = min(N_SEL, ns)
    nq = t // Q_BLOCK
    ncol = N_Q_HEADS * Q_BLOCK
    const = lambda a: pl.BlockSpec(a.shape, lambda bi, qi: (0,) * a.ndim)
    qcol = lambda n: pl.BlockSpec((n, Q_BLOCK), lambda bi, qi: (0, bi * nq + qi))
    seq_rows = lambda n: pl.BlockSpec((t, n), lambda bi, qi: (bi, 0))
    seq_cols = lambda n: pl.BlockSpec((n, t), lambda bi, qi: (0, bi))
    consts = [*cw, gkc, *ctabs, seg, mimp]
    kvc3 = kvc.reshape(b, t, KV_COLS)
    return pl.pallas_call(
        functools.partial(_prompt_attn_body, n_sub=n_sub, ns=ns, n_top=n_top),
        grid=(b, nq),
        in_specs=[pl.BlockSpec((1, t, LANES), lambda bi, qi: (bi, 0, 0)), pl.BlockSpec((1, t, LANES), lambda bi, qi: (bi, 0, 1)),
                  qcol(N_Q_HEADS * LANES), qcol(LANES), seq_rows(2 * LANES), seq_cols(V_ROWS), seq_rows(LANES), seq_cols(V_ROWS)]
                 + [const(a) for a in consts],
        out_specs=pl.BlockSpec((1, GROUP, Q_BLOCK, LANES), lambda bi, qi: (bi, 0, qi, 0)),
        out_shape=jax.ShapeDtypeStruct((b, GROUP, t, LANES), F32),
        scratch_shapes=[pltpu.VMEM((n_sub, LANES), BF16), pltpu.VMEM((LANES, n_sub), BF16),
                        pltpu.VMEM((2 * LANES, ncol), BF16), pltpu.VMEM((1, ncol), F32), pltpu.VMEM((V_ROWS, ncol), F32),
                        pltpu.VMEM((SEL_TK, ncol), F32), pltpu.VMEM((SEL_TK, ncol), F32), pltpu.VMEM((LANES, ncol), F32),
                        pltpu.VMEM((LANES, ncol), F32)],
        compiler_params=pltpu.CompilerParams(dimension_semantics=("arbitrary", "arbitrary"), vmem_limit_bytes=VMEM_LIMIT),
        name="prompt_attn",
    )(kvc3, kvc3, qxt, gatet, ka, vst, kw, vwt, *consts)


def _rglru_coeffs(xc, wra_ref, bra_ref, wrx_ref, brx_ref, lam_ref):
    rs, is_ = [], []
    for s in range(D_RNN // LANES):
        xs = xc[:, s * LANES:(s + 1) * LANES].astype(BF16)
        rs.append(_dot(xs, wra_ref[s]))
        is_.append(_dot(xs, wrx_ref[s]))
    r = _sigmoid(jnp.concatenate(rs, axis=1) + bra_ref[...])
    i = _sigmoid(jnp.concatenate(is_, axis=1) + brx_ref[...])
    z = -lam_ref[...]
    softplus = jnp.maximum(z, 0.0) + jnp.log1p(jnp.exp(-jnp.abs(z)))
    log_a = -RG_C * r * softplus
    a = jnp.exp(log_a)
    y = -jnp.tanh(log_a) * (a * a + 1.0)
    u = jnp.where(y > 0, y * lax.rsqrt(y), 0.0) * (i * xc)
    return a, u


RNN_TC = 256


def _prompt_rnn_body(xr_ref, zr_ref, cw_ref, cb_ref, wra_ref, bra_ref, wrx_ref, brx_ref, lam_ref,
                     rb_ref, conv_ref, h_ref, xp_scr, hc_scr):
    tc = pl.program_id(1)

    @pl.when(tc == 0)
    def _():
        xp_scr[0:8, :] = jnp.zeros((8, D_RNN), F32)
        hc_scr[...] = jnp.zeros(hc_scr.shape, F32)

    x = xr_ref[...]
    xp_scr[8:8 + RNN_TC, :] = x
    xp = xp_scr[...]
    xc = cb_ref[...]
    for k in range(CONV_W):
        back = CONV_W - 1 - k
        tap = x if back == 0 else pltpu.roll(xp, back, 0)[8:8 + RNN_TC, :]
        xc = xc + tap * cw_ref[k:k + 1, :]
    xp_scr[0:8, :] = x[RNN_TC - 8:, :]
    conv_ref[0] = x[RNN_TC - 8:, :]

    a, u = _rglru_coeffs(xc, wra_ref, bra_ref, wrx_ref, brx_ref, lam_ref)
    grp = 8
    row = lax.broadcasted_iota(jnp.int32, (RNN_TC, 1), 0) % grp
    d = 1
    while d < grp:
        keep = row >= d
        a_sh = jnp.where(keep, pltpu.roll(a, d, 0), 1.0)
        u_sh = jnp.where(keep, pltpu.roll(u, d, 0), 0.0)
        u = a * u_sh + u
        a = a * a_sh
        d *= 2
    carry = jnp.broadcast_to(hc_scr[...], (grp, D_RNN))
    for g in range(RNN_TC // grp):
        rows = slice(g * grp, (g + 1) * grp)
        h_g = u[rows] + a[rows] * carry
        rb_ref[rows, :] = (h_g * zr_ref[rows, :]).astype(BF16)
        carry = jnp.broadcast_to(h_g[grp - 1:, :], (grp, D_RNN))
    hc_scr[...] = carry[:1, :]
    h_ref[0] = carry[:1, :]


def _prompt_rnn(xr, zr, rw, b, t):
    ntc = t // RNN_TC
    const = lambda a: pl.BlockSpec(a.shape, lambda bi, ti: (0,) * a.ndim)
    row = pl.BlockSpec((RNN_TC, D_RNN), lambda bi, ti: (bi * ntc + ti, 0))
    return pl.pallas_call(
        _prompt_rnn_body,
        grid=(b, ntc),
        in_specs=[row, row] + [const(a) for a in rw],
        out_specs=[row, pl.BlockSpec((1, 8, D_RNN), lambda bi, ti: (bi, 0, 0)),
                   pl.BlockSpec((1, 1, D_RNN), lambda bi, ti: (bi, 0, 0))],
        out_shape=[jax.ShapeDtypeStruct((b * t, D_RNN), BF16), jax.ShapeDtypeStruct((b, 8, D_RNN), F32),
                   jax.ShapeDtypeStruct((b, 1, D_RNN), F32)],
        scratch_shapes=[pltpu.VMEM((8 + RNN_TC, D_RNN), F32), pltpu.VMEM((1, D_RNN), F32)],
        compiler_params=pltpu.CompilerParams(dimension_semantics=("arbitrary", "arbitrary"), vmem_limit_bytes=VMEM_LIMIT),
        name="prompt_rnn",
    )(xr, zr, *rw)


def _sample_rnn_body(xr_ref, zr_ref, sc_ref, h0_ref, cw_ref, cb_ref, wra_ref, bra_ref, wrx_ref, brx_ref, lam_ref,
                     rb_ref, conv_ref, h_ref):
    x = xr_ref[...]
    xc = cb_ref[...]
    for k in range(CONV_W - 1):
        xc = xc + sc_ref[k] * cw_ref[k:k + 1, :]
    xc = xc + x * cw_ref[CONV_W - 1:CONV_W, :]
    for k in range(CONV_W - 2):
        conv_ref[k] = sc_ref[k + 1]
    conv_ref[CONV_W - 2] = x
    a, u = _rglru_coeffs(xc, wra_ref, bra_ref, wrx_ref, brx_ref, lam_ref)
    h = a * h0_ref[...] + u
    h_ref[...] = h
    rb_ref[...] = (h * zr_ref[...]).astype(BF16)


def _sample_rnn(xr, zr, sc, h0, rw):
    n = xr.shape[0]
    return pl.pallas_call(
        _sample_rnn_body,
        out_shape=[jax.ShapeDtypeStruct((n, D_RNN), BF16), jax.ShapeDtypeStruct((CONV_W - 1, n, D_RNN), F32),
                   jax.ShapeDtypeStruct((n, D_RNN), F32)],
        compiler_params=pltpu.CompilerParams(vmem_limit_bytes=VMEM_LIMIT),
        name="sample_rnn",
    )(xr, zr, sc, h0, *rw)


def _outproj_body(x_ref, *refs, n_branch):
    o_refs = refs[:n_branch]
    zn_ref, rb_ref, gm_ref, wpa_ref, wpb_ref, wo_ref, y_ref = refs[n_branch:]
    pa = None
    for g in range(GROUP):
        o = o_refs[0][0, g]
        for r in o_refs[1:]:
            o = o + r[0, g]
        a = (o * zn_ref[:, g * LANES:(g + 1) * LANES]).astype(BF16)
        term = _dot(a, wpa_ref[g])
        pa = term if pa is None else pa + term
    pb = _dot(rb_ref[...], wpb_ref[...])
    merged = gm_ref[:, :D_MODEL] * pa + gm_ref[:, D_MODEL:] * pb
    y_ref[...] = x_ref[...] + _dot(merged.astype(BF16), wo_ref[...])


def _outproj(x2d, os_, zn, rb, gm, wpa, wpb, wo, tm):
    b, _, t, _ = os_[0].shape
    nt = t // tm
    row = lambda n: pl.BlockSpec((tm, n), lambda i: (i, 0))
    oblk = pl.BlockSpec((1, GROUP, tm, LANES), lambda i: (i // nt, 0, i % nt, 0))
    const = lambda a: pl.BlockSpec(a.shape, lambda i: (0,) * a.ndim)
    return pl.pallas_call(
        functools.partial(_outproj_body, n_branch=len(os_)),
        grid=(b * nt,),
        in_specs=[row(D_MODEL)] + [oblk] * len(os_) + [row(NSA_WIDTH), row(D_RNN), row(2 * D_MODEL), const(wpa), const(wpb), const(wo)],
        out_specs=row(D_MODEL),
        out_shape=jax.ShapeDtypeStruct(x2d.shape, F32),
        compiler_params=pltpu.CompilerParams(dimension_semantics=("arbitrary",), vmem_limit_bytes=VMEM_LIMIT),
        name="outproj",
    )(x2d, *os_, zn, rb, gm, wpa, wpb, wo)


def _sample_cmp_body(pt_ref, cache_ref, qbd_ref, gate_ref, wbdk_ref, w2k_ref, pek_ref, w1tk_ref, wbdv_ref, w2v_ref, pev_ref,
                     w1tv_ref, gkc_ref, cos_ref, s1_ref, s2_ref, seg_ref, mimp_ref, perm_ref,
                     oc_ref, idx_ref, buf, sem, rk, rv, pb_scr, acck_scr, accv_scr, *, n_pages, past, ns, n_top):
    s = pl.program_id(0)
    n_seq = pl.num_programs(0) - 1
    slot = s % 2
    sub_per_page = PAGE_SIZE // CMP_STRIDE
    n_sub = n_pages * sub_per_page
    nsp = mimp_ref.shape[1]

    def page_copy(sl, seq, p):
        return pltpu.make_async_copy(cache_ref.at[pt_ref[seq * n_pages + p]], buf.at[sl, p], sem.at[sl])

    def start_all(sl, seq):
        lax.fori_loop(0, n_pages, lambda p, c: (page_copy(sl, seq, p).start(), c)[1], 0, unroll=8)

    @pl.when(s == 0)
    def _():
        start_all(0, 0)
        pb_scr[0:1, :] = _pos_bias(pek_ref, w1tk_ref)
        pb_scr[1:2, :] = _pos_bias(pev_ref, w1tv_ref)
        acck_scr[...] = jnp.zeros(acck_scr.shape, F32)
        accv_scr[...] = jnp.zeros(accv_scr.shape, F32)

    @pl.when(s + 1 < n_seq)
    def _():
        start_all(1 - slot, s + 1)

    @pl.when(s < n_seq)
    def _():
        lax.fori_loop(0, n_pages, lambda p, c: (page_copy(slot, s, p).wait(), c)[1], 0, unroll=8)

    def regroup(pp, c):
        ya = _dot_nt(perm_ref[...], buf[slot, 2 * pp].astype(BF16))
        yb = _dot_nt(perm_ref[...], buf[slot, 2 * pp + 1].astype(BF16))
        r0 = pl.multiple_of(pp * (2 * sub_per_page), 2 * sub_per_page)
        for j in range(CMP_STRIDE):
            rows = slice(j * sub_per_page, (j + 1) * sub_per_page)
            y = jnp.concatenate([ya[rows], yb[rows]], axis=0).astype(BF16)
            rk[pl.ds(r0, 2 * sub_per_page), j * LANES:(j + 1) * LANES] = y[:, :LANES]
            rv[pl.ds(r0, 2 * sub_per_page), j * LANES:(j + 1) * LANES] = y[:, LANES:]
        return c
    lax.fori_loop(0, n_pages // 2, regroup, 0, unroll=True)

    acc_k = acck_scr[...]
    acc_v = accv_scr[...]

    n_chunk = 4
    rows_c = n_sub // n_chunk

    def first_layer(c):
        rs = slice(c * rows_c, (c + 1) * rows_c)
        acck_scr[rs, :] = _dot(rk[rs, :], wbdk_ref[...])
        accv_scr[rs, :] = _dot(rv[rs, :], wbdv_ref[...])

    first_layer(0)
    kc = _compress_finish(acc_k, n_sub, w2k_ref, pb_scr[0:1, :])
    vc = _compress_finish(acc_v, n_sub, w2v_ref, pb_scr[1:2, :]).astype(BF16)
    first_layer(1)
    kc = _headnorm_rope(kc, gkc_ref[...], cos_ref[...], s1_ref[...], s2_ref[...], seg_ref[...]).astype(BF16)

    nr = N_Q_HEADS
    c_end = lax.broadcasted_iota(jnp.int32, (1, n_sub), 1) * CMP_STRIDE + (CMP_BLOCK - 1)
    p = _masked_softmax(_dot_nt(qbd_ref[0].astype(BF16), kc), jnp.broadcast_to(c_end <= past, (nr, n_sub)))
    first_layer(2)
    o = _dot(p.astype(BF16), vc)
    oc_ref[0] = _own_half(o, gate_ref[0])

    psum = jnp.concatenate([jnp.sum(p[h * GROUP:(h + 1) * GROUP], axis=0, keepdims=True) for h in range(N_KV_HEADS)]
                           + [jnp.zeros((nr - N_KV_HEADS, n_sub), F32)], axis=0)
    jl = lax.broadcasted_iota(jnp.int32, (1, nsp), 1)
    cur = past // SEL_BLOCK
    forced = (jl == 0) | (jl == cur) | (jl == cur - 1)
    v = _dot_exact01(psum, mimp_ref[...]) + jnp.where(forced, FORCE_BONUS, 0.0)
    first_layer(3)
    v = jnp.where(jl < ns, v, -1.0)
    vt = v.T
    js = lax.broadcasted_iota(jnp.int32, (nsp, 1), 0)
    rl = lax.broadcasted_iota(jnp.int32, (1, LANES), 1)
    earlier = jnp.where(jl < js, 1.0, 0.0)
    idx_rows = []
    for h in range(N_KV_HEADS):
        vrow, vcol = v[h:h + 1, :], vt[:, h:h + 1]
        before = jnp.where(vrow > vcol, 1.0, 0.0) + jnp.where(vrow == vcol, earlier, 0.0)
        rank = jnp.sum(before, axis=1, keepdims=True)
        hit = rank == rl.astype(F32)
        idx_rows.append(jnp.sum(jnp.where(hit, js, 0), axis=0, keepdims=True))
    idx_ref[0] = jnp.concatenate(idx_rows + [jnp.zeros((nr - N_KV_HEADS, LANES), jnp.int32)], axis=0)


def _sample_cmp(pt_flat, cache_t, qbd, gate_c, cw, gkc, ctabs, seg, mimp, n_seq, n_pages, past, ns, n_top):
    sub_per_page = PAGE_SIZE // CMP_STRIDE
    n_sub = n_pages * sub_per_page
    const = lambda a: pl.BlockSpec(a.shape, lambda s, pt: (0,) * a.ndim)
    seqblk = pl.BlockSpec((1, N_Q_HEADS, LANES), lambda s, pt: (jnp.maximum(s - 1, 0), 0, 0))
    tokens = np.arange(PAGE_SIZE)
    dst_row = (tokens % CMP_STRIDE) * sub_per_page + tokens // CMP_STRIDE
    perm = jnp.asarray(np.arange(PAGE_SIZE)[:, None] == dst_row[None, :], BF16)
    flat = lambda w: w.reshape(CMP_STRIDE * LANES, 2 * LANES)
    cw = (flat(cw[0]), *cw[1:4], flat(cw[4]), *cw[5:])
    consts = [*cw, gkc, *ctabs, seg, mimp, perm]
    return pl.pallas_call(
        functools.partial(_sample_cmp_body, n_pages=n_pages, past=past, ns=ns, n_top=n_top),
        grid_spec=pltpu.PrefetchScalarGridSpec(
            num_scalar_prefetch=1,
            grid=(n_seq + 1,),
            in_specs=[pl.BlockSpec(memory_space=pl.ANY), seqblk, seqblk] + [const(a) for a in consts],
            out_specs=[seqblk, seqblk],
            scratch_shapes=[pltpu.VMEM((2, n_pages, KV_COLS, PAGE_SIZE), F32), pltpu.SemaphoreType.DMA((2,)),
                            pltpu.VMEM((n_sub, CMP_STRIDE * LANES), BF16), pltpu.VMEM((n_sub, CMP_STRIDE * LANES), BF16),
                            pltpu.VMEM((8, LANES), F32), pltpu.VMEM((n_sub, 2 * LANES), F32), pltpu.VMEM((n_sub, 2 * LANES), F32)],
        ),
        out_shape=[jax.ShapeDtypeStruct((n_seq, N_Q_HEADS, LANES), F32), jax.ShapeDtypeStruct((n_seq, N_Q_HEADS, LANES), jnp.int32)],
        compiler_params=pltpu.CompilerParams(dimension_semantics=("arbitrary",), vmem_limit_bytes=VMEM_LIMIT),
        name="sample_cmp",
    )(pt_flat, cache_t, qbd, gate_c, *consts)


def _decode_attend(q, k_t, v_t, bias, s_new, v_new):
    s = _dot(q.astype(BF16), k_t)
    if bias is not None:
        s = s + bias
    m = jnp.maximum(jnp.max(s, axis=1, keepdims=True), s_new)
    e = jnp.exp(s - m)
    e_new = jnp.exp(s_new - m)
    den = jnp.sum(e, axis=1, keepdims=True) + e_new
    num = _dot_nt(e.astype(BF16), v_t) + e_new * v_new
    return num / den


def _own_half(o, gate):
    lane = lax.broadcasted_iota(jnp.int32, o.shape, 1)
    rowh = lax.broadcasted_iota(jnp.int32, o.shape, 0) // GROUP
    return jnp.where(lane // HEAD_DIM == rowh, o * _sigmoid(gate), 0.0)


def _sample_sel_body(krow_ref, vrow_ref, meta_ref, cache_ref, qbd_ref, kvrow_ref, gate_ref, o_ref, buf, sem, *, n_top):
    s = pl.program_id(0)
    n_seq = pl.num_programs(0)
    slot = s % 2
    n_slots = N_KV_HEADS * n_top

    def copies(sl, seq, i):
        return [pltpu.make_async_copy(cache_ref.at[ref[seq * n_slots + i]], buf.at[sl, i, kv], sem.at[sl])
                for kv, ref in enumerate((krow_ref, vrow_ref))]

    def start_all(sl, seq):
        def body(i, c):
            for cp in copies(sl, seq, i):
                cp.start()
            return c
        lax.fori_loop(0, n_slots, body, 0, unroll=8)

    @pl.when(s == 0)
    def _():
        start_all(0, 0)

    @pl.when(s + 1 < n_seq)
    def _():
        start_all(1 - slot, s + 1)

    def wait_body(i, c):
        for cp in copies(slot, s, i):
            cp.wait()
        return c
    lax.fori_loop(0, n_slots, wait_body, 0, unroll=8)

    qbd = qbd_ref[0]
    new_row = kvrow_ref[0]
    s_new = jnp.sum(qbd * new_row[:, :LANES], axis=1, keepdims=True)
    lane = lax.broadcasted_iota(jnp.int32, (1, PAGE_SIZE), 1)
    outs = []
    for h in range(N_KV_HEADS):
        bias, k_parts, v_parts = [], [], []
        for r in range(n_top):
            lo = meta_ref[s * n_slots + h * n_top + r] * SEL_BLOCK
            bias.append(jnp.where((lane >= lo) & (lane < lo + SEL_BLOCK), 0.0, -jnp.inf))
            k_parts.append(buf[slot, h * n_top + r, 0].astype(BF16))
            v_parts.append(buf[slot, h * n_top + r, 1].astype(BF16))
        hd = slice(h * HEAD_DIM, (h + 1) * HEAD_DIM)
        outs.append(_decode_attend(qbd[:, hd], jnp.concatenate(k_parts, axis=1), jnp.concatenate(v_parts, axis=1),
                                   jnp.concatenate(bias, axis=1), s_new, new_row[:, LANES:][:, hd]))
    o_ref[0] = _own_half(jnp.concatenate(outs, axis=1), gate_ref[0])


def _sample_sel(pg, meta, cache_t, qbd, kvrow, gate, n_seq, n_top):
    parts = 2 * N_KV_HEADS
    cache_q = cache_t.reshape(cache_t.shape[0] * parts, HEAD_DIM, PAGE_SIZE)
    head = jnp.repeat(jnp.arange(N_KV_HEADS, dtype=jnp.int32), n_top)[None, :]
    krow = (pg * parts + head).reshape(-1).astype(jnp.int32)
    vrow = (pg * parts + N_KV_HEADS + head).reshape(-1).astype(jnp.int32)
    seqblk = pl.BlockSpec((1, N_Q_HEADS, LANES), lambda s, kr, vr, meta: (s, 0, 0))
    return pl.pallas_call(
        functools.partial(_sample_sel_body, n_top=n_top),
        grid_spec=pltpu.PrefetchScalarGridSpec(
            num_scalar_prefetch=3,
            grid=(n_seq,),
            in_specs=[pl.BlockSpec(memory_space=pl.ANY), seqblk, pl.BlockSpec((1, 1, KV_COLS), lambda s, kr, vr, meta: (s, 0, 0)), seqblk],
            out_specs=seqblk,
            scratch_shapes=[pltpu.VMEM((2, N_KV_HEADS * n_top, 2, HEAD_DIM, PAGE_SIZE), F32), pltpu.SemaphoreType.DMA((2,))],
        ),
        out_shape=jax.ShapeDtypeStruct((n_seq, N_Q_HEADS, LANES), F32),
        compiler_params=pltpu.CompilerParams(dimension_semantics=("arbitrary",), vmem_limit_bytes=VMEM_LIMIT),
        name="sample_sel",
    )(krow, vrow, meta, cache_q, qbd, kvrow, gate)


WIN_SEQS = 4


def _sample_win_body(win_ref, qbd_ref, kvrow_ref, kvcol_ref, gate_ref, o_ref, wout_ref):
    lane = lax.broadcasted_iota(jnp.int32, (1, WINDOW), 1)
    for i in range(win_ref.shape[0]):
        w = win_ref[i]
        wout_ref[i] = jnp.where(lane == WINDOW - 1, kvcol_ref[i], pltpu.roll(w, WINDOW - 1, 1))
        qbd, new_row = qbd_ref[i], kvrow_ref[i]
        s_new = jnp.sum(qbd * new_row[:, :LANES], axis=1, keepdims=True)
        o = _decode_attend(qbd, w[:LANES, :].astype(BF16), w[LANES:, :].astype(BF16), None, s_new, new_row[:, LANES:])
        o_ref[i] = _own_half(o, gate_ref[i])


def _sample_win(win_t, qbd, kvrow, kvcol, gate):
    n_seq = win_t.shape[0]
    sps = WIN_SEQS if n_seq % WIN_SEQS == 0 else 1
    blk = lambda *dims: pl.BlockSpec((sps, *dims), lambda s: (s, 0, 0))
    return pl.pallas_call(
        _sample_win_body,
        grid=(n_seq // sps,),
        in_specs=[blk(KV_COLS, WINDOW), blk(N_Q_HEADS, LANES), blk(1, KV_COLS), blk(KV_COLS, 1), blk(N_Q_HEADS, LANES)],
        out_specs=[blk(N_Q_HEADS, LANES), blk(KV_COLS, WINDOW)],
        out_shape=[jax.ShapeDtypeStruct((n_seq, N_Q_HEADS, LANES), F32), jax.ShapeDtypeStruct(win_t.shape, F32)],
        compiler_params=pltpu.CompilerParams(dimension_semantics=("arbitrary",), vmem_limit_bytes=VMEM_LIMIT),
        name="sample_win",
    )(win_t, qbd, kvrow, kvcol, gate)


def _prep_inproj_weight(w_in):
    sizes = (NSA_WIDTH, 3 * KV_COLS, 3 * N_Q_HEADS, NSA_WIDTH, D_RNN, D_RNN, 2 * D_MODEL)
    offs = np.concatenate([[0], np.cumsum(sizes)])
    w_in = w_in.astype(BF16)
    q, kv, gn, zn, xr, zr, gm = [w_in[:, offs[i]:offs[i + 1]] for i in range(len(sizes))]
    zn = zn.reshape(D_MODEL, N_KV_HEADS, GROUP, HEAD_DIM).transpose(0, 2, 1, 3).reshape(D_MODEL, NSA_WIDTH)
    pad = jnp.zeros((D_MODEL, LANES - 3 * N_Q_HEADS), BF16)
    return jnp.concatenate([q, kv, zn, xr, zr, gm, gn, pad], axis=1)


def _prep_compress(pe, w1, w2):
    eye = jnp.eye(N_KV_HEADS, dtype=F32)
    bd = lambda w: jnp.einsum('jde,hk->jhdke', w, eye).reshape(CMP_STRIDE, LANES, LANES)
    wbd = jnp.concatenate([bd(w1[:CMP_STRIDE]), bd(w1[CMP_STRIDE:])], axis=2).astype(BF16)
    w2bd = jnp.einsum('ed,hk->hekd', w2, eye).reshape(LANES, LANES).astype(BF16)
    pe_col = pe.reshape(CMP_BLOCK * HEAD_DIM, 1)
    w1t = jnp.tile(w1.reshape(CMP_BLOCK * HEAD_DIM, CMP_HIDDEN), (1, N_KV_HEADS))
    return wbd, w2bd, pe_col, w1t


def _prep_rnn(conv_w, conv_b, w_ra, b_ra, w_rx, b_rx, lam):
    def pairs(w):
        w = w.reshape(RNN_HEADS // 2, 2, RNN_HD, RNN_HD)
        eye = jnp.eye(2, dtype=F32)
        return jnp.einsum('shij,hk->shikj', w, eye).reshape(RNN_HEADS // 2, LANES, LANES).astype(BF16)
    return (conv_w, conv_b.reshape(1, D_RNN), pairs(w_ra), b_ra.reshape(1, D_RNN), pairs(w_rx), b_rx.reshape(1, D_RNN),
            lam.reshape(1, D_RNN))


def _feature_major(cache):
    n, r = cache.shape[:2]
    return cache.transpose(0, 2, 3, 4, 1).reshape(n, KV_COLS, r)


def kernel(x_prompt, x_sample, cache_kv_cmp, cache_kv_sel, cache_kv_win, state_conv, state_h, page_table, g_norm, w_in, g_q, g_kc, g_ks, g_kw, pe_k, w1_k, w2_k, pe_v, w1_v, w2_v, conv_w, conv_b, w_ra, b_ra, w_rx, b_rx, lam, w_pa, w_pb, w_out):
    b, t, _ = x_prompt.shape
    n_seq = x_sample.shape[0]
    n_pages = page_table.shape[1]
    past = n_pages * PAGE_SIZE
    assert x_sample.shape[1] == 1 and cache_kv_win.shape[1] == WINDOW and past >= WINDOW
    assert t % SEL_TK == 0 and t // SEL_BLOCK <= HEAD_DIM and t >= WIN_KEYS and n_pages % 2 == 0

    wp = _prep_inproj_weight(w_in)
    seg = _seg01()
    tile2 = lambda g: jnp.tile(g.reshape(1, HEAD_DIM), (1, 2))
    gq, gks, gkw, gkc = tile2(g_q), tile2(g_ks), tile2(g_kw), tile2(g_kc)
    gnorm = g_norm.reshape(1, D_MODEL)
    cw = (*_prep_compress(pe_k, w1_k, w2_k), *_prep_compress(pe_v, w1_v, w2_v))
    rw = _prep_rnn(conv_w, conv_b, w_ra, b_ra, w_rx, b_rx, lam)
    wpa = w_pa.reshape(N_KV_HEADS, GROUP, HEAD_DIM, D_MODEL).transpose(1, 0, 2, 3).reshape(GROUP, LANES, D_MODEL).astype(BF16)
    wpb, wo = w_pb.astype(BF16), w_out.astype(BF16)

    tm = 512
    tok = np.arange(t)
    oh = jnp.asarray((tok[:, None] // SEL_BLOCK) == (np.arange(LANES)[None, :] % HEAD_DIM), BF16)
    xp2 = x_prompt.reshape(b * t, D_MODEL)
    (kvc, kvct, kvst, kvwt, qxt, ka, kw, vst, vwt, gatet, zn, xr, zr, gm) = _inproj(
        xp2, gnorm, wp, _rope_tables(tok), oh, gq, gks, gkw, seg, tm, t, True)
    n_sub = t // CMP_STRIDE
    ns = t // SEL_BLOCK
    mimp = jnp.asarray(_imp_matrix(n_sub, ns, n_sub - 1, ns).T, BF16)
    ctabs = _rope_tables(np.arange(n_sub) * CMP_STRIDE + (CMP_BLOCK - 1))
    o_attn = _prompt_attn(kvc, qxt, gatet, ka, vst, kw, vwt, cw, gkc, ctabs, seg, mimp, b, t)
    rb, conv_tail, h_last = _prompt_rnn(xr, zr, rw, b, t)
    y_prompt = _outproj(xp2, [o_attn], zn, rb, gm, wpa, wpb, wo, tm).reshape(b, t, D_MODEL)
    kv5t = lambda a: a.reshape(a.shape[0], 2, N_KV_HEADS, HEAD_DIM, a.shape[2]).transpose(0, 4, 1, 2, 3)
    kv_cmp_prompt, kv_sel_prompt = kv5t(kvct), kv5t(kvst)
    kv_win_prompt = kv5t(kvwt[:, :, t - min(WINDOW, t):])
    conv_prompt = conv_tail[:, 8 - (CONV_W - 1):]
    h_prompt = h_last.reshape(b, D_RNN)

    xs2 = x_sample.reshape(n_seq, D_MODEL)
    oh_s = jnp.zeros((n_seq, LANES), BF16)
    (kvc_s, kvs_s, kvw_s, qf_s, gate_s, zn_s, xr_s, zr_s, gm_s) = _inproj(
        xs2, gnorm, wp, _rope_tables(np.full((n_seq,), past)), oh_s, gq, gks, gkw, seg, n_seq, n_seq, False)
    kv5 = lambda a, rows: a.reshape(-1, rows, 2, N_KV_HEADS, HEAD_DIM)
    eye = jnp.eye(N_KV_HEADS, dtype=F32)
    q4 = qf_s.reshape(n_seq, N_KV_HEADS, GROUP, HEAD_DIM)
    qbd = (q4[:, :, :, None, :] * eye[None, :, None, :, None]).reshape(n_seq, N_Q_HEADS, LANES)
    gates3 = gate_s[:, :3 * N_Q_HEADS].reshape(n_seq, N_Q_HEADS, 3)
    gate_b = lambda k: jnp.broadcast_to(gates3[:, :, k:k + 1], (n_seq, N_Q_HEADS, LANES))

    n_sub_s = past // CMP_STRIDE
    ns_s = past // SEL_BLOCK + 1
    n_top_s = min(N_SEL, ns_s)
    nsp = -(-ns_s // LANES) * LANES
    mimp_s = jnp.asarray(_imp_matrix(n_sub_s, nsp, n_sub_s - 1, ns_s), BF16)
    ctabs_s = _rope_tables(np.arange(n_sub_s) * CMP_STRIDE + (CMP_BLOCK - 1))
    oc_s, idx = _sample_cmp(page_table.reshape(-1), _feature_major(cache_kv_cmp), qbd, gate_b(0), cw, gkc, ctabs_s, seg, mimp_s,
                            n_seq, n_pages, past, ns_s, n_top_s)
    blocks = idx[:, :N_KV_HEADS, :n_top_s]
    per_page = PAGE_SIZE // SEL_BLOCK
    in_cache = blocks < past // SEL_BLOCK
    pg = jnp.take_along_axis(page_table, jnp.minimum(blocks // per_page, n_pages - 1).reshape(n_seq, -1), axis=1)
    meta = jnp.where(in_cache, blocks % per_page, per_page).reshape(-1).astype(jnp.int32)
    o_sel_s = _sample_sel(pg.astype(jnp.int32), meta, _feature_major(cache_kv_sel), qbd,
                          kvs_s.reshape(n_seq, 1, KV_COLS), gate_b(1), n_seq, n_top_s)
    o_win_s, win_new = _sample_win(_feature_major(cache_kv_win), qbd, kvw_s.reshape(n_seq, 1, KV_COLS),
                                   kvw_s.reshape(n_seq, KV_COLS, 1), gate_b(2))
    sc_t = state_conv.transpose(1, 0, 2)
    rb_s, conv_new, h_sample = _sample_rnn(xr_s, zr_s, sc_t, state_h, rw)
    to_o = lambda o: (o[:, :GROUP] + o[:, GROUP:]).transpose(1, 0, 2)[None]
    y_sample = _outproj(xs2, [to_o(oc_s), to_o(o_sel_s), to_o(o_win_s)], zn_s, rb_s, gm_s, wpa, wpb, wo, n_seq).reshape(n_seq, 1, D_MODEL)
    kv_cmp_sample, kv_sel_sample = kv5(kvc_s, 1), kv5(kvs_s, 1)
    kv_win_sample = win_new.reshape(n_seq, 2, N_KV_HEADS, HEAD_DIM, WINDOW).transpose(0, 4, 1, 2, 3)
    conv_sample = conv_new.transpose(1, 0, 2)

    return (y_prompt, y_sample, kv_cmp_prompt, kv_cmp_sample, kv_sel_prompt, kv_sel_sample,
            kv_win_prompt, kv_win_sample, conv_prompt, conv_sample, h_prompt, h_sample)
```

```python
import functools

import numpy as np
import jax
import jax.numpy as jnp
from jax import lax
from jax.experimental import pallas as pl
from jax.experimental.pallas import tpu as pltpu

D_MODEL = 1024
HEAD_DIM = 64
N_Q_HEADS = 8
N_KV_HEADS = 2
GROUP = N_Q_HEADS // N_KV_HEADS
NSA_WIDTH = N_Q_HEADS * HEAD_DIM
SCALE = HEAD_DIM ** -0.5
ROPE_DIM = HEAD_DIM // 4
ROPE_HALF = ROPE_DIM // 2
ROPE_THETA = 500000.0
CMP_BLOCK = 32
CMP_STRIDE = 16
CMP_HIDDEN = 64
SEL_BLOCK = 64
N_SEL = 16
WINDOW = 512
Q_BLOCK = 256
FORCE_BONUS = 1.0e4
D_RNN = D_MODEL // 2
RNN_HEADS = 8
RNN_HD = D_RNN // RNN_HEADS
CONV_W = 4
RG_C = 8.0
EPS = 1e-6
PAGE_SIZE = 128

LANES = 128
KV_COLS = 2 * N_KV_HEADS * HEAD_DIM
NEG = -(2.0 ** 100)
LOG2E = 1.4426950408889634
V_ROWS = LANES + 16
VMEM_LIMIT = 56 * 1024 * 1024
F32 = jnp.float32
BF16 = jnp.bfloat16

C_Q = 0
C_KV = C_Q + NSA_WIDTH
C_ZN = C_KV + 3 * KV_COLS
C_XR = C_ZN + NSA_WIDTH
C_ZR = C_XR + D_RNN
C_GM = C_ZR + D_RNN
C_GN = C_GM + 2 * D_MODEL
N_WCOLS = C_GN + LANES


def _dot(a, b):
    return jnp.dot(a, b, preferred_element_type=F32)


def _dot_nt(a, b):
    return lax.dot_general(a, b, (((1,), (1,)), ((), ())), preferred_element_type=F32)


def _split3(a):
    a1 = a.astype(BF16)
    r1 = a - a1.astype(F32)
    a2 = r1.astype(BF16)
    a3 = (r1 - a2.astype(F32)).astype(BF16)
    return a1, a2, a3


def _dot_exact01(a, b01):
    a1, a2, a3 = _split3(a)
    return _dot(a1, b01) + _dot(a2, b01) + _dot(a3, b01)


def _dot_01_exact(b01, a):
    a1, a2, a3 = _split3(a)
    return _dot(b01, a1) + _dot(b01, a2) + _dot(b01, a3)


def _sigmoid(x):
    return 1.0 / (1.0 + jnp.exp(-x))


def _silu(x):
    return x * _sigmoid(x)


def _headnorm_rope(xs, gain, cos, s1, s2, seg01):
    ss = _dot_exact01(xs * xs, seg01)
    y = xs * lax.rsqrt(ss * (1.0 / HEAD_DIM) + EPS) * gain
    return y * cos + pltpu.roll(y, ROPE_HALF, 1) * s1 + pltpu.roll(y, LANES - ROPE_HALF, 1) * s2


def _masked_softmax(s, mask):
    s = jnp.where(mask, s, -jnp.inf)
    m = jnp.max(s, axis=-1, keepdims=True)
    m = jnp.where(m == -jnp.inf, 0.0, m)
    e = jnp.where(mask, jnp.exp(s - m), 0.0)
    d = jnp.sum(e, axis=-1, keepdims=True)
    return e / jnp.where(d > 0, d, 1.0)


def _masked_softmax_t(s, mask):
    s = jnp.where(mask, s, -jnp.inf)
    m = jnp.max(s, axis=0, keepdims=True)
    m = jnp.where(m == -jnp.inf, 0.0, m)
    e = jnp.where(mask, jnp.exp2(s - m), 0.0)
    d = jnp.sum(e, axis=0, keepdims=True)
    return e * (1.0 / jnp.where(d > 0, d, 1.0))


def _rope_tables(pos):
    pos = np.asarray(pos, np.float64)
    inv = ROPE_THETA ** (-np.arange(ROPE_HALF, dtype=np.float64) / ROPE_HALF)
    ang = (pos.astype(np.float32)[:, None] * inv.astype(np.float32)[None, :]).astype(np.float32).astype(np.float64)
    cos, sin = np.cos(ang), np.sin(ang)
    n = pos.shape[0]
    c = np.ones((n, HEAD_DIM)); s1 = np.zeros((n, HEAD_DIM)); s2 = np.zeros((n, HEAD_DIM))
    c[:, :ROPE_HALF] = cos; c[:, ROPE_HALF:ROPE_DIM] = cos
    s1[:, ROPE_HALF:ROPE_DIM] = sin
    s2[:, :ROPE_HALF] = -sin
    t = lambda a: jnp.asarray(np.tile(a, (1, 2)), F32)
    return t(c), t(s1), t(s2)


def _seg01():
    lane = np.arange(LANES)
    return jnp.asarray((lane[:, None] // HEAD_DIM) == (lane[None, :] // HEAD_DIM), BF16)


def _imp_matrix(nc_pad, ns_pad, nc, ns):
    r = SEL_BLOCK // CMP_STRIDE
    lead = CMP_BLOCK // CMP_STRIDE - 1
    c = np.arange(nc_pad)[:, None]
    j = np.arange(ns_pad)[None, :]
    m = (c >= r * j - lead) & (c <= r * j + r - 1) & (c < nc) & (j < ns)
    return m


def _values_t(v, ref):
    ref[:LANES, :] = v.T.astype(BF16)
    ones_row = lax.broadcasted_iota(jnp.int32, (V_ROWS - LANES, v.shape[0]), 0) == 0
    ref[LANES:, :] = jnp.where(ones_row, 1.0, 0.0).astype(BF16)


def _inproj_body(x_ref, gn_ref, w_ref, cos_ref, s1_ref, s2_ref, oh_ref, gq_ref, gks_ref, gkw_ref, seg_ref, *out_refs, prompt):
    if prompt:
        (kvc_ref, kvct_ref, kvst_ref, kvwt_ref, qxt_ref, ka_ref, kw_ref, vst_ref, vwt_ref, gatet_ref,
         zn_ref, xr_ref, zr_ref, gm_ref) = out_refs
    else:
        kvc_ref, kvs_ref, kvw_ref, qf_ref, gate_ref, zn_ref, xr_ref, zr_ref, gm_ref = out_refs
    x = x_ref[...]
    r = lax.rsqrt(jnp.mean(x * x, axis=-1, keepdims=True) + EPS)
    u = (x * r * gn_ref[...]).astype(BF16)
    cos, s1, s2, seg = cos_ref[...], s1_ref[...], s2_ref[...], seg_ref[...]
    lane = lax.broadcasted_iota(jnp.int32, (1, LANES), 1)

    def proj(c0, n):
        return _dot(u, w_ref[:, c0:c0 + n])

    q = proj(C_Q, NSA_WIDTH)
    kv = proj(C_KV, 3 * KV_COLS)
    zn_ref[...] = _silu(proj(C_ZN, NSA_WIDTH)).astype(BF16)

    for s in range(NSA_WIDTH // LANES):
        qs = _headnorm_rope(q[:, s * LANES:(s + 1) * LANES], gq_ref[...], cos, s1, s2, seg) * SCALE
        if not prompt:
            qf_ref[:, s * LANES:(s + 1) * LANES] = qs
            continue
        qs = qs * LOG2E
        qs_sw = pltpu.roll(qs, HEAD_DIM, 1)
        for half in range(2):
            i = 2 * s + half
            h = i // GROUP
            src = qs if half == h else qs_sw
            keep = (lane >= h * HEAD_DIM) & (lane < (h + 1) * HEAD_DIM)
            qxt_ref[i * LANES:(i + 1) * LANES, :] = jnp.where(keep, src, 0.0).T.astype(BF16)

    xr_ref[...] = proj(C_XR, D_RNN)
    zr_ref[...] = _silu(proj(C_ZR, D_RNN)).astype(BF16)

    kc, vc = kv[:, :LANES], kv[:, LANES:KV_COLS]
    ks = _headnorm_rope(kv[:, KV_COLS:KV_COLS + LANES], gks_ref[...], cos, s1, s2, seg)
    vs = kv[:, KV_COLS + LANES:2 * KV_COLS]
    gm_ref[:, :D_MODEL] = _sigmoid(proj(C_GM, D_MODEL)).astype(BF16)
    kw = _headnorm_rope(kv[:, 2 * KV_COLS:2 * KV_COLS + LANES], gkw_ref[...], cos, s1, s2, seg)
    vw = kv[:, 2 * KV_COLS + LANES:]
    gm_ref[:, D_MODEL:] = _sigmoid(proj(C_GM + D_MODEL, D_MODEL)).astype(BF16)
    kvc_ref[...] = kv[:, :KV_COLS]
    if prompt:
        for ref, k, v in ((kvct_ref, kc, vc), (kvst_ref, ks, vs), (kvwt_ref, kw, vw)):
            ref[0, :LANES, :] = k.T
            ref[0, LANES:, :] = v.T
        ka_ref[:, :LANES] = ks.astype(BF16)
        ka_ref[:, LANES:] = oh_ref[...]
        kw_ref[...] = kw.astype(BF16)
        _values_t(vs, vst_ref)
        _values_t(vw, vwt_ref)
        gatet_ref[...] = proj(C_GN, LANES).T
    else:
        kvs_ref[:, :LANES] = ks
        kvs_ref[:, LANES:] = vs
        kvw_ref[:, :LANES] = kw
        kvw_ref[:, LANES:] = vw
        gate_ref[...] = proj(C_GN, LANES)


def _inproj(x2d, gnorm, wp, tabs, oh, gq, gks, gkw, seg, tm, t, prompt):
    rows = x2d.shape[0]
    nt = t // tm
    row = lambda n: pl.BlockSpec((tm, n), lambda i: (i, 0))
    col = lambda n: pl.BlockSpec((n, tm), lambda i: (0, i))
    tab = pl.BlockSpec((tm, LANES), lambda i: (i % nt, 0))
    const = lambda a: pl.BlockSpec(a.shape, lambda i: (0,) * a.ndim)
    rowshape = lambda n, dt: jax.ShapeDtypeStruct((rows, n), dt)
    colshape = lambda n, dt: jax.ShapeDtypeStruct((n, rows), dt)
    tail = [(row(NSA_WIDTH), rowshape(NSA_WIDTH, BF16)), (row(D_RNN), rowshape(D_RNN, F32)), (row(D_RNN), rowshape(D_RNN, BF16)),
            (row(2 * D_MODEL), rowshape(2 * D_MODEL, BF16))]
    if prompt:
        leaf_t = (pl.BlockSpec((1, KV_COLS, tm), lambda i: (i // nt, 0, i % nt)), jax.ShapeDtypeStruct((rows // t, KV_COLS, t), F32))
        outs = [(row(KV_COLS), rowshape(KV_COLS, F32)), leaf_t, leaf_t, leaf_t,
                (col(N_Q_HEADS * LANES), colshape(N_Q_HEADS * LANES, BF16)),
                (row(2 * LANES), rowshape(2 * LANES, BF16)), (row(LANES), rowshape(LANES, BF16)),
                (col(V_ROWS), colshape(V_ROWS, BF16)), (col(V_ROWS), colshape(V_ROWS, BF16)),
                (col(LANES), colshape(LANES, F32))] + tail
    else:
        outs = [(row(KV_COLS), rowshape(KV_COLS, F32))] * 3 + [(row(NSA_WIDTH), rowshape(NSA_WIDTH, F32)),
                                                               (row(LANES), rowshape(LANES, F32))] + tail
    return pl.pallas_call(
        functools.partial(_inproj_body, prompt=prompt),
        grid=(rows // tm,),
        in_specs=[row(D_MODEL), const(gnorm), const(wp), tab, tab, tab, tab, const(gq), const(gks), const(gkw), const(seg)],
        out_specs=[o[0] for o in outs],
        out_shape=[o[1] for o in outs],
        compiler_params=pltpu.CompilerParams(dimension_semantics=("arbitrary",), vmem_limit_bytes=VMEM_LIMIT),
        name="inproj",
    )(x2d, gnorm, wp, *tabs, oh, gq, gks, gkw, seg)


def _compress_rows(load_rows, n_sub, wbd_ref, w2_ref, pe_ref, w1t_ref):
    acc = jnp.zeros((n_sub, 2 * LANES), F32)
    for j in range(CMP_STRIDE):
        acc = acc + _dot(load_rows(j).astype(BF16), wbd_ref[j])
    return _compress_finish(acc, n_sub, w2_ref, _pos_bias(pe_ref, w1t_ref))


def _pos_bias(pe_ref, w1t_ref):
    return jnp.sum(pe_ref[...] * w1t_ref[...], axis=0, keepdims=True)


def _compress_finish(acc, n_sub, w2_ref, pos_bias):
    lo, hi = acc[:, :LANES], acc[:, LANES:]
    hid = _silu(lo + pltpu.roll(hi, n_sub - 1, 0) + pos_bias)
    return _dot(hid.astype(BF16), w2_ref[...])


SEL_TK = 512
SEL_TC = 256
N_BACK = WINDOW // Q_BLOCK
WIN_KEYS = (N_BACK + 1) * Q_BLOCK


def _prompt_attn_body(kraw_ref, vraw_ref, qxt_ref, gatet_ref, ka_ref, vst_ref, kw_ref, vwt_ref,
                      wbdk_ref, w2k_ref, pek_ref, w1tk_ref, wbdv_ref, w2v_ref, pev_ref, w1tv_ref,
                      gkc_ref, cos_ref, s1_ref, s2_ref, seg_ref, mimp_ref,
                      o_ref, kc_scr, vct_scr, qat_scr, m_scr, acc_scr, sa_scr, sb_scr, osum_scr, ow_scr, *, n_sub, ns, n_top):
    qi = pl.program_id(1)
    ncol = N_Q_HEADS * Q_BLOCK

    @pl.when(qi == 0)
    def _():
        def rows(ref):
            return lambda j: ref[0, pl.ds(j, n_sub, stride=CMP_STRIDE), :]
        kc = _compress_rows(rows(kraw_ref), n_sub, wbdk_ref, w2k_ref, pek_ref, w1tk_ref)
        kc = _headnorm_rope(kc, gkc_ref[...], cos_ref[...], s1_ref[...], s2_ref[...], seg_ref[...])
        kc_scr[...] = kc.astype(BF16)
        vct_scr[...] = _compress_rows(rows(vraw_ref), n_sub, wbdv_ref, w2v_ref, pev_ref, w1tv_ref).T.astype(BF16)

    for i in range(N_Q_HEADS):
        qat_scr[:LANES, i * Q_BLOCK:(i + 1) * Q_BLOCK] = qxt_ref[i * LANES:(i + 1) * LANES, :]
    qt = qat_scr[:LANES, :]
    qpos = qi * Q_BLOCK + lax.broadcasted_iota(jnp.int32, (1, ncol), 1) % Q_BLOCK
    gates = _sigmoid(gatet_ref[...])

    def gate_row(k):
        return jnp.concatenate([gates[i * 3 + k:i * 3 + k + 1, :] for i in range(N_Q_HEADS)], axis=1)

    w0 = pl.multiple_of(jnp.maximum(qi - N_BACK, 0) * Q_BLOCK, Q_BLOCK)
    sw = _dot(kw_ref[pl.ds(w0, WIN_KEYS), :], qt)
    r = lax.broadcasted_iota(jnp.int32, (Q_BLOCK, 1), 0)
    c = lax.broadcasted_iota(jnp.int32, (1, Q_BLOCK), 1)
    newest = jnp.where(r <= c, 0.0, -jnp.inf)
    oldest = jnp.where(c <= r, 0.0, -jnp.inf)
    biased = []
    for blk in range(N_BACK + 1):
        back = qi - (w0 // Q_BLOCK + blk)
        inner = jnp.where((back > 0) & (back < N_BACK), 0.0, -jnp.inf)
        bias = jnp.where(back == 0, newest, jnp.where(back == N_BACK, oldest, inner))
        biased.append(sw[blk * Q_BLOCK:(blk + 1) * Q_BLOCK] + jnp.concatenate([bias] * N_Q_HEADS, axis=1))
    sw = jnp.concatenate(biased, axis=0)
    e = jnp.exp2(sw - jnp.max(sw, axis=0, keepdims=True))
    ow = _dot(vwt_ref[:, pl.ds(w0, WIN_KEYS)], e.astype(BF16))
    ow_scr[...] = ow[:LANES, :] * (gate_row(2) / ow[LANES:LANES + 1, :])

    c_end = lax.broadcasted_iota(jnp.int32, (n_sub, 1), 0) * CMP_STRIDE + (CMP_BLOCK - 1)
    p = _masked_softmax_t(_dot(kc_scr[...], qt), c_end <= qpos)
    osum_scr[...] = _dot(vct_scr[...], p.astype(BF16)) * gate_row(0)

    jrow = lax.broadcasted_iota(jnp.int32, (ns, 1), 0)
    cur = (qi * Q_BLOCK + lax.broadcasted_iota(jnp.int32, (1, Q_BLOCK), 1)) // SEL_BLOCK
    forced = (jrow == 0) | (jrow == cur) | (jrow == cur - 1)
    zeros = jnp.zeros((HEAD_DIM, Q_BLOCK), F32)
    for h in range(N_KV_HEADS):
        pg = [p[:, (h * GROUP + g) * Q_BLOCK:(h * GROUP + g + 1) * Q_BLOCK] for g in range(GROUP)]
        psum = ((pg[0] + pg[1]) + pg[2]) + pg[3]
        v = _dot_01_exact(mimp_ref[...], psum) + jnp.where(forced, FORCE_BONUS, 0.0)
        groups = [v[r * 8:(r + 1) * 8, :] for r in range(ns // 8)]
        ranks = [jnp.zeros((8, Q_BLOCK), F32) for _ in groups]
        sub = lax.broadcasted_iota(jnp.int32, (8, 1), 0)
        for i in range(ns):
            vi = v[i:i + 1, :]
            for r, vg in enumerate(groups):
                if r * 8 + 7 < i:
                    cnt = jnp.where(vi > vg, 1.0, 0.0)
                elif r * 8 > i:
                    cnt = jnp.where(vi >= vg, 1.0, 0.0)
                else:
                    tie = jnp.where(sub + r * 8 > i, 1.0, 0.0)
                    cnt = jnp.where(vi > vg, 1.0, 0.0) + jnp.where(vi == vg, tie, 0.0)
                ranks[r] = ranks[r] + cnt
        neg = jnp.where(jnp.concatenate(ranks, axis=0) < n_top, 0.0, NEG)
        if ns < HEAD_DIM:
            neg = jnp.concatenate([neg, jnp.zeros((HEAD_DIM - ns, Q_BLOCK), F32)], axis=0)
        blk = jnp.concatenate([neg, zeros] if h == 0 else [zeros, neg], axis=0).astype(BF16)
        for g in range(GROUP):
            i = h * GROUP + g
            qat_scr[LANES:, i * Q_BLOCK:(i + 1) * Q_BLOCK] = blk

    m_scr[...] = jnp.full(m_scr.shape, -jnp.inf, F32)
    acc_scr[...] = jnp.zeros(acc_scr.shape, F32)
    last = (qi * Q_BLOCK) // SEL_TK

    def scores(st_ref, kt):
        k0 = pl.multiple_of(kt * SEL_TK, SEL_TK)
        st_ref[...] = _dot(ka_ref[pl.ds(k0, SEL_TK), :], qat_scr[...])

    def attend(st_ref, kt, on_diagonal):
        k0 = pl.multiple_of(kt * SEL_TK, SEL_TK)
        vt = vst_ref[:, pl.ds(k0, SEL_TK)]
        m_old = m_scr[...]
        m_news, pvs = [], []
        for c in range(ncol // SEL_TC):
            cs = slice(c * SEL_TC, (c + 1) * SEL_TC)
            st = st_ref[:, cs]
            if on_diagonal:
                kpos = k0 + lax.broadcasted_iota(jnp.int32, (SEL_TK, 1), 0)
                st = jnp.where(kpos <= qpos[:, cs], st, NEG)
            m_new = jnp.maximum(m_old[:, cs], jnp.max(st, axis=0, keepdims=True))
            pvs.append(_dot(vt, jnp.exp2(st - m_new).astype(BF16)))
            m_news.append(m_new)
        m_new = jnp.concatenate(m_news, axis=1)
        acc_scr[...] = jnp.exp2(m_old - m_new) * acc_scr[...] + jnp.concatenate(pvs, axis=1)
        m_scr[...] = m_new

    scores(sa_scr, 0)

    def tile_pair(pp, c):
        scores(sb_scr, 2 * pp + 1)
        attend(sa_scr, 2 * pp, False)
        scores(sa_scr, 2 * pp + 2)
        attend(sb_scr, 2 * pp + 1, False)
        return c
    lax.fori_loop(0, last // 2, tile_pair, 0)

    @pl.when(last % 2 == 0)
    def _():
        attend(sa_scr, last, True)

    @pl.when(last % 2 == 1)
    def _():
        scores(sb_scr, last)
        attend(sa_scr, last - 1, False)
        attend(sb_scr, last, True)

    ot = (osum_scr[...] + acc_scr[:LANES, :] * (gate_row(1) / acc_scr[LANES:LANES + 1, :])) + ow_scr[...]

    for g in range(GROUP):
        cols = lambda h: slice((h * GROUP + g) * Q_BLOCK, (h * GROUP + g + 1) * Q_BLOCK)
        blk = jnp.concatenate([ot[h * HEAD_DIM:(h + 1) * HEAD_DIM, cols(h)] for h in range(N_KV_HEADS)], axis=0)
        o_ref[0, g] = blk.T


def _prompt_attn(kvc, qxt, gatet, ka, vst, kw, vwt, cw, gkc, ctabs, seg, mimp, b, t):
    n_sub = t // CMP_STRIDE
    ns = t // SEL_BLOCK
    n_top = min(N_SEL, ns)
    nq = t // Q_BLOCK
    ncol = N_Q_HEADS * Q_BLOCK
    const = lambda a: pl.BlockSpec(a.shape, lambda bi, qi: (0,) * a.ndim)
    qcol = lambda n: pl.BlockSpec((n, Q_BLOCK), lambda bi, qi: (0, bi * nq + qi))
    seq_rows = lambda n: pl.BlockSpec((t, n), lambda bi, qi: (bi, 0))
    seq_cols = lambda n: pl.BlockSpec((n, t), lambda bi, qi: (0, bi))
    consts = [*cw, gkc, *ctabs, seg, mimp]
    kvc3 = kvc.reshape(b, t, KV_COLS)
    return pl.pallas_call(
        functools.partial(_prompt_attn_body, n_sub=n_sub, ns=ns, n_top=n_top),
        grid=(b, nq),
        in_specs=[pl.BlockSpec((1, t, LANES), lambda bi, qi: (bi, 0, 0)), pl.BlockSpec((1, t, LANES), lambda bi, qi: (bi, 0, 1)),
                  qcol(N_Q_HEADS * LANES), qcol(LANES), seq_rows(2 * LANES), seq_cols(V_ROWS), seq_rows(LANES), seq_cols(V_ROWS)]
                 + [const(a) for a in consts],
        out_specs=pl.BlockSpec((1, GROUP, Q_BLOCK, LANES), lambda bi, qi: (bi, 0, qi, 0)),
        out_shape=jax.ShapeDtypeStruct((b, GROUP, t, LANES), F32),
        scratch_shapes=[pltpu.VMEM((n_sub, LANES), BF16), pltpu.VMEM((LANES, n_sub), BF16),
                        pltpu.VMEM((2 * LANES, ncol), BF16), pltpu.VMEM((1, ncol), F32), pltpu.VMEM((V_ROWS, ncol), F32),
                        pltpu.VMEM((SEL_TK, ncol), F32), pltpu.VMEM((SEL_TK, ncol), F32), pltpu.VMEM((LANES, ncol), F32),
                        pltpu.VMEM((LANES, ncol), F32)],
        compiler_params=pltpu.CompilerParams(dimension_semantics=("arbitrary", "arbitrary"), vmem_limit_bytes=VMEM_LIMIT),
        name="prompt_attn",
    )(kvc3, kvc3, qxt, gatet, ka, vst, kw, vwt, *consts)


def _rglru_coeffs(xc, wra_ref, bra_ref, wrx_ref, brx_ref, lam_ref):
    rs, is_ = [], []
    for s in range(D_RNN // LANES):
        xs = xc[:, s * LANES:(s + 1) * LANES].astype(BF16)
        rs.append(_dot(xs, wra_ref[s]))
        is_.append(_dot(xs, wrx_ref[s]))
    r = _sigmoid(jnp.concatenate(rs, axis=1) + bra_ref[...])
    i = _sigmoid(jnp.concatenate(is_, axis=1) + brx_ref[...])
    z = -lam_ref[...]
    softplus = jnp.maximum(z, 0.0) + jnp.log1p(jnp.exp(-jnp.abs(z)))
    log_a = -RG_C * r * softplus
    a = jnp.exp(log_a)
    y = -jnp.tanh(log_a) * (a * a + 1.0)
    u = jnp.where(y > 0, y * lax.rsqrt(y), 0.0) * (i * xc)
    return a, u


RNN_TC = 256


def _prompt_rnn_body(xr_ref, zr_ref, cw_ref, cb_ref, wra_ref, bra_ref, wrx_ref, brx_ref, lam_ref,
                     rb_ref, conv_ref, h_ref, xp_scr, hc_scr):
    tc = pl.program_id(1)

    @pl.when(tc == 0)
    def _():
        xp_scr[0:8, :] = jnp.zeros((8, D_RNN), F32)
        hc_scr[...] = jnp.zeros(hc_scr.shape, F32)

    x = xr_ref[...]
    xp_scr[8:8 + RNN_TC, :] = x
    xp = xp_scr[...]
    xc = cb_ref[...]
    for k in range(CONV_W):
        back = CONV_W - 1 - k
        tap = x if back == 0 else pltpu.roll(xp, back, 0)[8:8 + RNN_TC, :]
        xc = xc + tap * cw_ref[k:k + 1, :]
    xp_scr[0:8, :] = x[RNN_TC - 8:, :]
    conv_ref[0] = x[RNN_TC - 8:, :]

    a, u = _rglru_coeffs(xc, wra_ref, bra_ref, wrx_ref, brx_ref, lam_ref)
    grp = 8
    row = lax.broadcasted_iota(jnp.int32, (RNN_TC, 1), 0) % grp
    d = 1
    while d < grp:
        keep = row >= d
        a_sh = jnp.where(keep, pltpu.roll(a, d, 0), 1.0)
        u_sh = jnp.where(keep, pltpu.roll(u, d, 0), 0.0)
        u = a * u_sh + u
        a = a * a_sh
        d *= 2
    carry = jnp.broadcast_to(hc_scr[...], (grp, D_RNN))
    for g in range(RNN_TC // grp):
        rows = slice(g * grp, (g + 1) * grp)
        h_g = u[rows] + a[rows] * carry
        rb_ref[rows, :] = (h_g * zr_ref[rows, :]).astype(BF16)
        carry = jnp.broadcast_to(h_g[grp - 1:, :], (grp, D_RNN))
    hc_scr[...] = carry[:1, :]
    h_ref[0] = carry[:1, :]


def _prompt_rnn(xr, zr, rw, b, t):
    ntc = t // RNN_TC
    const = lambda a: pl.BlockSpec(a.shape, lambda bi, ti: (0,) * a.ndim)
    row = pl.BlockSpec((RNN_TC, D_RNN), lambda bi, ti: (bi * ntc + ti, 0))
    return pl.pallas_call(
        _prompt_rnn_body,
        grid=(b, ntc),
        in_specs=[row, row] + [const(a) for a in rw],
        out_specs=[row, pl.BlockSpec((1, 8, D_RNN), lambda bi, ti: (bi, 0, 0)),
                   pl.BlockSpec((1, 1, D_RNN), lambda bi, ti: (bi, 0, 0))],
        out_shape=[jax.ShapeDtypeStruct((b * t, D_RNN), BF16), jax.ShapeDtypeStruct((b, 8, D_RNN), F32),
                   jax.ShapeDtypeStruct((b, 1, D_RNN), F32)],
        scratch_shapes=[pltpu.VMEM((8 + RNN_TC, D_RNN), F32), pltpu.VMEM((1, D_RNN), F32)],
        compiler_params=pltpu.CompilerParams(dimension_semantics=("arbitrary", "arbitrary"), vmem_limit_bytes=VMEM_LIMIT),
        name="prompt_rnn",
    )(xr, zr, *rw)


def _sample_rnn_body(xr_ref, zr_ref, sc_ref, h0_ref, cw_ref, cb_ref, wra_ref, bra_ref, wrx_ref, brx_ref, lam_ref,
                     rb_ref, conv_ref, h_ref):
    x = xr_ref[...]
    xc = cb_ref[...]
    for k in range(CONV_W - 1):
        xc = xc + sc_ref[k] * cw_ref[k:k + 1, :]
    xc = xc + x * cw_ref[CONV_W - 1:CONV_W, :]
    for k in range(CONV_W - 2):
        conv_ref[k] = sc_ref[k + 1]
    conv_ref[CONV_W - 2] = x
    a, u = _rglru_coeffs(xc, wra_ref, bra_ref, wrx_ref, brx_ref, lam_ref)
    h = a * h0_ref[...] + u
    h_ref[...] = h
    rb_ref[...] = (h * zr_ref[...]).astype(BF16)


def _sample_rnn(xr, zr, sc, h0, rw):
    n = xr.shape[0]
    return pl.pallas_call(
        _sample_rnn_body,
        out_shape=[jax.ShapeDtypeStruct((n, D_RNN), BF16), jax.ShapeDtypeStruct((CONV_W - 1, n, D_RNN), F32),
                   jax.ShapeDtypeStruct((n, D_RNN), F32)],
        compiler_params=pltpu.CompilerParams(vmem_limit_bytes=VMEM_LIMIT),
        name="sample_rnn",
    )(xr, zr, sc, h0, *rw)


def _outproj_body(x_ref, *refs, n_branch):
    o_refs = refs[:n_branch]
    zn_ref, rb_ref, gm_ref, wpa_ref, wpb_ref, wo_ref, y_ref = refs[n_branch:]
    pa = None
    for g in range(GROUP):
        o = o_refs[0][0, g]
        for r in o_refs[1:]:
            o = o + r[0, g]
        a = (o * zn_ref[:, g * LANES:(g + 1) * LANES]).astype(BF16)
        term = _dot(a, wpa_ref[g])
        pa = term if pa is None else pa + term
    pb = _dot(rb_ref[...], wpb_ref[...])
    merged = gm_ref[:, :D_MODEL] * pa + gm_ref[:, D_MODEL:] * pb
    y_ref[...] = x_ref[...] + _dot(merged.astype(BF16), wo_ref[...])


def _outproj(x2d, os_, zn, rb, gm, wpa, wpb, wo, tm):
    b, _, t, _ = os_[0].shape
    nt = t // tm
    row = lambda n: pl.BlockSpec((tm, n), lambda i: (i, 0))
    oblk = pl.BlockSpec((1, GROUP, tm, LANES), lambda i: (i // nt, 0, i % nt, 0))
    const = lambda a: pl.BlockSpec(a.shape, lambda i: (0,) * a.ndim)
    return pl.pallas_call(
        functools.partial(_outproj_body, n_branch=len(os_)),
        grid=(b * nt,),
        in_specs=[row(D_MODEL)] + [oblk] * len(os_) + [row(NSA_WIDTH), row(D_RNN), row(2 * D_MODEL), const(wpa), const(wpb), const(wo)],
        out_specs=row(D_MODEL),
        out_shape=jax.ShapeDtypeStruct(x2d.shape, F32),
        compiler_params=pltpu.CompilerParams(dimension_semantics=("arbitrary",), vmem_limit_bytes=VMEM_LIMIT),
        name="outproj",
    )(x2d, *os_, zn, rb, gm, wpa, wpb, wo)


def _sample_cmp_body(pt_ref, cache_ref, qbd_ref, gate_ref, wbdk_ref, w2k_ref, pek_ref, w1tk_ref, wbdv_ref, w2v_ref, pev_ref,
                     w1tv_ref, gkc_ref, cos_ref, s1_ref, s2_ref, seg_ref, mimp_ref, perm_ref,
                     oc_ref, idx_ref, buf, sem, rk, rv, pb_scr, acck_scr, accv_scr, *, n_pages, past, ns, n_top):
    s = pl.program_id(0)
    n_seq = pl.num_programs(0) - 1
    slot = s % 2
    sub_per_page = PAGE_SIZE // CMP_STRIDE
    n_sub = n_pages * sub_per_page
    nsp = mimp_ref.shape[1]

    def page_copy(sl, seq, p):
        return pltpu.make_async_copy(cache_ref.at[pt_ref[seq * n_pages + p]], buf.at[sl, p], sem.at[sl])

    def start_all(sl, seq):
        lax.fori_loop(0, n_pages, lambda p, c: (page_copy(sl, seq, p).start(), c)[1], 0, unroll=8)

    @pl.when(s == 0)
    def _():
        start_all(0, 0)
        pb_scr[0:1, :] = _pos_bias(pek_ref, w1tk_ref)
        pb_scr[1:2, :] = _pos_bias(pev_ref, w1tv_ref)
        acck_scr[...] = jnp.zeros(acck_scr.shape, F32)
        accv_scr[...] = jnp.zeros(accv_scr.shape, F32)

    @pl.when(s + 1 < n_seq)
    def _():
        start_all(1 - slot, s + 1)

    @pl.when(s < n_seq)
    def _():
        lax.fori_loop(0, n_pages, lambda p, c: (page_copy(slot, s, p).wait(), c)[1], 0, unroll=8)

    def regroup(pp, c):
        ya = _dot_nt(perm_ref[...], buf[slot, 2 * pp].astype(BF16))
        yb = _dot_nt(perm_ref[...], buf[slot, 2 * pp + 1].astype(BF16))
        r0 = pl.multiple_of(pp * (2 * sub_per_page), 2 * sub_per_page)
        for j in range(CMP_STRIDE):
            rows = slice(j * sub_per_page, (j + 1) * sub_per_page)
            y = jnp.concatenate([ya[rows], yb[rows]], axis=0).astype(BF16)
            rk[pl.ds(r0, 2 * sub_per_page), j * LANES:(j + 1) * LANES] = y[:, :LANES]
            rv[pl.ds(r0, 2 * sub_per_page), j * LANES:(j + 1) * LANES] = y[:, LANES:]
        return c
    lax.fori_loop(0, n_pages // 2, regroup, 0, unroll=True)

    acc_k = acck_scr[...]
    acc_v = accv_scr[...]

    n_chunk = 4
    rows_c = n_sub // n_chunk

    def first_layer(c):
        rs = slice(c * rows_c, (c + 1) * rows_c)
        acck_scr[rs, :] = _dot(rk[rs, :], wbdk_ref[...])
        accv_scr[rs, :] = _dot(rv[rs, :], wbdv_ref[...])

    first_layer(0)
    kc = _compress_finish(acc_k, n_sub, w2k_ref, pb_scr[0:1, :])
    vc = _compress_finish(acc_v, n_sub, w2v_ref, pb_scr[1:2, :]).astype(BF16)
    first_layer(1)
    kc = _headnorm_rope(kc, gkc_ref[...], cos_ref[...], s1_ref[...], s2_ref[...], seg_ref[...]).astype(BF16)

    nr = N_Q_HEADS
    c_end = lax.broadcasted_iota(jnp.int32, (1, n_sub), 1) * CMP_STRIDE + (CMP_BLOCK - 1)
    p = _masked_softmax(_dot_nt(qbd_ref[0].astype(BF16), kc), jnp.broadcast_to(c_end <= past, (nr, n_sub)))
    first_layer(2)
    o = _dot(p.astype(BF16), vc)
    oc_ref[0] = _own_half(o, gate_ref[0])

    psum = jnp.concatenate([jnp.sum(p[h * GROUP:(h + 1) * GROUP], axis=0, keepdims=True) for h in range(N_KV_HEADS)]
                           + [jnp.zeros((nr - N_KV_HEADS, n_sub), F32)], axis=0)
    jl = lax.broadcasted_iota(jnp.int32, (1, nsp), 1)
    cur = past // SEL_BLOCK
    forced = (jl == 0) | (jl == cur) | (jl == cur - 1)
    v = _dot_exact01(psum, mimp_ref[...]) + jnp.where(forced, FORCE_BONUS, 0.0)
    first_layer(3)
    v = jnp.where(jl < ns, v, -1.0)
    vt = v.T
    js = lax.broadcasted_iota(jnp.int32, (nsp, 1), 0)
    rl = lax.broadcasted_iota(jnp.int32, (1, LANES), 1)
    earlier = jnp.where(jl < js, 1.0, 0.0)
    idx_rows = []
    for h in range(N_KV_HEADS):
        vrow, vcol = v[h:h + 1, :], vt[:, h:h + 1]
        before = jnp.where(vrow > vcol, 1.0, 0.0) + jnp.where(vrow == vcol, earlier, 0.0)
        rank = jnp.sum(before, axis=1, keepdims=True)
        hit = rank == rl.astype(F32)
        idx_rows.append(jnp.sum(jnp.where(hit, js, 0), axis=0, keepdims=True))
    idx_ref[0] = jnp.concatenate(idx_rows + [jnp.zeros((nr - N_KV_HEADS, LANES), jnp.int32)], axis=0)


def _sample_cmp(pt_flat, cache_t, qbd, gate_c, cw, gkc, ctabs, seg, mimp, n_seq, n_pages, past, ns, n_top):
    sub_per_page = PAGE_SIZE // CMP_STRIDE
    n_sub = n_pages * sub_per_page
    const = lambda a: pl.BlockSpec(a.shape, lambda s, pt: (0,) * a.ndim)
    seqblk = pl.BlockSpec((1, N_Q_HEADS, LANES), lambda s, pt: (jnp.maximum(s - 1, 0), 0, 0))
    tokens = np.arange(PAGE_SIZE)
    dst_row = (tokens % CMP_STRIDE) * sub_per_page + tokens // CMP_STRIDE
    perm = jnp.asarray(np.arange(PAGE_SIZE)[:, None] == dst_row[None, :], BF16)
    flat = lambda w: w.reshape(CMP_STRIDE * LANES, 2 * LANES)
    cw = (flat(cw[0]), *cw[1:4], flat(cw[4]), *cw[5:])
    consts = [*cw, gkc, *ctabs, seg, mimp, perm]
    return pl.pallas_call(
        functools.partial(_sample_cmp_body, n_pages=n_pages, past=past, ns=ns, n_top=n_top),
        grid_spec=pltpu.PrefetchScalarGridSpec(
            num_scalar_prefetch=1,
            grid=(n_seq + 1,),
            in_specs=[pl.BlockSpec(memory_space=pl.ANY), seqblk, seqblk] + [const(a) for a in consts],
            out_specs=[seqblk, seqblk],
            scratch_shapes=[pltpu.VMEM((2, n_pages, KV_COLS, PAGE_SIZE), F32), pltpu.SemaphoreType.DMA((2,)),
                            pltpu.VMEM((n_sub, CMP_STRIDE * LANES), BF16), pltpu.VMEM((n_sub, CMP_STRIDE * LANES), BF16),
                            pltpu.VMEM((8, LANES), F32), pltpu.VMEM((n_sub, 2 * LANES), F32), pltpu.VMEM((n_sub, 2 * LANES), F32)],
        ),
        out_shape=[jax.ShapeDtypeStruct((n_seq, N_Q_HEADS, LANES), F32), jax.ShapeDtypeStruct((n_seq, N_Q_HEADS, LANES), jnp.int32)],
        compiler_params=pltpu.CompilerParams(dimension_semantics=("arbitrary",), vmem_limit_bytes=VMEM_LIMIT),
        name="sample_cmp",
    )(pt_flat, cache_t, qbd, gate_c, *consts)


def _decode_attend(q, k_t, v_t, bias, s_new, v_new):
    s = _dot(q.astype(BF16), k_t)
    if bias is not None:
        s = s + bias
    m = jnp.maximum(jnp.max(s, axis=1, keepdims=True), s_new)
    e = jnp.exp(s - m)
    e_new = jnp.exp(s_new - m)
    den = jnp.sum(e, axis=1, keepdims=True) + e_new
    num = _dot_nt(e.astype(BF16), v_t) + e_new * v_new
    return num / den


def _own_half(o, gate):
    lane = lax.broadcasted_iota(jnp.int32, o.shape, 1)
    rowh = lax.broadcasted_iota(jnp.int32, o.shape, 0) // GROUP
    return jnp.where(lane // HEAD_DIM == rowh, o * _sigmoid(gate), 0.0)


SEL_SEQS = 4


def _sample_sel_body(krow_ref, vrow_ref, meta_ref, cache_ref, qbd_ref, kvrow_ref, gate_ref, o_ref, buf, sem, *, n_top, sps):
    s = pl.program_id(0)
    n_steps = pl.num_programs(0)
    slot = s % 2
    n_slots = N_KV_HEADS * n_top
    n_cp = sps * n_slots

    def copies(sl, step, i):
        return [pltpu.make_async_copy(cache_ref.at[ref[step * n_cp + i]], buf.at[sl, i, kv], sem.at[sl])
                for kv, ref in enumerate((krow_ref, vrow_ref))]

    def start_all(sl, step):
        def body(i, c):
            for cp in copies(sl, step, i):
                cp.start()
            return c
        lax.fori_loop(0, n_cp, body, 0, unroll=8)

    @pl.when(s == 0)
    def _():
        start_all(0, 0)

    @pl.when(s + 1 < n_steps)
    def _():
        start_all(1 - slot, s + 1)

    def wait_body(i, c):
        for cp in copies(slot, s, i):
            cp.wait()
        return c
    lax.fori_loop(0, n_cp, wait_body, 0, unroll=8)

    lane = lax.broadcasted_iota(jnp.int32, (1, PAGE_SIZE), 1)
    for q in range(sps):
        qbd = qbd_ref[q]
        new_row = kvrow_ref[q]
        s_new = jnp.sum(qbd * new_row[:, :LANES], axis=1, keepdims=True)
        outs = []
        for h in range(N_KV_HEADS):
            bias, k_parts, v_parts = [], [], []
            for r in range(n_top):
                i = q * n_slots + h * n_top + r
                lo = meta_ref[s * n_cp + i] * SEL_BLOCK
                bias.append(jnp.where((lane >= lo) & (lane < lo + SEL_BLOCK), 0.0, -jnp.inf))
                k_parts.append(buf[slot, i, 0].astype(BF16))
                v_parts.append(buf[slot, i, 1].astype(BF16))
            hd = slice(h * HEAD_DIM, (h + 1) * HEAD_DIM)
            outs.append(_decode_attend(qbd[:, hd], jnp.concatenate(k_parts, axis=1), jnp.concatenate(v_parts, axis=1),
                                       jnp.concatenate(bias, axis=1), s_new, new_row[:, LANES:][:, hd]))
        o_ref[q] = _own_half(jnp.concatenate(outs, axis=1), gate_ref[q])


def _sample_sel(pg, meta, cache_t, qbd, kvrow, gate, n_seq, n_top):
    parts = 2 * N_KV_HEADS
    cache_q = cache_t.reshape(cache_t.shape[0] * parts, HEAD_DIM, PAGE_SIZE)
    head = jnp.repeat(jnp.arange(N_KV_HEADS, dtype=jnp.int32), n_top)[None, :]
    krow = (pg * parts + head).reshape(-1).astype(jnp.int32)
    vrow = (pg * parts + N_KV_HEADS + head).reshape(-1).astype(jnp.int32)
    sps = SEL_SEQS if n_seq % SEL_SEQS == 0 else 1
    blk = lambda *dims: pl.BlockSpec((sps, *dims), lambda s, kr, vr, meta: (s, 0, 0))
    return pl.pallas_call(
        functools.partial(_sample_sel_body, n_top=n_top, sps=sps),
        grid_spec=pltpu.PrefetchScalarGridSpec(
            num_scalar_prefetch=3,
            grid=(n_seq // sps,),
            in_specs=[pl.BlockSpec(memory_space=pl.ANY), blk(N_Q_HEADS, LANES), blk(1, KV_COLS), blk(N_Q_HEADS, LANES)],
            out_specs=blk(N_Q_HEADS, LANES),
            scratch_shapes=[pltpu.VMEM((2, sps * N_KV_HEADS * n_top, 2, HEAD_DIM, PAGE_SIZE), F32), pltpu.SemaphoreType.DMA((2,))],
        ),
        out_shape=jax.ShapeDtypeStruct((n_seq, N_Q_HEADS, LANES), F32),
        compiler_params=pltpu.CompilerParams(dimension_semantics=("arbitrary",), vmem_limit_bytes=VMEM_LIMIT),
        name="sample_sel",
    )(krow, vrow, meta, cache_q, qbd, kvrow, gate)


WIN_SEQS = 8


def _sample_win_body(win_ref, qbd_ref, kvrow_ref, kvcol_ref, gate_ref, o_ref, wout_ref):
    lane = lax.broadcasted_iota(jnp.int32, (1, WINDOW), 1)
    for i in range(win_ref.shape[0]):
        w = win_ref[i]
        wout_ref[i] = jnp.where(lane == WINDOW - 1, kvcol_ref[i], pltpu.roll(w, WINDOW - 1, 1))
        qbd, new_row = qbd_ref[i], kvrow_ref[i]
        s_new = jnp.sum(qbd * new_row[:, :LANES], axis=1, keepdims=True)
        o = _decode_attend(qbd, w[:LANES, :].astype(BF16), w[LANES:, :].astype(BF16), None, s_new, new_row[:, LANES:])
        o_ref[i] = _own_half(o, gate_ref[i])


def _sample_win(win_t, qbd, kvrow, kvcol, gate):
    n_seq = win_t.shape[0]
    sps = WIN_SEQS if n_seq % WIN_SEQS == 0 else 1
    blk = lambda *dims: pl.BlockSpec((sps, *dims), lambda s: (s, 0, 0))
    return pl.pallas_call(
        _sample_win_body,
        grid=(n_seq // sps,),
        in_specs=[blk(KV_COLS, WINDOW), blk(N_Q_HEADS, LANES), blk(1, KV_COLS), blk(KV_COLS, 1), blk(N_Q_HEADS, LANES)],
        out_specs=[blk(N_Q_HEADS, LANES), blk(KV_COLS, WINDOW)],
        out_shape=[jax.ShapeDtypeStruct((n_seq, N_Q_HEADS, LANES), F32), jax.ShapeDtypeStruct(win_t.shape, F32)],
        compiler_params=pltpu.CompilerParams(dimension_semantics=("arbitrary",), vmem_limit_bytes=VMEM_LIMIT),
        name="sample_win",
    )(win_t, qbd, kvrow, kvcol, gate)


def _prep_inproj_weight(w_in):
    sizes = (NSA_WIDTH, 3 * KV_COLS, 3 * N_Q_HEADS, NSA_WIDTH, D_RNN, D_RNN, 2 * D_MODEL)
    offs = np.concatenate([[0], np.cumsum(sizes)])
    w_in = w_in.astype(BF16)
    q, kv, gn, zn, xr, zr, gm = [w_in[:, offs[i]:offs[i + 1]] for i in range(len(sizes))]
    zn = zn.reshape(D_MODEL, N_KV_HEADS, GROUP, HEAD_DIM).transpose(0, 2, 1, 3).reshape(D_MODEL, NSA_WIDTH)
    pad = jnp.zeros((D_MODEL, LANES - 3 * N_Q_HEADS), BF16)
    return jnp.concatenate([q, kv, zn, xr, zr, gm, gn, pad], axis=1)


def _prep_compress(pe, w1, w2):
    eye = jnp.eye(N_KV_HEADS, dtype=F32)
    bd = lambda w: jnp.einsum('jde,hk->jhdke', w, eye).reshape(CMP_STRIDE, LANES, LANES)
    wbd = jnp.concatenate([bd(w1[:CMP_STRIDE]), bd(w1[CMP_STRIDE:])], axis=2).astype(BF16)
    w2bd = jnp.einsum('ed,hk->hekd', w2, eye).reshape(LANES, LANES).astype(BF16)
    pe_col = pe.reshape(CMP_BLOCK * HEAD_DIM, 1)
    w1t = jnp.tile(w1.reshape(CMP_BLOCK * HEAD_DIM, CMP_HIDDEN), (1, N_KV_HEADS))
    return wbd, w2bd, pe_col, w1t


def _prep_rnn(conv_w, conv_b, w_ra, b_ra, w_rx, b_rx, lam):
    def pairs(w):
        w = w.reshape(RNN_HEADS // 2, 2, RNN_HD, RNN_HD)
        eye = jnp.eye(2, dtype=F32)
        return jnp.einsum('shij,hk->shikj', w, eye).reshape(RNN_HEADS // 2, LANES, LANES).astype(BF16)
    return (conv_w, conv_b.reshape(1, D_RNN), pairs(w_ra), b_ra.reshape(1, D_RNN), pairs(w_rx), b_rx.reshape(1, D_RNN),
            lam.reshape(1, D_RNN))


def _feature_major(cache):
    n, r = cache.shape[:2]
    return cache.transpose(0, 2, 3, 4, 1).reshape(n, KV_COLS, r)


def kernel(x_prompt, x_sample, cache_kv_cmp, cache_kv_sel, cache_kv_win, state_conv, state_h, page_table, g_norm, w_in, g_q, g_kc, g_ks, g_kw, pe_k, w1_k, w2_k, pe_v, w1_v, w2_v, conv_w, conv_b, w_ra, b_ra, w_rx, b_rx, lam, w_pa, w_pb, w_out):
    b, t, _ = x_prompt.shape
    n_seq = x_sample.shape[0]
    n_pages = page_table.shape[1]
    past = n_pages * PAGE_SIZE
    assert x_sample.shape[1] == 1 and cache_kv_win.shape[1] == WINDOW and past >= WINDOW
    assert t % SEL_TK == 0 and t // SEL_BLOCK <= HEAD_DIM and t >= WIN_KEYS and n_pages % 2 == 0

    wp = _prep_inproj_weight(w_in)
    seg = _seg01()
    tile2 = lambda g: jnp.tile(g.reshape(1, HEAD_DIM), (1, 2))
    gq, gks, gkw, gkc = tile2(g_q), tile2(g_ks), tile2(g_kw), tile2(g_kc)
    gnorm = g_norm.reshape(1, D_MODEL)
    cw = (*_prep_compress(pe_k, w1_k, w2_k), *_prep_compress(pe_v, w1_v, w2_v))
    rw = _prep_rnn(conv_w, conv_b, w_ra, b_ra, w_rx, b_rx, lam)
    wpa = w_pa.reshape(N_KV_HEADS, GROUP, HEAD_DIM, D_MODEL).transpose(1, 0, 2, 3).reshape(GROUP, LANES, D_MODEL).astype(BF16)
    wpb, wo = w_pb.astype(BF16), w_out.astype(BF16)

    tm = 512
    tok = np.arange(t)
    oh = jnp.asarray((tok[:, None] // SEL_BLOCK) == (np.arange(LANES)[None, :] % HEAD_DIM), BF16)
    xp2 = x_prompt.reshape(b * t, D_MODEL)
    (kvc, kvct, kvst, kvwt, qxt, ka, kw, vst, vwt, gatet, zn, xr, zr, gm) = _inproj(
        xp2, gnorm, wp, _rope_tables(tok), oh, gq, gks, gkw, seg, tm, t, True)
    n_sub = t // CMP_STRIDE
    ns = t // SEL_BLOCK
    mimp = jnp.asarray(_imp_matrix(n_sub, ns, n_sub - 1, ns).T, BF16)
    ctabs = _rope_tables(np.arange(n_sub) * CMP_STRIDE + (CMP_BLOCK - 1))
    o_attn = _prompt_attn(kvc, qxt, gatet, ka, vst, kw, vwt, cw, gkc, ctabs, seg, mimp, b, t)
    rb, conv_tail, h_last = _prompt_rnn(xr, zr, rw, b, t)
    y_prompt = _outproj(xp2, [o_attn], zn, rb, gm, wpa, wpb, wo, tm).reshape(b, t, D_MODEL)
    kv5t = lambda a: a.reshape(a.shape[0], 2, N_KV_HEADS, HEAD_DIM, a.shape[2]).transpose(0, 4, 1, 2, 3)
    kv_cmp_prompt, kv_sel_prompt = kv5t(kvct), kv5t(kvst)
    kv_win_prompt = kv5t(kvwt[:, :, t - min(WINDOW, t):])
    conv_prompt = conv_tail[:, 8 - (CONV_W - 1):]
    h_prompt = h_last.reshape(b, D_RNN)

    xs2 = x_sample.reshape(n_seq, D_MODEL)
    oh_s = jnp.zeros((n_seq, LANES), BF16)
    (kvc_s, kvs_s, kvw_s, qf_s, gate_s, zn_s, xr_s, zr_s, gm_s) = _inproj(
        xs2, gnorm, wp, _rope_tables(np.full((n_seq,), past)), oh_s, gq, gks, gkw, seg, n_seq, n_seq, False)
    kv5 = lambda a, rows: a.reshape(-1, rows, 2, N_KV_HEADS, HEAD_DIM)
    eye = jnp.eye(N_KV_HEADS, dtype=F32)
    q4 = qf_s.reshape(n_seq, N_KV_HEADS, GROUP, HEAD_DIM)
    qbd = (q4[:, :, :, None, :] * eye[None, :, None, :, None]).reshape(n_seq, N_Q_HEADS, LANES)
    gates3 = gate_s[:, :3 * N_Q_HEADS].reshape(n_seq, N_Q_HEADS, 3)
    gate_b = lambda k: jnp.broadcast_to(gates3[:, :, k:k + 1], (n_seq, N_Q_HEADS, LANES))

    n_sub_s = past // CMP_STRIDE
    ns_s = past // SEL_BLOCK + 1
    n_top_s = min(N_SEL, ns_s)
    nsp = -(-ns_s // LANES) * LANES
    mimp_s = jnp.asarray(_imp_matrix(n_sub_s, nsp, n_sub_s - 1, ns_s), BF16)
    ctabs_s = _rope_tables(np.arange(n_sub_s) * CMP_STRIDE + (CMP_BLOCK - 1))
    oc_s, idx = _sample_cmp(page_table.reshape(-1), _feature_major(cache_kv_cmp), qbd, gate_b(0), cw, gkc, ctabs_s, seg, mimp_s,
                            n_seq, n_pages, past, ns_s, n_top_s)
    blocks = idx[:, :N_KV_HEADS, :n_top_s]
    per_page = PAGE_SIZE // SEL_BLOCK
    in_cache = blocks < past // SEL_BLOCK
    pg = jnp.take_along_axis(page_table, jnp.minimum(blocks // per_page, n_pages - 1).reshape(n_seq, -1), axis=1)
    meta = jnp.where(in_cache, blocks % per_page, per_page).reshape(-1).astype(jnp.int32)
    o_sel_s = _sample_sel(pg.astype(jnp.int32), meta, _feature_major(cache_kv_sel), qbd,
                          kvs_s.reshape(n_seq, 1, KV_COLS), gate_b(1), n_seq, n_top_s)
    o_win_s, win_new = _sample_win(_feature_major(cache_kv_win), qbd, kvw_s.reshape(n_seq, 1, KV_COLS),
                                   kvw_s.reshape(n_seq, KV_COLS, 1), gate_b(2))
    sc_t = state_conv.transpose(1, 0, 2)
    rb_s, conv_new, h_sample = _sample_rnn(xr_s, zr_s, sc_t, state_h, rw)
    to_o = lambda o: (o[:, :GROUP] + o[:, GROUP:]).transpose(1, 0, 2)[None]
    y_sample = _outproj(xs2, [to_o(oc_s), to_o(o_sel_s), to_o(o_win_s)], zn_s, rb_s, gm_s, wpa, wpb, wo, n_seq).reshape(n_seq, 1, D_MODEL)
    kv_cmp_sample, kv_sel_sample = kv5(kvc_s, 1), kv5(kvs_s, 1)
    kv_win_sample = win_new.reshape(n_seq, 2, N_KV_HEADS, HEAD_DIM, WINDOW).transpose(0, 4, 1, 2, 3)
    conv_sample = conv_new.transpose(1, 0, 2)

    return (y_prompt, y_sample, kv_cmp_prompt, kv_cmp_sample, kv_sel_prompt, kv_sel_sample,
            kv_win_prompt, kv_win_sample, conv_prompt, conv_sample, h_prompt, h_sample)
```
